```python
import functools
import jax, jax.numpy as jnp
from jax import lax
import numpy as np

D_MODEL = 1024
BATCH = 8
SEQ = 4096
DEPTH = 1
DEC_BATCH = 128
DEC_SEQ = 1
PAST_LEN = 8192
PAGE_SIZE = 128

H_RET = 8
HD_RET = D_MODEL // 16
RET_CHUNK = 128
RET_THETA = 10000.0
H_NSA = 8
G_NSA = 2
R_NSA = H_NSA // G_NSA
HD_NSA = D_MODEL // 16
CMP_BLK = 64
N_SEL = 16
WINDOW = 512
WIN_QBLK = 128
SEL_QCH = 32
SEL_FORCE = 1.0e4
ROPE_THETA = 500000.0
ROPE_DIMS = HD_NSA // 4
N_MEM = 256
H_MEM = 4
HD_MEM = D_MODEL // 8
PEER_KEYS = 128
PEER_HEADS = 8
PEER_DKEY = 128
PEER_TOPK = 16
PEER_CHUNK = 256
N_EXPERTS = PEER_KEYS * PEER_KEYS
EPS = 1e-6

D_RET = H_RET * HD_RET
D_NSA = H_NSA * HD_NSA
D_MIX = D_RET + D_NSA
D_KV = G_NSA * HD_NSA
D_IN = 4 * D_RET + D_NSA + 6 * D_KV + 3 * H_NSA

kernel_name = 'hymba_retnet_nsa_peer_step'


def rms_norm(x, g):
    xf = x.astype(jnp.float32)
    y = xf * lax.rsqrt(jnp.mean(xf * xf, axis=-1, keepdims=True) + EPS)
    return (y * g.astype(jnp.float32)).astype(x.dtype)


def rotary(x, pos, n_rot, theta):
    half = n_rot // 2
    inv = theta ** (-jnp.arange(half, dtype=jnp.float32) / half)
    ang = pos.astype(jnp.float32)[:, None] * inv[None, :]
    cos = jnp.cos(ang)[None, :, None, :]
    sin = jnp.sin(ang)[None, :, None, :]
    xf = x.astype(jnp.float32)
    x1, x2 = xf[..., :half], xf[..., half:n_rot]
    out = jnp.concatenate([x1 * cos - x2 * sin, x1 * sin + x2 * cos, xf[..., n_rot:]], axis=-1)
    return out.astype(x.dtype)


def masked_softmax(s, mask):
    s = jnp.where(mask, s.astype(jnp.float32), -jnp.inf)
    m = jnp.max(s, axis=-1, keepdims=True)
    m = jnp.where(jnp.isfinite(m), m, 0.0)
    p = jnp.exp(s - m)
    return p / jnp.maximum(jnp.sum(p, axis=-1, keepdims=True), 1e-30)


def mixer_projections(hn, pos, w_in):
    B, L, _ = hn.shape
    sizes = [D_RET] * 4 + [D_NSA] + [D_KV] * 6 + [3 * H_NSA]
    cuts = np.cumsum(sizes)[:-1].tolist()
    rq, rk, rv, rg, nq, ck, cv, sk, sv, wk, wv, ng = jnp.split(hn @ w_in, cuts, axis=-1)
    ret_heads = lambda t: t.reshape(B, L, H_RET, HD_RET)
    rq = rotary(ret_heads(rq), pos, HD_RET, RET_THETA).astype(jnp.float32)
    rk = rotary(ret_heads(rk), pos, HD_RET, RET_THETA).astype(jnp.float32) * (HD_RET ** -0.5)
    rv = ret_heads(rv).astype(jnp.float32)
    nq = rotary(nq.reshape(B, L, H_NSA, HD_NSA), pos, ROPE_DIMS, ROPE_THETA)
    nq = nq.reshape(B, L, G_NSA, R_NSA, HD_NSA)

    def kv_rows(t, rotate):
        t = t.reshape(B, L, G_NSA, HD_NSA)
        if rotate:
            t = rotary(t, pos, ROPE_DIMS, ROPE_THETA)
        return t.transpose(0, 2, 1, 3)

    ck, sk, wk = (kv_rows(t, True) for t in (ck, sk, wk))
    cv, sv, wv = (kv_rows(t, False) for t in (cv, sv, wv))
    gates = jax.nn.sigmoid(ng.astype(jnp.float32)).reshape(B, L, G_NSA, R_NSA, 3)
    return rq, rk, rv, rg, nq, ck, cv, sk, sv, wk, wv, gates


def retention(q, k, v, s0):
    B, L, H, d = q.shape
    C = RET_CHUNK if L % RET_CHUNK == 0 else L
    nC = L // C
    lg = jnp.log(1.0 - 2.0 ** (-5.0 - jnp.arange(H, dtype=jnp.float32)))
    idx = jnp.arange(C, dtype=jnp.float32)
    diff = idx[:, None] - idx[None, :]
    dmat = jnp.where(diff >= 0, jnp.exp(lg[:, None, None] * jnp.maximum(diff, 0.0)), 0.0)
    xi = jnp.exp(lg[None, :] * (idx[:, None] + 1.0))
    zeta = jnp.exp(lg[:, None] * (C - 1.0 - idx[None, :]))
    g_c = jnp.exp(lg * C)
    to_chunks = lambda t: t.reshape(B, nC, C, H, d).transpose(1, 0, 2, 3, 4)

    def step(S, inp):
        qc, kc, vc = inp
        att = jnp.einsum('bihd,bjhd->bhij', qc, kc) * dmat[None]
        inner = jnp.einsum('bhij,bjhe->bihe', att, vc)
        cross = jnp.einsum('bihd,bhde->bihe', qc, S) * xi[None, :, :, None]
        S = S * g_c[None, :, None, None] + jnp.einsum('bjhd,bjhe,hj->bhde', kc, vc, zeta)
        return S, inner + cross

    S, o = lax.scan(step, s0, (to_chunks(q), to_chunks(k), to_chunks(v)))
    return o.transpose(1, 0, 2, 3, 4).reshape(B, L, H, d), S


def head_group_norm(o, g):
    B, L, H, d = o.shape
    mu = jnp.mean(o, axis=-1, keepdims=True)
    var = jnp.mean(jnp.square(o - mu), axis=-1, keepdims=True)
    return ((o - mu) * lax.rsqrt(var + EPS)).reshape(B, L, H * d) * g.astype(jnp.float32)


def compress(rows, w, b):
    return jnp.einsum('...jd,jde->...e', rows, w) + b


def cmp_attend(q, q_pos, ck, cv):
    nb = ck.shape[2]
    s = jnp.einsum('bqgrd,bgnd->bqgrn', q, ck).astype(jnp.float32) * (HD_NSA ** -0.5)
    blk_end = jnp.arange(nb) * CMP_BLK + CMP_BLK - 1
    vis = blk_end[None, :] <= q_pos[:, None]
    p = masked_softmax(s, vis[None, :, None, None, :])
    o = jnp.einsum('bqgrn,bgnd->bqgrd', p.astype(cv.dtype), cv)
    return o, jnp.sum(p, axis=3), vis


def select_blocks(imp, vis, q_pos):
    nb = imp.shape[-1]
    n_sel = min(N_SEL, nb)
    blk = jnp.arange(nb)
    forced = (blk[None, :] == 0) | (blk[None, :] == (q_pos[:, None] // CMP_BLK))
    score = jnp.where(forced[None, :, None, :], SEL_FORCE, jnp.where(vis[None, :, None, :], imp, -1.0))
    top, idx = lax.top_k(score, n_sel)
    return idx, top >= 0.0


def sel_attend(q, q_pos, idx, valid, gather):
    kb, vb = gather(idx)
    kpos = idx[..., None] * CMP_BLK + jnp.arange(CMP_BLK)
    mask = valid[..., None] & (kpos <= q_pos[None, :, None, None, None])
    s = jnp.einsum('bqgrd,bqgnkd->bqgrnk', q, kb).astype(jnp.float32) * (HD_NSA ** -0.5)
    B, Lq, G, R, n, K = s.shape
    p = masked_softmax(s.reshape(B, Lq, G, R, n * K), mask.reshape(B, Lq, G, 1, n * K))
    return jnp.einsum('bqgrm,bqgmd->bqgrd', p.astype(vb.dtype), vb.reshape(B, Lq, G, n * K, HD_NSA))


def win_attend(q, q_pos, k, v, k_pos):
    s = jnp.einsum('bqgrd,bgkd->bqgrk', q, k).astype(jnp.float32) * (HD_NSA ** -0.5)
    dist = q_pos[:, None] - k_pos[None, :]
    mask = (dist >= 0) & (dist < WINDOW) & (k_pos[None, :] >= 0)
    p = masked_softmax(s, mask[None, :, None, None, :])
    return jnp.einsum('bqgrk,bgkd->bqgrd', p.astype(v.dtype), v)


def nsa_prompt(nq, ck, cv, sk, sv, wk, wv, cw, cb, pos):
    B, L = nq.shape[:2]
    nb = L // CMP_BLK
    blocks = lambda t: t.reshape(B, G_NSA, nb, CMP_BLK, HD_NSA)
    cK = compress(blocks(ck), cw[0], cb[0])
    cV = compress(blocks(cv), cw[1], cb[1])
    o_cmp, imp, vis = cmp_attend(nq, pos, cK, cV)
    idx, valid = select_blocks(imp, vis, pos)
    kb, vb = blocks(sk), blocks(sv)
    bi = jnp.arange(B)[:, None, None, None]
    gi = jnp.arange(G_NSA)[None, None, :, None]
    gather = lambda i: (kb[bi, gi, i], vb[bi, gi, i])
    nqc = L // SEL_QCH
    chunk = lambda t: jnp.moveaxis(t.reshape((B, nqc, SEL_QCH) + t.shape[2:]), 1, 0)
    o_slc = lax.map(lambda a: sel_attend(a[0], a[1], a[2], a[3], gather),
                    (chunk(nq), pos.reshape(nqc, SEL_QCH), chunk(idx), chunk(valid)))
    o_slc = jnp.moveaxis(o_slc, 0, 1).reshape(B, L, G_NSA, R_NSA, HD_NSA)
    nqb = L // WIN_QBLK
    span = WINDOW + WIN_QBLK
    lpad = lambda t: jnp.pad(t, ((0, 0), (0, 0), (WINDOW, 0), (0, 0)))
    wkp, wvp = lpad(wk), lpad(wv)
    qb = jnp.moveaxis(nq.reshape(B, nqb, WIN_QBLK, G_NSA, R_NSA, HD_NSA), 1, 0)

    def band(args):
        i, qi = args
        start = i * WIN_QBLK
        kk = lax.dynamic_slice_in_dim(wkp, start, span, axis=2)
        vv = lax.dynamic_slice_in_dim(wvp, start, span, axis=2)
        return win_attend(qi, start + jnp.arange(WIN_QBLK), kk, vv, start - WINDOW + jnp.arange(span))

    o_win = jnp.moveaxis(lax.map(band, (jnp.arange(nqb), qb)), 0, 1).reshape(B, L, G_NSA, R_NSA, HD_NSA)
    to_pages = lambda t: t.reshape(B, G_NSA, L // PAGE_SIZE, PAGE_SIZE, HD_NSA).transpose(0, 2, 1, 3, 4)
    keep = min(WINDOW, L)
    new_state = (to_pages(ck), to_pages(cv), to_pages(sk), to_pages(sv), wk[:, :, L - keep:], wv[:, :, L - keep:])
    return o_cmp, o_slc, o_win, new_state


def nsa_sample(nq, ck, cv, sk, sv, wk, wv, cw, cb, pos, page_table, layer,
               pool_ck, pool_cv, pool_sk, pool_sv, win_k, win_v):
    DB, LS = nq.shape[:2]
    n_pages = page_table.shape[1]
    bpp = PAGE_SIZE // CMP_BLK
    nb_past = n_pages * bpp
    pad_n = (-LS) % CMP_BLK
    nnb = (LS + pad_n) // CMP_BLK

    def new_blocks(t):
        t = jnp.pad(t, ((0, 0), (0, 0), (0, pad_n), (0, 0)))
        return t.reshape(DB, G_NSA, nnb, CMP_BLK, HD_NSA)

    def past_compressed(pool, w, b):
        rows = pool[layer, page_table]
        rows = rows.reshape(DB, n_pages, G_NSA, bpp, CMP_BLK, HD_NSA)
        c = compress(rows, w, b)
        return c.transpose(0, 2, 1, 3, 4).reshape(DB, G_NSA, nb_past, HD_NSA)

    cK = jnp.concatenate([past_compressed(pool_ck, cw[0], cb[0]), compress(new_blocks(ck), cw[0], cb[0])], axis=2)
    cV = jnp.concatenate([past_compressed(pool_cv, cw[1], cb[1]), compress(new_blocks(cv), cw[1], cb[1])], axis=2)
    o_cmp, imp, vis = cmp_attend(nq, pos, cK, cV)
    idx, valid = select_blocks(imp, vis, pos)
    bi = jnp.arange(DB)[:, None, None, None]
    gi = jnp.arange(G_NSA)[None, None, :, None]
    offs = jnp.arange(CMP_BLK)

    def block_source(pool, new):
        nblocks = new_blocks(new)

        def get(i):
            ip = jnp.minimum(i, nb_past - 1)
            page = page_table[bi, ip // bpp]
            row = ((ip % bpp) * CMP_BLK)[..., None] + offs
            past_rows = pool[layer, page[..., None], gi[..., None], row]
            new_rows = nblocks[bi, gi, jnp.clip(i - nb_past, 0, nnb - 1)]
            return jnp.where((i < nb_past)[..., None, None], past_rows, new_rows)
        return get

    get_k, get_v = block_source(pool_sk, sk), block_source(pool_sv, sv)
    o_slc = sel_attend(nq, pos, idx, valid, lambda i: (get_k(i), get_v(i)))
    wb = win_k.shape[2]
    kw = jnp.concatenate([win_k, wk], axis=2)
    vw = jnp.concatenate([win_v, wv], axis=2)
    o_win = win_attend(nq, pos, kw, vw, PAST_LEN - wb + jnp.arange(wb + LS))
    keep = min(WINDOW, wb + LS)
    new_state = (ck, cv, sk, sv, kw[:, :, wb + LS - keep:], vw[:, :, wb + LS - keep:])
    return o_cmp, o_slc, o_win, new_state


def mixer_output(o_ret, rg, gn_g, o_cmp, o_slc, o_win, gates, w_out):
    B, L = rg.shape[:2]
    y_ret = jax.nn.silu(rg.astype(jnp.float32)) * head_group_norm(o_ret, gn_g)
    o_nsa = gates[..., 0:1] * o_cmp + gates[..., 1:2] * o_slc + gates[..., 2:3] * o_win
    cat = jnp.concatenate([y_ret, o_nsa.reshape(B, L, D_NSA)], axis=-1)
    return cat.astype(w_out.dtype) @ w_out


def mem_kv(mem, g, w_mk, w_mv):
    Bm, M, _ = mem.shape
    m = rms_norm(mem, g)
    return (m @ w_mk).reshape(Bm, M, H_MEM, HD_MEM), (m @ w_mv).reshape(Bm, M, H_MEM, HD_MEM)


def mem_attend(hn, mk, mv, w_mq, w_mo):
    B, L, _ = hn.shape
    q = (hn @ w_mq).reshape(B, L, H_MEM, HD_MEM)
    s = jnp.einsum('blhd,bmhd->bhlm', q, mk).astype(jnp.float32) * (HD_MEM ** -0.5)
    p = jax.nn.softmax(s, axis=-1).astype(mv.dtype)
    o = jnp.einsum('bhlm,bmhd->blhd', p, mv).reshape(B, L, H_MEM * HD_MEM)
    return o @ w_mo


def peer_ffn(hn, wq, subkeys, u, v):
    B, L, D = hn.shape
    n = B * L
    pad = (-n) % PEER_CHUNK
    t = jnp.pad(hn.reshape(n, D), ((0, pad), (0, 0))).reshape(-1, PEER_CHUNK, D)

    def chunk(xc):
        q = (xc @ wq).reshape(PEER_CHUNK, PEER_HEADS, 2, PEER_DKEY // 2)
        s = jnp.einsum('chpd,hpkd->chpk', q, subkeys).astype(jnp.float32)
        s1, i1 = lax.top_k(s[:, :, 0], PEER_TOPK)
        s2, i2 = lax.top_k(s[:, :, 1], PEER_TOPK)
        cand = (s1[..., :, None] + s2[..., None, :]).reshape(PEER_CHUNK, PEER_HEADS, PEER_TOPK * PEER_TOPK)
        cidx = (i1[..., :, None] * PEER_KEYS + i2[..., None, :]).reshape(PEER_CHUNK, PEER_HEADS, PEER_TOPK * PEER_TOPK)
        top, sel = lax.top_k(cand, PEER_TOPK)
        expert = jnp.take_along_axis(cidx, sel, axis=-1)
        g = jax.nn.softmax(top, axis=-1)
        act = jax.nn.gelu(jnp.einsum('cd,chkd->chk', xc, u[expert]).astype(jnp.float32))
        return jnp.einsum('chk,chkd->cd', (g * act).astype(v.dtype), v[expert])

    out = lax.map(chunk, t).reshape(-1, D)[:n]
    return out.reshape(B, L, D)


def trunk_layer(h, pos, s0, nsa_fn, mk, mv, lw):
    g_mix, w_in, gn_g, w_out, g_mem, w_mq, w_mo, g_ffn, p_wq, p_keys, p_u, p_v = lw
    hn = rms_norm(h, g_mix)
    rq, rk, rv, rg, nq, ck, cv, sk, sv, wk, wv, gates = mixer_projections(hn, pos, w_in)
    o_ret, s_new = retention(rq, rk, rv, s0)
    o_cmp, o_slc, o_win, nsa_state = nsa_fn(nq, ck, cv, sk, sv, wk, wv)
    h = h + mixer_output(o_ret, rg, gn_g, o_cmp, o_slc, o_win, gates, w_out).astype(h.dtype)
    h = h + mem_attend(rms_norm(h, g_mem), mk, mv, w_mq, w_mo).astype(h.dtype)
    h = h + peer_ffn(rms_norm(h, g_ffn), p_wq, p_keys, p_u, p_v).astype(h.dtype)
    return h, s_new, nsa_state


def setup_inputs(seed: int = 0) -> dict:
    key = jax.random.key(seed)
    keys = jax.random.split(key, 40)
    f32 = jnp.float32
    n_pages = PAST_LEN // PAGE_SIZE
    n_used = DEC_BATCH * n_pages
    n_phys = n_used + n_used // 4
    win_buf = min(WINDOW, PAST_LEN)
    nrm = lambda i, shape, scale: jax.random.normal(keys[i], shape, f32) * scale
    gain = lambda i, shape: 1.0 + 0.05 * jax.random.normal(keys[i], shape, f32)
    page_table = jax.random.permutation(keys[0], n_phys)[:n_used].reshape(DEC_BATCH, n_pages).astype(jnp.int32)
    pool = (DEPTH, n_phys, G_NSA, PAGE_SIZE, HD_NSA)
    return {
        'x_prompt': nrm(1, (BATCH, SEQ, D_MODEL), 1.0),
        'x_sample': nrm(2, (DEC_BATCH, DEC_SEQ, D_MODEL), 1.0),
        'mem_prompt': nrm(3, (BATCH, N_MEM, D_MODEL), 1.0),
        'state_ret': nrm(4, (DEPTH, DEC_BATCH, H_RET, HD_RET, HD_RET), 1.0),
        'cache_cmp_k': nrm(5, pool, 1.0),
        'cache_cmp_v': nrm(6, pool, 1.0),
        'cache_slc_k': nrm(7, pool, 1.0),
        'cache_slc_v': nrm(8, pool, 1.0),
        'cache_win_k': nrm(9, (DEPTH, DEC_BATCH, G_NSA, win_buf, HD_NSA), 1.0),
        'cache_win_v': nrm(10, (DEPTH, DEC_BATCH, G_NSA, win_buf, HD_NSA), 1.0),
        'cache_mem_k': nrm(11, (DEPTH, DEC_BATCH, N_MEM, H_MEM, HD_MEM), 1.0),
        'cache_mem_v': nrm(12, (DEPTH, DEC_BATCH, N_MEM, H_MEM, HD_MEM), 1.0),
        'page_table': page_table,
        'norm_mix_g': gain(13, (DEPTH, D_MODEL)),
        'w_in': nrm(14, (DEPTH, D_MODEL, D_IN), D_MODEL ** -0.5),
        'ret_gn_g': gain(15, (DEPTH, D_RET)),
        'cmp_w': nrm(16, (DEPTH, 2, CMP_BLK, HD_NSA, HD_NSA), (CMP_BLK * HD_NSA) ** -0.5),
        'cmp_b': nrm(17, (DEPTH, 2, HD_NSA), 0.02),
        'w_out': nrm(18, (DEPTH, D_MIX, D_MODEL), D_MIX ** -0.5),
        'norm_mem_g': gain(19, (DEPTH, D_MODEL)),
        'mem_norm_g': gain(20, (DEPTH, D_MODEL)),
        'w_mq': nrm(21, (DEPTH, D_MODEL, H_MEM * HD_MEM), D_MODEL ** -0.5),
        'w_mk': nrm(22, (DEPTH, D_MODEL, H_MEM * HD_MEM), D_MODEL ** -0.5),
        'w_mv': nrm(23, (DEPTH, D_MODEL, H_MEM * HD_MEM), D_MODEL ** -0.5),
        'w_mo': nrm(24, (DEPTH, H_MEM * HD_MEM, D_MODEL), (H_MEM * HD_MEM) ** -0.5),
        'norm_ffn_g': gain(25, (DEPTH, D_MODEL)),
        'peer_wq': nrm(26, (DEPTH, D_MODEL, PEER_HEADS * PEER_DKEY), D_MODEL ** -0.5),
        'peer_subkeys': nrm(27, (DEPTH, PEER_HEADS, 2, PEER_KEYS, PEER_DKEY // 2), (PEER_DKEY // 2) ** -0.5),
        'peer_u': nrm(28, (DEPTH, N_EXPERTS, D_MODEL), D_MODEL ** -0.5),
        'peer_v': nrm(29, (DEPTH, N_EXPERTS, D_MODEL), 0.5),
        'norm_final_g': gain(30, (D_MODEL,)),
    }


def reference(x_prompt, x_sample, mem_prompt, state_ret, cache_cmp_k, cache_cmp_v, cache_slc_k, cache_slc_v,
              cache_win_k, cache_win_v, cache_mem_k, cache_mem_v, page_table, norm_mix_g, w_in, ret_gn_g,
              cmp_w, cmp_b, w_out, norm_mem_g, mem_norm_g, w_mq, w_mk, w_mv, w_mo, norm_ffn_g,
              peer_wq, peer_subkeys, peer_u, peer_v, norm_final_g):
    B, L, _ = x_prompt.shape
    DB, LS, _ = x_sample.shape
    pos_p = jnp.arange(L, dtype=jnp.int32)
    pos_s = PAST_LEN + jnp.arange(LS, dtype=jnp.int32)
    hp, hs = x_prompt, x_sample
    p_ret, p_ck, p_cv, p_sk, p_sv, p_wk, p_wv, p_mk, p_mv = [], [], [], [], [], [], [], [], []
    s_ret, s_ck, s_cv, s_sk, s_sv, s_wk, s_wv = [], [], [], [], [], [], []
    for l in range(DEPTH):
        lw = (norm_mix_g[l], w_in[l], ret_gn_g[l], w_out[l], norm_mem_g[l], w_mq[l], w_mo[l],
              norm_ffn_g[l], peer_wq[l], peer_subkeys[l], peer_u[l], peer_v[l])
        mk, mv = mem_kv(mem_prompt, mem_norm_g[l], w_mk[l], w_mv[l])
        nsa_p = functools.partial(nsa_prompt, cw=cmp_w[l], cb=cmp_b[l], pos=pos_p)
        s0 = jnp.zeros((B, H_RET, HD_RET, HD_RET), jnp.float32)
        hp, sr, st = trunk_layer(hp, pos_p, s0, nsa_p, mk, mv, lw)
        p_ret.append(sr)
        p_mk.append(mk)
        p_mv.append(mv)
        for lst, a in zip((p_ck, p_cv, p_sk, p_sv, p_wk, p_wv), st):
            lst.append(a)
        nsa_s = functools.partial(nsa_sample, cw=cmp_w[l], cb=cmp_b[l], pos=pos_s, page_table=page_table,
                                  layer=l, pool_ck=cache_cmp_k, pool_cv=cache_cmp_v, pool_sk=cache_slc_k,
                                  pool_sv=cache_slc_v, win_k=cache_win_k[l], win_v=cache_win_v[l])
        hs, sr, st = trunk_layer(hs, pos_s, state_ret[l].astype(jnp.float32), nsa_s,
                                 cache_mem_k[l], cache_mem_v[l], lw)
        s_ret.append(sr)
        for lst, a in zip((s_ck, s_cv, s_sk, s_sv, s_wk, s_wv), st):
            lst.append(a)
    y_prompt = rms_norm(hp, norm_final_g)
    y_sample = rms_norm(hs, norm_final_g)
    stk = lambda xs: jnp.stack(xs, axis=0)
    return (y_prompt, y_sample, stk(p_ret), stk(p_ck), stk(p_cv), stk(p_sk), stk(p_sv), stk(p_wk), stk(p_wv),
            stk(p_mk), stk(p_mv), stk(s_ret), stk(s_ck), stk(s_cv), stk(s_sk), stk(s_sv), stk(s_wk), stk(s_wv))
```

```python
import functools

import numpy as np
import jax
import jax.numpy as jnp
from jax import lax
from jax.experimental import pallas as pl
from jax.experimental.pallas import tpu as pltpu

F32 = jnp.float32
BF16 = jnp.bfloat16

D_MODEL = 1024
PAGE_SIZE = 128
H_RET = 8
HD_RET = 64
RET_CHUNK = 128
RET_THETA = 10000.0
H_NSA = 8
G_NSA = 2
R_NSA = H_NSA // G_NSA
HD_NSA = 64
CMP_BLK = 64
N_SEL = 16
WINDOW = 512
SEL_FORCE = 1.0e4
ROPE_THETA = 500000.0
ROPE_DIMS = HD_NSA // 4
H_MEM = 4
HD_MEM = 128
PEER_KEYS = 128
PEER_HEADS = 8
PEER_DKEY = 128
PEER_TOPK = 16
EPS = 1e-6

D_RET = H_RET * HD_RET
D_NSA = H_NSA * HD_NSA
D_KV = G_NSA * HD_NSA
D_MEM = H_MEM * HD_MEM
NEG = -1.0e30

LANES = 128
VMEM_LIMIT = 56 * 1024 * 1024

C_RQ, C_RQR, C_RK, C_RKR, C_RV, C_RG, C_NQ, C_NQR = (i * 512 for i in range(8))
C_KV = 4096
C_KVR = C_KV + 6 * D_KV
C_NG = C_KVR + 3 * D_KV
N_PROJ = C_NG + LANES


def _cparams(sem):
    return pltpu.CompilerParams(dimension_semantics=sem, vmem_limit_bytes=VMEM_LIMIT)


def _rms(x, g):
    return x * lax.rsqrt(jnp.mean(x * x, axis=-1, keepdims=True) + EPS) * g


def _dot(a, b):
    return jnp.dot(a, b, preferred_element_type=F32)


def _dot_nt(a, b):
    return lax.dot_general(a, b, (((1,), (1,)), ((), ())), preferred_element_type=F32)


def _full(shape):
    n = len(shape)
    return pl.BlockSpec(shape, lambda *_: (0,) * n)


def _partner_cols(n_heads, hd, n_rot):
    half = n_rot // 2
    j = np.arange(hd)
    p = np.where(j < half, j + half, np.where(j < n_rot, j - half, j))
    return (np.arange(n_heads)[:, None] * hd + p[None, :]).reshape(-1)


def _prep_w_in(w_in):
    o = 0
    seg = {}
    for name, size in (("rq", D_RET), ("rk", D_RET), ("rv", D_RET), ("rg", D_RET), ("nq", D_NSA),
                       ("ck", D_KV), ("cv", D_KV), ("sk", D_KV), ("sv", D_KV), ("wk", D_KV), ("wv", D_KV),
                       ("ng", 3 * H_NSA)):
        seg[name] = (o, size)
        o += size
    cols = lambda n: np.arange(seg[n][0], seg[n][0] + seg[n][1])
    pr = _partner_cols(H_RET, HD_RET, HD_RET)
    pn = _partner_cols(H_NSA, HD_NSA, ROPE_DIMS)
    pk = _partner_cols(G_NSA, HD_NSA, ROPE_DIMS)
    order = np.concatenate([
        cols("rq"), cols("rq")[pr], cols("rk"), cols("rk")[pr], cols("rv"), cols("rg"),
        cols("nq"), cols("nq")[pn],
        cols("ck"), cols("cv"), cols("sk"), cols("sv"), cols("wk"), cols("wv"),
        cols("ck")[pk], cols("sk")[pk], cols("wk")[pk], cols("ng")])
    w = jnp.take(w_in, jnp.asarray(order, jnp.int32), axis=1)
    w = jnp.pad(w, ((0, 0), (0, N_PROJ - w.shape[1])))
    return w.astype(BF16)


def _rot_tables(pos, n_heads, hd, n_rot, theta):
    half = n_rot // 2
    inv = theta ** (-jnp.arange(half, dtype=F32) / half)
    ang = pos.astype(F32)[:, None] * inv[None, :]
    cos, sin = jnp.cos(ang), jnp.sin(ang)
    P = pos.shape[0]
    c = jnp.concatenate([cos, cos, jnp.ones((P, hd - n_rot), F32)], axis=1)
    s = jnp.concatenate([-sin, sin, jnp.zeros((P, hd - n_rot), F32)], axis=1)
    return jnp.tile(c, (1, n_heads)), jnp.tile(s, (1, n_heads))


def _proj_kernel(x_ref, g_ref, w_ref, cr_ref, sr_ref, cn_ref, sn_ref, ck_ref, sk_ref,
                 rq_ref, rk_ref, rv_ref, rg_ref, nq_ref, kv_ref, kvb_ref, gt_ref):
    hn = _rms(x_ref[...], g_ref[...]).astype(BF16)
    seg = lambda c0, n: _dot(hn, w_ref[:, c0:c0 + n])
    cr, sr = cr_ref[...], sr_ref[...]
    rq_ref[...] = (seg(C_RQ, 512) * cr + seg(C_RQR, 512) * sr).astype(BF16)
    rk_ref[...] = ((seg(C_RK, 512) * cr + seg(C_RKR, 512) * sr) * (HD_RET ** -0.5)).astype(BF16)
    rv_ref[...] = seg(C_RV, 512).astype(BF16)
    rg_ref[...] = seg(C_RG, 512)
    nq_ref[...] = (seg(C_NQ, 512) * cn_ref[...] + seg(C_NQR, 512) * sn_ref[...]).astype(BF16)
    ck, sk = ck_ref[...], sk_ref[...]
    for i in range(6):
        a = seg(C_KV + i * D_KV, D_KV)
        if i % 2 == 0:
            a = a * ck + seg(C_KVR + (i // 2) * D_KV, D_KV) * sk
        kv_ref[:, i * D_KV:(i + 1) * D_KV] = a
        kvb_ref[:, i * D_KV:(i + 1) * D_KV] = a.astype(BF16)
    z = seg(C_NG, LANES)
    gt_ref[...] = 1.0 / (1.0 + jnp.exp(-z))


def _proj_call(x2d, g, w_all, tabs, tm):
    n = x2d.shape[0]
    period = tabs[0].shape[0] // tm
    row = lambda w: pl.BlockSpec((tm, w), lambda i: (i, 0))
    tab = lambda w: pl.BlockSpec((tm, w), lambda i: (i % period, 0))
    outs = [(512, BF16), (512, BF16), (512, BF16), (512, F32), (512, BF16), (6 * D_KV, F32), (6 * D_KV, BF16),
            (LANES, F32)]
    return pl.pallas_call(
        _proj_kernel,
        grid=(n // tm,),
        in_specs=[row(D_MODEL), _full((1, D_MODEL)), _full((D_MODEL, N_PROJ)),
                  tab(512), tab(512), tab(512), tab(512), tab(D_KV), tab(D_KV)],
        out_specs=[row(w) for w, _ in outs],
        out_shape=[jax.ShapeDtypeStruct((n, w), dt) for w, dt in outs],
        compiler_params=_cparams(("parallel",)),
        name="proj",
    )(x2d, g.reshape(1, D_MODEL), w_all, *tabs)


def _ret_consts(C):
    lg = jnp.log(1.0 - 2.0 ** (-5.0 - jnp.arange(H_RET, dtype=F32)))
    idx = jnp.arange(C, dtype=F32)
    diff = idx[:, None] - idx[None, :]
    dmat = jnp.where(diff >= 0, jnp.exp(lg[:, None, None] * jnp.maximum(diff, 0.0)), 0.0)
    xi = jnp.exp(lg[None, :] * (idx[:, None] + 1.0))
    zeta = jnp.exp(lg[:, None] * (C - 1.0 - idx[None, :]))
    g_c = jnp.exp(lg * C)
    return dmat, xi, zeta, g_c


def _ret_kernel(gc_ref, q_ref, k_ref, v_ref, rg_ref, gn_ref, dmat_ref, xi_ref, zeta_ref,
                y_ref, st_ref, s_scr):
    c = pl.program_id(1)

    @pl.when(c == 0)
    def _():
        s_scr[...] = jnp.zeros_like(s_scr)

    q, k, v = q_ref[...], k_ref[...], v_ref[...]
    k_t = k.astype(F32).T
    outs = []
    for h in range(H_RET):
        sl = slice(h * HD_RET, (h + 1) * HD_RET)
        qh, kh, vh = q[:, sl], k[:, sl], v[:, sl]
        att = _dot_nt(qh, kh) * dmat_ref[h]
        inner = _dot(att.astype(BF16), vh)
        s_old = s_scr[h]
        cross = _dot(qh, s_old.astype(BF16)) * xi_ref[:, h:h + 1]
        o = inner + cross
        kz = (k_t[sl, :] * zeta_ref[h:h + 1, :]).astype(BF16)
        s_scr[h] = s_old * gc_ref[h] + _dot(kz, vh)
        mu = jnp.mean(o, axis=-1, keepdims=True)
        d = o - mu
        var = jnp.mean(d * d, axis=-1, keepdims=True)
        outs.append(d * lax.rsqrt(var + EPS))
    on = jnp.concatenate(outs, axis=-1)
    rg = rg_ref[...]
    silu = rg * (1.0 / (1.0 + jnp.exp(-rg)))
    y_ref[...] = (silu * (on * gn_ref[...])).astype(BF16)

    @pl.when(c == pl.num_programs(1) - 1)
    def _():
        st_ref[0] = s_scr[...]


def _ret_call(rq, rk, rv, rg, gn, B, L):
    C = RET_CHUNK
    nC = L // C
    dmat, xi, zeta, g_c = _ret_consts(C)
    blk = lambda: pl.BlockSpec((C, D_RET), lambda b, c: (b * nC + c, 0))
    return pl.pallas_call(
        _ret_kernel,
        grid=(B, nC),
        in_specs=[pl.BlockSpec(memory_space=pltpu.SMEM), blk(), blk(), blk(), blk(), _full((1, D_RET)),
                  _full((H_RET, C, C)), _full((C, H_RET)), _full((H_RET, C))],
        out_specs=[blk(), pl.BlockSpec((1, H_RET, HD_RET, HD_RET), lambda b, c: (b, 0, 0, 0))],
        out_shape=[jax.ShapeDtypeStruct((B * L, D_RET), BF16),
                   jax.ShapeDtypeStruct((B, H_RET, HD_RET, HD_RET), F32)],
        scratch_shapes=[pltpu.VMEM((H_RET, HD_RET, HD_RET), F32)],
        compiler_params=_cparams(("parallel", "arbitrary")),
        name="retention",
    )(g_c, rq, rk, rv, rg, gn.reshape(1, D_RET), dmat, xi, zeta)


def _ret1_kernel(q_ref, k_ref, v_ref, s_ref, gam_ref, rg_ref, gn_ref, y_ref, so_ref):
    q, k, v, s, gam = q_ref[...], k_ref[...], v_ref[...], s_ref[...], gam_ref[...]
    qk = jnp.sum(q * k, axis=1, keepdims=True)
    cross = jnp.sum(q * s, axis=1, keepdims=True) * gam
    o = qk * v + cross
    so_ref[...] = s * gam + k * v
    mu = jnp.mean(o, axis=-1, keepdims=True)
    d = o - mu
    var = jnp.mean(d * d, axis=-1, keepdims=True)
    rg = rg_ref[...]
    silu = rg * (1.0 / (1.0 + jnp.exp(-rg)))
    y_ref[...] = silu * (d * lax.rsqrt(var + EPS) * gn_ref[...])


def _ret1_call(rq, rk, rv, rg, gn, state):
    DB = rq.shape[0]
    n = DB * H_RET
    _, _, _, g_c = _ret_consts(1)
    col = lambda t: t.astype(F32).reshape(n, HD_RET, 1)
    rowv = lambda t: t.astype(F32).reshape(n, 1, HD_RET)
    gam = jnp.tile(g_c, DB).reshape(n, 1, 1)
    gn3 = jnp.tile(gn.reshape(H_RET, 1, HD_RET), (DB, 1, 1))
    tb = 128
    b3 = lambda a, b: pl.BlockSpec((tb, a, b), lambda i: (i, 0, 0))
    y, s_new = pl.pallas_call(
        _ret1_kernel,
        grid=(n // tb,),
        in_specs=[b3(HD_RET, 1), b3(HD_RET, 1), b3(1, HD_RET), b3(HD_RET, HD_RET), b3(1, 1), b3(1, HD_RET),
                  b3(1, HD_RET)],
        out_specs=[b3(1, HD_RET), b3(HD_RET, HD_RET)],
        out_shape=[jax.ShapeDtypeStruct((n, 1, HD_RET), F32),
                   jax.ShapeDtypeStruct((n, HD_RET, HD_RET), F32)],
        compiler_params=_cparams(("parallel",)),
        name="retention_step",
    )(col(rq), col(rk), rowv(rv), state.astype(F32).reshape(n, HD_RET, HD_RET), gam, rowv(rg), gn3)
    return y.reshape(DB, D_RET).astype(BF16), s_new.reshape(DB, H_RET, HD_RET, HD_RET)


NSA_TQ = 128
NSA_TK = 128


def _prep_cmp_w(cmp_w):
    z = jnp.zeros_like(cmp_w)
    top = jnp.concatenate([cmp_w, z], axis=-1)
    bot = jnp.concatenate([z, cmp_w], axis=-1)
    return jnp.concatenate([top, bot], axis=-2).astype(BF16)


def _iota(shape, dim):
    return lax.broadcasted_iota(jnp.int32, shape, dim)


def _group_queries(nqf, g, tq):
    lane_g = _iota((tq, LANES), 1) // HD_NSA
    parts = []
    for r in range(R_NSA):
        h = g * R_NSA + r
        x = nqf[:, (h // 2) * LANES:(h // 2 + 1) * LANES]
        if h % 2 != g:
            x = pltpu.roll(x, HD_NSA, axis=1)
        parts.append(jnp.where(lane_g == g, x, 0.0))
    return jnp.concatenate(parts, axis=0).astype(BF16)


def _select_blocks(score, nb, n_sel):
    s_t = score.T[:nb, :]
    n_i = _iota(s_t.shape, 0)
    rank = jnp.zeros(s_t.shape, F32)
    for m in range(nb):
        row = s_t[m:m + 1, :]
        ahead = (row > s_t) | ((row == s_t) & (n_i > m))
        rank = rank + jnp.where(ahead, 1.0, 0.0)
    sel_t = jnp.where((rank < n_sel) & (s_t >= 0.0), 1.0, 0.0)
    sel_t = jnp.concatenate([sel_t, jnp.zeros((LANES - nb, s_t.shape[1]), F32)], axis=0)
    return sel_t.T


def _flash_step(q, kt, vt, bias, carry):
    m, l, acc = carry
    tq, tk = bias.shape
    s = (_dot_nt(q, kt) * (HD_NSA ** -0.5)).reshape(R_NSA, tq, tk) + bias[None]
    m_new = jnp.maximum(m, jnp.max(s, axis=-1, keepdims=True))
    alpha = jnp.exp(m - m_new)
    p = jnp.where(s > 0.5 * NEG, jnp.exp(s - m_new), 0.0)
    l = alpha * l + jnp.sum(p, axis=-1, keepdims=True)
    pv = _dot(p.reshape(R_NSA * tq, tk).astype(BF16), vt).reshape(R_NSA, tq, LANES)
    return m_new, l, alpha * acc + pv


def _flash_init(tq):
    return (jnp.full((R_NSA, tq, 1), NEG, F32), jnp.zeros((R_NSA, tq, 1), F32),
            jnp.zeros((R_NSA, tq, LANES), F32))


def _nsa_kernel(nq_ref, ckf_ref, cvf_ref, sk_ref, sv_ref, wk_ref, wv_ref, gt_ref, cw_ref, cb_ref,
                o_ref, ck_scr, cv_scr, *, nb):
    qi = pl.program_id(1)
    tq, tk = NSA_TQ, NSA_TK
    n_sel = min(N_SEL, nb)

    @pl.when(qi == 0)
    def _compress():
        for which, (src, dst) in enumerate(((ckf_ref, ck_scr), (cvf_ref, cv_scr))):
            def body(j, acc):
                x = src[pl.ds(j, nb, stride=CMP_BLK), :].astype(BF16)
                return acc + _dot(x, cw_ref[which, j])
            acc = lax.fori_loop(0, CMP_BLK, body, jnp.zeros((nb, LANES), F32))
            dst[...] = jnp.zeros_like(dst)
            dst[0:nb, :] = (acc + cb_ref[which]).astype(BF16)

    t0 = qi * tq
    nqf = nq_ref[...].astype(F32)
    gt = gt_ref[...]
    pos = t0 + _iota((tq, 1), 0)
    blk = _iota((1, LANES), 1)
    vis = ((blk * CMP_BLK + CMP_BLK - 1) <= pos) & (blk < nb)
    forced = ((blk == 0) | (blk == pos // CMP_BLK)) & (blk < nb)
    vis_bias = jnp.where(vis, 0.0, NEG)
    lane = _iota((tq, LANES), 1)
    kcol = _iota((1, tk), 1)
    blk_row = _iota((LANES, tk), 0)
    blk_of_key = _iota((LANES, tk), 1) // CMP_BLK

    for g in range(G_NSA):
        qp = _group_queries(nqf, g, tq)

        sc = (_dot_nt(qp, ck_scr[...]) * (HD_NSA ** -0.5)).reshape(R_NSA, tq, LANES) + vis_bias[None]
        mc = jnp.max(sc, axis=-1, keepdims=True)
        pc = jnp.where(sc > 0.5 * NEG, jnp.exp(sc - mc), 0.0)
        pc = pc / jnp.maximum(jnp.sum(pc, axis=-1, keepdims=True), 1e-30)
        o_cmp = _dot(pc.reshape(R_NSA * tq, LANES).astype(BF16), cv_scr[...]).reshape(R_NSA, tq, LANES)
        imp = pc[0] + pc[1] + pc[2] + pc[3]

        score = jnp.where(forced, SEL_FORCE, jnp.where(vis, imp, -1.0))
        score = jnp.where(blk < nb, score, -2.0)
        sel = _select_blocks(score, nb, n_sel).astype(BF16)

        def sel_body(j, carry):
            expand = jnp.where(blk_row == blk_of_key + j * (tk // CMP_BLK), 1.0, 0.0).astype(BF16)
            chosen = _dot(sel, expand)
            kpos = j * tk + kcol
            bias = jnp.where((chosen > 0.5) & (kpos <= pos), 0.0, NEG)
            return _flash_step(qp, sk_ref[pl.ds(pl.multiple_of(j * tk, tk), tk), :],
                               sv_ref[pl.ds(pl.multiple_of(j * tk, tk), tk), :], bias, carry)

        _, l_s, a_s = lax.fori_loop(0, qi * (tq // tk) + tq // tk, sel_body, _flash_init(tq))
        o_slc = a_s / jnp.maximum(l_s, 1e-30)

        def win_body(j, carry):
            dist = pos - (j * tk + kcol)
            bias = jnp.where((dist >= 0) & (dist < WINDOW), 0.0, NEG)
            return _flash_step(qp, wk_ref[pl.ds(pl.multiple_of(j * tk, tk), tk), :],
                               wv_ref[pl.ds(pl.multiple_of(j * tk, tk), tk), :], bias, carry)

        j_lo = jnp.maximum(t0 - WINDOW + 1, 0) // tk
        _, l_w, a_w = lax.fori_loop(j_lo, qi * (tq // tk) + tq // tk, win_body, _flash_init(tq))
        o_win = a_w / jnp.maximum(l_w, 1e-30)

        mixed = []
        for r in range(R_NSA):
            h = g * R_NSA + r
            o_h = (gt[:, 3 * h:3 * h + 1] * o_cmp[r] + gt[:, 3 * h + 1:3 * h + 2] * o_slc[r]
                   + gt[:, 3 * h + 2:3 * h + 3] * o_win[r])
            if h % 2 != g:
                o_h = pltpu.roll(o_h, HD_NSA, axis=1)
            mixed.append(o_h)
        for c in range(R_NSA // 2):
            pair = jnp.where(lane < HD_NSA, mixed[2 * c], mixed[2 * c + 1])
            col = (g * (R_NSA // 2) + c) * LANES
            o_ref[:, col:col + LANES] = pair.astype(BF16)


def _nsa_call(nq, kv, kvb, gates, cw_bd, cb2, B, L):
    tq = NSA_TQ
    nQ = L // tq
    nb = L // CMP_BLK
    assert nb <= LANES and L % tq == 0
    rows = lambda w: pl.BlockSpec((tq, w), lambda b, q: (b * nQ + q, 0))
    seq = lambda c: pl.BlockSpec((L, LANES), lambda b, q: (b, c))
    return pl.pallas_call(
        functools.partial(_nsa_kernel, nb=nb),
        grid=(B, nQ),
        in_specs=[rows(D_NSA), seq(0), seq(1), seq(2), seq(3), seq(4), seq(5), rows(LANES),
                  _full((2, CMP_BLK, LANES, LANES)), _full((2, 1, LANES))],
        out_specs=rows(D_NSA),
        out_shape=jax.ShapeDtypeStruct((B * L, D_NSA), BF16),
        scratch_shapes=[pltpu.VMEM((LANES, LANES), BF16), pltpu.VMEM((LANES, LANES), BF16)],
        compiler_params=_cparams(("parallel", "arbitrary")),
        name="nsa_prompt",
    )(nq, kv, kv, kvb, kvb, kvb, kvb, gates, cw_bd, cb2)


def _normmm_kernel(x_ref, g_ref, w_ref, o_ref, ob_ref):
    y = _dot(_rms(x_ref[...], g_ref[...]).astype(BF16), w_ref[...])
    o_ref[...] = y
    ob_ref[...] = y.astype(BF16)


def _normmm_call(x2d, g, w, tm):
    n, d = x2d.shape
    m = w.shape[1]
    return pl.pallas_call(
        _normmm_kernel,
        grid=(n // tm,),
        in_specs=[pl.BlockSpec((tm, d), lambda i: (i, 0)), _full((1, d)), _full((d, m))],
        out_specs=[pl.BlockSpec((tm, m), lambda i: (i, 0))] * 2,
        out_shape=[jax.ShapeDtypeStruct((n, m), F32), jax.ShapeDtypeStruct((n, m), BF16)],
        compiler_params=_cparams(("parallel",)),
        name="norm_matmul",
    )(x2d, g.reshape(1, d), w)


def _mixout_kernel(yr_ref, on_ref, h_ref, wo_ref, g_ref, wq_ref, h1_ref, mq_ref):
    h1 = h_ref[...] + _dot(yr_ref[...], wo_ref[0:D_RET, :]) + _dot(on_ref[...], wo_ref[D_RET:, :])
    h1_ref[...] = h1
    mq_ref[...] = _dot(_rms(h1, g_ref[...]).astype(BF16), wq_ref[...]).astype(BF16)


def _mixout_call(yret, onsa, h, w_out, g_mem, w_mq, tm):
    n = h.shape[0]
    row = lambda w: pl.BlockSpec((tm, w), lambda i: (i, 0))
    return pl.pallas_call(
        _mixout_kernel,
        grid=(n // tm,),
        in_specs=[row(D_RET), row(D_NSA), row(D_MODEL), _full((D_RET + D_NSA, D_MODEL)), _full((1, D_MODEL)),
                  _full((D_MODEL, D_MEM))],
        out_specs=[row(D_MODEL), row(D_MEM)],
        out_shape=[jax.ShapeDtypeStruct((n, D_MODEL), F32), jax.ShapeDtypeStruct((n, D_MEM), BF16)],
        compiler_params=_cparams(("parallel",)),
        name="mixer_out",
    )(yret, onsa, h, w_out, g_mem.reshape(1, D_MODEL), w_mq)


def _memattn_kernel(q_ref, k_ref, v_ref, o_ref):
    q, k, v = q_ref[...], k_ref[...], v_ref[...]
    for h in range(H_MEM):
        sl = slice(h * HD_MEM, (h + 1) * HD_MEM)
        s = _dot_nt(q[:, sl], k[:, sl]) * (HD_MEM ** -0.5)
        p = jnp.exp(s - jnp.max(s, axis=-1, keepdims=True))
        p = p / jnp.sum(p, axis=-1, keepdims=True)
        o_ref[:, sl] = _dot(p.astype(BF16), v[:, sl]).astype(BF16)


def _memattn_call(mq, mkvb, B, L, n_mem, tm):
    nT = L // tm
    return pl.pallas_call(
        _memattn_kernel,
        grid=(B, nT),
        in_specs=[pl.BlockSpec((tm, D_MEM), lambda b, i: (b * nT + i, 0)),
                  pl.BlockSpec((n_mem, D_MEM), lambda b, i: (b, 0)),
                  pl.BlockSpec((n_mem, D_MEM), lambda b, i: (b, 1))],
        out_specs=pl.BlockSpec((tm, D_MEM), lambda b, i: (b * nT + i, 0)),
        out_shape=jax.ShapeDtypeStruct((B * L, D_MEM), BF16),
        compiler_params=_cparams(("parallel", "parallel")),
        name="mem_attention",
    )(mq, mkvb, mkvb)


def _memattn1_kernel(q_ref, k_ref, v_ref, o_ref):
    tb = q_ref.shape[0]
    for b in range(tb):
        q = q_ref[b]
        prod = k_ref[b] * q
        outs = []
        for h in range(H_MEM):
            sl = slice(h * HD_MEM, (h + 1) * HD_MEM)
            s = jnp.sum(prod[:, sl], axis=-1, keepdims=True) * (HD_MEM ** -0.5)
            p = jnp.exp(s - jnp.max(s, axis=0, keepdims=True))
            p = p / jnp.sum(p, axis=0, keepdims=True)
            outs.append(jnp.sum(p * v_ref[b][:, sl], axis=0, keepdims=True))
        o_ref[b] = jnp.concatenate(outs, axis=-1)


def _memattn1_call(mq, cache_k, cache_v):
    DB, n_mem = cache_k.shape[0], cache_k.shape[1]
    tb = 8
    blk = pl.BlockSpec((tb, n_mem, D_MEM), lambda i: (i, 0, 0))
    q3 = pl.BlockSpec((tb, 1, D_MEM), lambda i: (i, 0, 0))
    o = pl.pallas_call(
        _memattn1_kernel,
        grid=(DB // tb,),
        in_specs=[q3, blk, blk],
        out_specs=q3,
        out_shape=jax.ShapeDtypeStruct((DB, 1, D_MEM), F32),
        compiler_params=_cparams(("parallel",)),
        name="mem_attention_step",
    )(mq.astype(F32).reshape(DB, 1, D_MEM), cache_k.reshape(DB, n_mem, D_MEM), cache_v.reshape(DB, n_mem, D_MEM))
    return o.reshape(DB, D_MEM).astype(BF16)


def _prep_peer_keys(subkeys):
    half = PEER_DKEY // 2
    z = jnp.zeros_like(subkeys[:, 0])
    k0 = jnp.concatenate([subkeys[:, 0], z], axis=-1)
    k1 = jnp.concatenate([z, subkeys[:, 1]], axis=-1)
    return jnp.concatenate([k0, k1], axis=1).astype(BF16)


def _top_rows(x, k, payload=None):
    n = x.shape[-2]
    ri = _iota(x.shape, x.ndim - 2)
    vals, picks = [], []
    for _ in range(k):
        m = jnp.max(x, axis=-2, keepdims=True)
        i = jnp.min(jnp.where(x == m, ri, n), axis=-2, keepdims=True)
        hit = ri == i
        vals.append(m)
        picks.append(i if payload is None else jnp.max(jnp.where(hit, payload, -1), axis=-2, keepdims=True))
        x = jnp.where(hit, -jnp.inf, x)
    return jnp.concatenate(vals, axis=-2), jnp.concatenate(picks, axis=-2)


def _route_kernel(om_ref, h1_ref, wo_ref, g_ref, wq_ref, sk_ref, h2_ref, xn_ref, idx_ref, gw_ref,
                  idx_scr, gw_scr):
    tm = h1_ref.shape[0]
    h2 = h1_ref[...] + _dot(om_ref[...], wo_ref[...])
    h2_ref[...] = h2
    xn = _rms(h2, g_ref[...])
    xn_ref[...] = xn
    xb = xn.astype(BF16)

    def head(h, _):
        pq = _dot(xb, wq_ref[h]).astype(BF16)
        s = _dot_nt(sk_ref[h], pq).reshape(2, PEER_KEYS, tm)
        v12, i12 = _top_rows(s, PEER_TOPK)
        cand = (v12[0][:, None, :] + v12[1][None, :, :]).reshape(PEER_TOPK * PEER_TOPK, tm)
        cidx = (i12[0][:, None, :] * PEER_KEYS + i12[1][None, :, :]).reshape(PEER_TOPK * PEER_TOPK, tm)
        top, expert = _top_rows(cand, PEER_TOPK, payload=cidx)
        e = jnp.exp(top - top[0:1, :])
        rows = pl.ds(pl.multiple_of(h * PEER_TOPK, PEER_TOPK), PEER_TOPK)
        idx_scr[rows, :] = expert
        gw_scr[rows, :] = e / jnp.sum(e, axis=0, keepdims=True)
        return 0

    lax.fori_loop(0, PEER_HEADS, head, 0)
    idx_ref[...] = idx_scr[...].T
    gw_ref[...] = gw_scr[...].T


def _route_call(omem, h1, w_mo, g_ffn, wq_h, sk_pad, tm):
    n = h1.shape[0]
    nk = PEER_HEADS * PEER_TOPK
    row = lambda w: pl.BlockSpec((tm, w), lambda i: (i, 0))
    return pl.pallas_call(
        _route_kernel,
        grid=(n // tm,),
        in_specs=[row(D_MEM), row(D_MODEL), _full((D_MEM, D_MODEL)), _full((1, D_MODEL)),
                  _full((PEER_HEADS, D_MODEL, PEER_DKEY)), _full((PEER_HEADS, 2 * PEER_KEYS, PEER_DKEY))],
        out_specs=[row(D_MODEL), row(D_MODEL), row(nk), row(nk)],
        out_shape=[jax.ShapeDtypeStruct((n, D_MODEL), F32), jax.ShapeDtypeStruct((n, D_MODEL), F32),
                   jax.ShapeDtypeStruct((n, nk), jnp.int32), jax.ShapeDtypeStruct((n, nk), F32)],
        scratch_shapes=[pltpu.VMEM((nk, tm), jnp.int32), pltpu.VMEM((nk, tm), F32)],
        compiler_params=_cparams(("parallel",)),
        name="peer_route",
    )(omem, h1, w_mo, g_ffn.reshape(1, D_MODEL), wq_h, sk_pad)


def _gelu_tanh(x):
    return 0.5 * x * (1.0 + jnp.tanh(0.7978845608028654 * (x + 0.044715 * x * x * x)))


def _peer_kernel(idx_ref, x_ref, gw_ref, h2_ref, gf_ref, u_hbm, v_hbm, y_ref, ubuf, vbuf, sem):
    tt = x_ref.shape[0]
    nk = gw_ref.shape[1]

    def row_copy(tbl, buf, e, slot, j, s):
        return pltpu.make_async_copy(tbl.at[pl.ds(e, 1), :], buf.at[slot, pl.ds(j, 1), :], sem.at[slot, s])

    def issue(t, slot):
        def body(j, _):
            e = idx_ref[t, j]
            row_copy(u_hbm, ubuf, e, slot, j, 0).start()
            row_copy(v_hbm, vbuf, e, slot, j, 1).start()
            return 0
        lax.fori_loop(0, nk, body, 0, unroll=8)

    def wait(slot):
        pltpu.make_async_copy(u_hbm.at[pl.ds(0, nk), :], ubuf.at[slot], sem.at[slot, 0]).wait()
        pltpu.make_async_copy(v_hbm.at[pl.ds(0, nk), :], vbuf.at[slot], sem.at[slot, 1]).wait()

    issue(0, 0)

    def token(t, _):
        slot = t % 2

        @pl.when(t + 1 < tt)
        def _():
            issue(t + 1, 1 - slot)

        wait(slot)
        xr = x_ref[pl.ds(t, 1), :].astype(BF16)
        a = _dot_nt(xr, ubuf[slot].astype(BF16))
        c = gw_ref[pl.ds(t, 1), :] * _gelu_tanh(a)
        o = _dot(c.astype(BF16), vbuf[slot].astype(BF16))
        y_ref[pl.ds(t, 1), :] = _rms(h2_ref[pl.ds(t, 1), :] + o, gf_ref[...])
        return 0

    lax.fori_loop(0, tt, token, 0)


def _peer_call(idx, xn, gw, h2, g_final, u, v, tt):
    n = xn.shape[0]
    nk = idx.shape[1]
    row = lambda w: pl.BlockSpec((tt, w), lambda i: (i, 0))
    return pl.pallas_call(
        _peer_kernel,
        grid=(n // tt,),
        in_specs=[pl.BlockSpec((tt, nk), lambda i: (i, 0), memory_space=pltpu.SMEM),
                  row(D_MODEL), row(nk), row(D_MODEL), _full((1, D_MODEL)),
                  pl.BlockSpec(memory_space=pl.ANY), pl.BlockSpec(memory_space=pl.ANY)],
        out_specs=row(D_MODEL),
        out_shape=jax.ShapeDtypeStruct((n, D_MODEL), F32),
        scratch_shapes=[pltpu.VMEM((2, nk, D_MODEL), F32), pltpu.VMEM((2, nk, D_MODEL), F32),
                        pltpu.SemaphoreType.DMA((2, 2))],
        compiler_params=_cparams(("arbitrary",)),
        name="peer_experts",
    )(idx, xn, gw, h2, g_final.reshape(1, D_MODEL), u, v)


CMP_PAGES = 64


def _cmp_pages_kernel(x_ref, w_ref, b_ref, o_ref):
    n_rows = o_ref.shape[0]
    halves = []
    for c in range(PAGE_SIZE // CMP_BLK):
        def body(j, acc):
            x = x_ref[pl.ds(c * CMP_BLK + j, n_rows, stride=PAGE_SIZE), :].astype(BF16)
            return acc + _dot(x, w_ref[j])
        halves.append(lax.fori_loop(0, CMP_BLK, body, jnp.zeros((n_rows, HD_NSA), F32)))
    o_ref[...] = jnp.concatenate(halves, axis=-1) + b_ref[...]


def _cmp_pages_call(pool, w, b):
    n_phys = pool.shape[0]
    rows = n_phys * G_NSA
    step = CMP_PAGES * G_NSA
    assert rows % step == 0
    return pl.pallas_call(
        _cmp_pages_kernel,
        grid=(rows // step,),
        in_specs=[pl.BlockSpec((step * PAGE_SIZE, HD_NSA), lambda i: (i, 0)),
                  _full((CMP_BLK, HD_NSA, HD_NSA)), _full((1, LANES))],
        out_specs=pl.BlockSpec((step, LANES), lambda i: (i, 0)),
        out_shape=jax.ShapeDtypeStruct((rows, LANES), F32),
        compiler_params=_cparams(("parallel",)),
        name="compress_pages",
    )(pool.reshape(rows * PAGE_SIZE, HD_NSA), w.astype(BF16), jnp.tile(b, 2).reshape(1, LANES))


def _nsa1_cmp_kernel(pt_ref, nq_ref, ckn_ref, cvn_ref, tk_ref, tv_ref, cw_ref, cb_ref, ocmp_ref, idx_ref,
                     kg_scr, vg_scr, *, n_pages, q_pos):
    tb = nq_ref.shape[0]
    nb_past = n_pages * (PAGE_SIZE // CMP_BLK)
    base = pl.program_id(0) * tb
    lane = _iota((1, LANES), 1)
    blk_n = 2 * (lane % HD_NSA) + lane // HD_NSA
    forced = (blk_n == 0) | (blk_n == q_pos // CMP_BLK)
    vis = (blk_n * CMP_BLK + CMP_BLK - 1) <= q_pos
    new_vis = (nb_past * CMP_BLK + CMP_BLK - 1) <= q_pos
    new_forced = nb_past == q_pos // CMP_BLK
    n_row = jnp.broadcast_to(blk_n, (LANES, LANES))
    n_col = 2 * (_iota((LANES, LANES), 0) % HD_NSA) + _iota((LANES, LANES), 0) // HD_NSA
    lane8 = _iota((1, LANES), 1) // HD_NSA

    def sample(b, _):
        qrow = nq_ref[pl.ds(b, 1), :].astype(F32)
        new_k = _dot(ckn_ref[pl.ds(b, 1), :].astype(BF16), cw_ref[0]) + cb_ref[0]
        new_v = _dot(cvn_ref[pl.ds(b, 1), :].astype(BF16), cw_ref[1]) + cb_ref[1]
        o_row, idx_row = [], jnp.full((1, LANES), -1, jnp.int32)
        for g in range(G_NSA):
            def gather(i, _):
                r = pt_ref[base + b, i] * G_NSA + g
                kg_scr[pl.ds(i, 1), :] = tk_ref[pl.ds(r, 1), :]
                vg_scr[pl.ds(i, 1), :] = tv_ref[pl.ds(r, 1), :]
                return 0
            lax.fori_loop(0, n_pages, gather, 0)
            rows = []
            for c in range(2):
                for r in range(R_NSA):
                    h = g * R_NSA + r
                    x = qrow[:, (h // 2) * LANES:(h // 2 + 1) * LANES]
                    if h % 2 != c:
                        x = pltpu.roll(x, HD_NSA, axis=1)
                    rows.append(jnp.where(lane8 == c, x, 0.0))
            qpad = jnp.concatenate(rows, axis=0)
            s = _dot_nt(qpad.astype(BF16), kg_scr[...].astype(BF16)) * (HD_NSA ** -0.5)
            s3 = s.reshape(2, R_NSA, n_pages)
            qg = qpad[R_NSA * g:R_NSA * (g + 1), :]
            nk_g = jnp.where(lane8 == g, new_k.astype(BF16).astype(F32), 0.0)
            s_new = jnp.sum(qg.astype(BF16).astype(F32) * nk_g, axis=-1, keepdims=True) * (HD_NSA ** -0.5)
            s_new = jnp.where(new_vis, s_new, NEG)[None]
            m = jnp.maximum(jnp.max(jnp.max(s3, axis=2, keepdims=True), axis=0, keepdims=True), s_new)
            p = jnp.exp(s3 - m)
            p_new = jnp.where(s_new > 0.5 * NEG, jnp.exp(s_new - m), 0.0)
            l = jnp.sum(jnp.sum(p, axis=2, keepdims=True), axis=0, keepdims=True) + p_new
            inv = 1.0 / jnp.maximum(l, 1e-30)
            p = p * inv
            p_new = p_new * inv
            res = _dot(p.reshape(2 * R_NSA, n_pages).astype(BF16), vg_scr[...].astype(BF16))
            o4 = res[0:R_NSA] + pltpu.roll(res[R_NSA:], HD_NSA, axis=1)
            nv_g = new_v.astype(BF16).astype(F32)
            if g == 1:
                nv_g = pltpu.roll(nv_g, HD_NSA, axis=1)
            o4 = o4 + p_new[0].astype(BF16).astype(F32) * nv_g
            o_row += [o4[r:r + 1, 0:HD_NSA] for r in range(R_NSA)]
            imp2 = jnp.sum(p, axis=1)
            imp = jnp.concatenate([imp2[0:1], imp2[1:2]], axis=-1)
            score = jnp.where(forced, SEL_FORCE, jnp.where(vis, imp, -1.0))
            imp_new = jnp.sum(p_new)
            sc_new = SEL_FORCE if new_forced else jnp.where(new_vis, imp_new, -1.0)
            a = jnp.broadcast_to(score, (LANES, LANES))
            bt = a.T
            ahead = (bt > a) | ((bt == a) & (n_col < n_row))
            rank = jnp.sum(jnp.where(ahead, 1.0, 0.0), axis=0, keepdims=True) + jnp.where(sc_new > score, 1.0, 0.0)
            rank_new = jnp.sum(jnp.where(score >= sc_new, 1.0, 0.0))
            for r in range(N_SEL):
                hit = (rank == r) & (score >= 0.0)
                val = jnp.sum(jnp.where(hit, blk_n + 1, 0)) - 1
                val = jnp.where((rank_new == r) & (sc_new >= 0.0), nb_past, val)
                idx_row = jnp.where(lane == g * N_SEL + r, val, idx_row)
        ocmp_ref[pl.ds(b, 1), :] = jnp.concatenate(o_row, axis=-1)
        idx_ref[pl.ds(b, 1), :] = idx_row
        return 0

    lax.fori_loop(0, tb, sample, 0)


def _nsa1_cmp_call(page_table, nq, ck_new, cv_new, tbl_k, tbl_v, cw_bd, cb2, q_pos):
    DB, n_pages = page_table.shape
    assert n_pages * (PAGE_SIZE // CMP_BLK) == LANES
    tb = 8
    row = lambda w: pl.BlockSpec((tb, w), lambda i, pt: (i, 0))
    whole = lambda a: pl.BlockSpec(a.shape, lambda i, pt: (0,) * a.ndim, pipeline_mode=pl.Buffered(1))
    return pl.pallas_call(
        functools.partial(_nsa1_cmp_kernel, n_pages=n_pages, q_pos=q_pos),
        grid_spec=pltpu.PrefetchScalarGridSpec(
            num_scalar_prefetch=1,
            grid=(DB // tb,),
            in_specs=[row(D_NSA), row(LANES), row(LANES), whole(tbl_k), whole(tbl_v),
                      pl.BlockSpec((2, LANES, LANES), lambda i, pt: (0, 0, 0)),
                      pl.BlockSpec((2, 1, LANES), lambda i, pt: (0, 0, 0))],
            out_specs=[row(D_NSA), row(LANES)],
            scratch_shapes=[pltpu.VMEM((n_pages, LANES), F32), pltpu.VMEM((n_pages, LANES), F32)]),
        out_shape=[jax.ShapeDtypeStruct((DB, D_NSA), F32), jax.ShapeDtypeStruct((DB, LANES), jnp.int32)],
        compiler_params=_cparams(("arbitrary",)),
        name="nsa_step_compressed",
    )(page_table, nq.astype(F32), ck_new, cv_new, tbl_k, tbl_v, cw_bd[:, 0], cb2)


def _nsa1_attn_kernel(idx_ref, pt_ref, nq_ref, kvn_ref, gt_ref, ocmp_ref, wk_ref, wv_ref, sk_hbm, sv_hbm,
                      o_ref, kbuf, vbuf, sem, *, n_pages, q_pos, past_len):
    b = pl.program_id(0)
    nb_past = n_pages * (PAGE_SIZE // CMP_BLK)
    wb = wk_ref.shape[2]

    def block_copies(bb, slot, g, r):
        n = idx_ref[bb, g * N_SEL + r]
        past = (n >= 0) & (n < nb_past)
        ns = jnp.clip(n, 0, nb_past - 1)
        page = pt_ref[bb, ns // 2]
        rows = pl.ds(pl.multiple_of((ns % 2) * CMP_BLK, CMP_BLK), CMP_BLK)
        ck = pltpu.make_async_copy(sk_hbm.at[page, g, rows, :], kbuf.at[slot, g, r], sem.at[slot, 0])
        cv = pltpu.make_async_copy(sv_hbm.at[page, g, rows, :], vbuf.at[slot, g, r], sem.at[slot, 1])
        return past, ck, cv

    def issue(bb, slot):
        for g in range(G_NSA):
            for r in range(N_SEL):
                past, ck, cv = block_copies(bb, slot, g, r)

                @pl.when(past)
                def _():
                    ck.start()
                    cv.start()

                @pl.when(jnp.logical_not(past))
                def _():
                    kbuf[slot, g, r] = jnp.zeros((CMP_BLK, HD_NSA), F32)
                    vbuf[slot, g, r] = jnp.zeros((CMP_BLK, HD_NSA), F32)

    @pl.when(b == 0)
    def _():
        issue(0, 0)

    @pl.when(b + 1 < pl.num_programs(0))
    def _():
        issue(b + 1, (b + 1) % 2)

    slot = b % 2
    for g in range(G_NSA):
        for r in range(N_SEL):
            past, ck, cv = block_copies(b, slot, g, r)

            @pl.when(past)
            def _():
                ck.wait()
                cv.wait()

    qrow = nq_ref[0].astype(F32)
    kvn = kvn_ref[0]
    gt = gt_ref[0]
    ocmp = ocmp_ref[0]
    scale = HD_NSA ** -0.5
    key_slot = _iota((1, N_SEL * CMP_BLK), 1) // CMP_BLK
    wpos = past_len - wb + _iota((1, wb), 1)
    wdist = q_pos - wpos
    w_ok = (wdist >= 0) & (wdist < WINDOW) & (wpos >= 0)
    bfr = lambda t: t.astype(BF16).astype(F32)
    pieces = []
    for g in range(G_NSA):
        q4 = jnp.concatenate([qrow[:, (g * R_NSA + r) * HD_NSA:(g * R_NSA + r + 1) * HD_NSA]
                              for r in range(R_NSA)], axis=0)
        q4b = q4.astype(BF16)
        new = lambda i: kvn[:, i * D_KV + g * HD_NSA:i * D_KV + (g + 1) * HD_NSA]
        ok = jnp.zeros((1, N_SEL * CMP_BLK), jnp.int32)
        has_new = jnp.int32(0)
        for r in range(N_SEL):
            n = idx_ref[b, g * N_SEL + r]
            ok = jnp.where(key_slot == r, ((n >= 0) & (n < nb_past)).astype(jnp.int32), ok)
            has_new = has_new | (n == nb_past).astype(jnp.int32)
        kg = kbuf[slot, g].reshape(N_SEL * CMP_BLK, HD_NSA).astype(BF16)
        vg = vbuf[slot, g].reshape(N_SEL * CMP_BLK, HD_NSA).astype(BF16)
        s = jnp.where(ok > 0, _dot_nt(q4b, kg) * scale, NEG)
        s_new = jnp.sum(bfr(q4) * bfr(new(2)), axis=-1, keepdims=True) * scale
        s_new = jnp.where(has_new > 0, s_new, NEG)
        m = jnp.maximum(jnp.max(s, axis=-1, keepdims=True), s_new)
        p = jnp.where(s > 0.5 * NEG, jnp.exp(s - m), 0.0)
        p_new = jnp.where(s_new > 0.5 * NEG, jnp.exp(s_new - m), 0.0)
        l = jnp.sum(p, axis=-1, keepdims=True) + p_new
        o_slc = (_dot(p.astype(BF16), vg) + bfr(p_new) * bfr(new(3))) / jnp.maximum(l, 1e-30)
        wk = wk_ref[0, g].astype(BF16)
        wv = wv_ref[0, g].astype(BF16)
        s = jnp.where(w_ok, _dot_nt(q4b, wk) * scale, NEG)
        s_new = jnp.sum(bfr(q4) * bfr(new(4)), axis=-1, keepdims=True) * scale
        m = jnp.maximum(jnp.max(s, axis=-1, keepdims=True), s_new)
        p = jnp.where(s > 0.5 * NEG, jnp.exp(s - m), 0.0)
        p_new = jnp.exp(s_new - m)
        l = jnp.sum(p, axis=-1, keepdims=True) + p_new
        o_win = (_dot(p.astype(BF16), wv) + bfr(p_new) * bfr(new(5))) / jnp.maximum(l, 1e-30)
        for r in range(R_NSA):
            h = g * R_NSA + r
            pieces.append(gt[:, 3 * h:3 * h + 1] * ocmp[:, h * HD_NSA:(h + 1) * HD_NSA]
                          + gt[:, 3 * h + 1:3 * h + 2] * o_slc[r:r + 1]
                          + gt[:, 3 * h + 2:3 * h + 3] * o_win[r:r + 1])
    o_ref[0] = jnp.concatenate(pieces, axis=-1)


def _nsa1_attn_call(idx, page_table, nq, kv_new, gates, ocmp, win_k, win_v, slc_k, slc_v, q_pos, past_len):
    DB, n_pages = page_table.shape
    wb = win_k.shape[2]
    r3 = lambda a: a.reshape(DB, 1, a.shape[-1])
    row = lambda w: pl.BlockSpec((1, 1, w), lambda b, *_: (b, 0, 0))
    win = pl.BlockSpec((1, G_NSA, wb, HD_NSA), lambda b, *_: (b, 0, 0, 0))
    o = pl.pallas_call(
        functools.partial(_nsa1_attn_kernel, n_pages=n_pages, q_pos=q_pos, past_len=past_len),
        grid_spec=pltpu.PrefetchScalarGridSpec(
            num_scalar_prefetch=2,
            grid=(DB,),
            in_specs=[row(D_NSA), row(6 * D_KV), row(LANES), row(D_NSA), win, win,
                      pl.BlockSpec(memory_space=pl.ANY), pl.BlockSpec(memory_space=pl.ANY)],
            out_specs=row(D_NSA),
            scratch_shapes=[pltpu.VMEM((2, G_NSA, N_SEL, CMP_BLK, HD_NSA), F32),
                            pltpu.VMEM((2, G_NSA, N_SEL, CMP_BLK, HD_NSA), F32),
                            pltpu.SemaphoreType.DMA((2, 2))]),
        out_shape=jax.ShapeDtypeStruct((DB, 1, D_NSA), F32),
        compiler_params=_cparams(("arbitrary",)),
        name="nsa_step_attend",
    )(idx, page_table, r3(nq), r3(kv_new), r3(gates), r3(ocmp), win_k, win_v, slc_k, slc_v)
    return o.reshape(DB, D_NSA).astype(BF16)


def _token_tail(yret, onsa, h, mem_attend, lw, tm):
    w_out, g_mem, w_mq, w_mo, g_ffn, wq_h, sk_pad, u, v, g_final = lw
    h1, mq = _mixout_call(yret, onsa, h, w_out, g_mem, w_mq, tm)
    omem = mem_attend(mq)
    h2, xn, idx, gw = _route_call(omem, h1, w_mo, g_ffn, wq_h, sk_pad, tm)
    return _peer_call(idx, xn, gw, h2, g_final, u, v, min(tm, 32))


def _all_tables(pos):
    return (_rot_tables(pos, H_RET, HD_RET, HD_RET, RET_THETA)
            + _rot_tables(pos, H_NSA, HD_NSA, ROPE_DIMS, ROPE_THETA)
            + _rot_tables(pos, G_NSA, HD_NSA, ROPE_DIMS, ROPE_THETA))


def kernel(x_prompt, x_sample, mem_prompt, state_ret, cache_cmp_k, cache_cmp_v, cache_slc_k, cache_slc_v,
           cache_win_k, cache_win_v, cache_mem_k, cache_mem_v, page_table, norm_mix_g, w_in, ret_gn_g,
           cmp_w, cmp_b, w_out, norm_mem_g, mem_norm_g, w_mq, w_mk, w_mv, w_mo, norm_ffn_g,
           peer_wq, peer_subkeys, peer_u, peer_v, norm_final_g):
    B, L, D = x_prompt.shape
    DB, LS, _ = x_sample.shape
    n_mem = mem_prompt.shape[1]
    n_pages = page_table.shape[1]
    past_len = n_pages * PAGE_SIZE
    assert w_in.shape[0] == 1 and LS == 1 and D == D_MODEL
    l = 0
    tm = 256

    w_all = _prep_w_in(w_in[l])
    cw_bd = _prep_cmp_w(cmp_w[l])
    cb2 = jnp.tile(cmp_b[l], (1, G_NSA)).reshape(2, 1, LANES)
    w_kv = jnp.concatenate([w_mk[l], w_mv[l]], axis=1).astype(BF16)
    wq_h = peer_wq[l].reshape(D, PEER_HEADS, PEER_DKEY).transpose(1, 0, 2).astype(BF16)
    lw = (w_out[l].astype(BF16), norm_mem_g[l], w_mq[l].astype(BF16), w_mo[l].astype(BF16), norm_ffn_g[l],
          wq_h, _prep_peer_keys(peer_subkeys[l]), peer_u[l], peer_v[l], norm_final_g)

    xp = x_prompt.reshape(B * L, D)
    rq, rk, rv, rg, nq, kv, kvb, gt = _proj_call(xp, norm_mix_g[l], w_all,
                                                 _all_tables(jnp.arange(L, dtype=jnp.int32)), tm)
    yret, p_state = _ret_call(rq, rk, rv, rg, ret_gn_g[l], B, L)
    onsa = _nsa_call(nq, kv, kvb, gt, cw_bd, cb2, B, L)
    mkv, mkvb = _normmm_call(mem_prompt.reshape(B * n_mem, D), mem_norm_g[l], w_kv, tm)
    y_p = _token_tail(yret, onsa, xp, lambda mq: _memattn_call(mq, mkvb, B, L, n_mem, tm), lw, tm)

    kv6 = kv.reshape(B, L, 6, G_NSA, HD_NSA)
    pages = lambda i: kv6[:, :, i].reshape(B, L // PAGE_SIZE, PAGE_SIZE, G_NSA, HD_NSA).transpose(0, 1, 3, 2, 4)
    keep = min(WINDOW, L)
    tail = lambda i: kv6[:, L - keep:, i].transpose(0, 2, 1, 3)
    mem4 = lambda t: t.reshape(B, n_mem, H_MEM, HD_MEM)

    xs = x_sample.reshape(DB, D)
    rq, rk, rv, rg, nq, kvs, _, gt = _proj_call(xs, norm_mix_g[l], w_all,
                                                _all_tables(jnp.full((DB,), past_len, jnp.int32)), DB)
    yret, s_state = _ret1_call(rq, rk, rv, rg, ret_gn_g[l], state_ret[l])
    tbl_k = _cmp_pages_call(cache_cmp_k[l], cmp_w[l, 0], cmp_b[l, 0])
    tbl_v = _cmp_pages_call(cache_cmp_v[l], cmp_w[l, 1], cmp_b[l, 1])
    ocmp, sel_idx = _nsa1_cmp_call(page_table, nq, kvs[:, 0:D_KV], kvs[:, D_KV:2 * D_KV], tbl_k, tbl_v, cw_bd, cb2,
                                   past_len)
    onsa = _nsa1_attn_call(sel_idx, page_table, nq, kvs, gt, ocmp, cache_win_k[l], cache_win_v[l],
                           cache_slc_k[l], cache_slc_v[l], past_len, past_len)
    y_s = _token_tail(yret, onsa, xs, lambda mq: _memattn1_call(mq, cache_mem_k[l], cache_mem_v[l]), lw, DB)

    new = lambda i: kvs[:, i * D_KV:(i + 1) * D_KV].reshape(DB, G_NSA, 1, HD_NSA)
    wb = cache_win_k.shape[3]
    keep_s = min(WINDOW, wb + 1)
    win = lambda cache, i: jnp.concatenate([cache[l], new(i)], axis=2)[:, :, wb + 1 - keep_s:]
    st = lambda t: t[None]
    return (y_p.reshape(B, L, D), y_s.reshape(DB, 1, D), st(p_state),
            st(pages(0)), st(pages(1)), st(pages(2)), st(pages(3)), st(tail(4)), st(tail(5)),
            st(mem4(mkv[:, :D_MEM])), st(mem4(mkv[:, D_MEM:])), st(s_state),
            st(new(0)), st(new(1)), st(new(2)), st(new(3)), st(win(cache_win_k, 4)), st(win(cache_win_v, 5)))
```

```python
import functools

import numpy as np
import jax
import jax.numpy as jnp
from jax import lax
from jax.experimental import pallas as pl
from jax.experimental.pallas import tpu as pltpu

F32 = jnp.float32
BF16 = jnp.bfloat16

D_MODEL = 1024
PAGE_SIZE = 128
H_RET = 8
HD_RET = 64
RET_CHUNK = 128
RET_THETA = 10000.0
H_NSA = 8
G_NSA = 2
R_NSA = H_NSA // G_NSA
HD_NSA = 64
CMP_BLK = 64
N_SEL = 16
WINDOW = 512
SEL_FORCE = 1.0e4
ROPE_THETA = 500000.0
ROPE_DIMS = HD_NSA // 4
H_MEM = 4
HD_MEM = 128
PEER_KEYS = 128
PEER_HEADS = 8
PEER_DKEY = 128
PEER_TOPK = 16
PEER_ROW = 4
EPS = 1e-6

D_RET = H_RET * HD_RET
D_NSA = H_NSA * HD_NSA
D_KV = G_NSA * HD_NSA
D_MEM = H_MEM * HD_MEM
NEG = -1.0e30

LANES = 128
VMEM_LIMIT = 56 * 1024 * 1024

C_RQ, C_RQR, C_RK, C_RKR, C_RV, C_RG, C_NQ, C_NQR = (i * 512 for i in range(8))
C_KV = 4096
C_KVR = C_KV + 6 * D_KV
C_NG = C_KVR + 3 * D_KV
N_PROJ = C_NG + LANES


def _cparams(sem):
    return pltpu.CompilerParams(dimension_semantics=sem, vmem_limit_bytes=VMEM_LIMIT)


def _rms(x, g):
    return x * lax.rsqrt(jnp.mean(x * x, axis=-1, keepdims=True) + EPS) * g


def _dot(a, b):
    return jnp.dot(a, b, preferred_element_type=F32)


def _dot_nt(a, b):
    return lax.dot_general(a, b, (((1,), (1,)), ((), ())), preferred_element_type=F32)


def _full(shape):
    n = len(shape)
    return pl.BlockSpec(shape, lambda *_: (0,) * n)


def _partner_cols(n_heads, hd, n_rot):
    half = n_rot // 2
    j = np.arange(hd)
    p = np.where(j < half, j + half, np.where(j < n_rot, j - half, j))
    return (np.arange(n_heads)[:, None] * hd + p[None, :]).reshape(-1)


def _prep_w_in(w_in):
    o = 0
    seg = {}
    for name, size in (("rq", D_RET), ("rk", D_RET), ("rv", D_RET), ("rg", D_RET), ("nq", D_NSA),
                       ("ck", D_KV), ("cv", D_KV), ("sk", D_KV), ("sv", D_KV), ("wk", D_KV), ("wv", D_KV),
                       ("ng", 3 * H_NSA)):
        seg[name] = (o, size)
        o += size
    cols = lambda n: np.arange(seg[n][0], seg[n][0] + seg[n][1])
    pr = _partner_cols(H_RET, HD_RET, HD_RET)
    pn = _partner_cols(H_NSA, HD_NSA, ROPE_DIMS)
    pk = _partner_cols(G_NSA, HD_NSA, ROPE_DIMS)
    order = np.concatenate([
        cols("rq"), cols("rq")[pr], cols("rk"), cols("rk")[pr], cols("rv"), cols("rg"),
        cols("nq"), cols("nq")[pn],
        cols("ck"), cols("cv"), cols("sk"), cols("sv"), cols("wk"), cols("wv"),
        cols("ck")[pk], cols("sk")[pk], cols("wk")[pk], cols("ng")])
    w = jnp.take(w_in, jnp.asarray(order, jnp.int32), axis=1)
    w = jnp.pad(w, ((0, 0), (0, N_PROJ - w.shape[1])))
    return w.astype(BF16)


def _rot_tables(pos, n_heads, hd, n_rot, theta):
    half = n_rot // 2
    inv = theta ** (-jnp.arange(half, dtype=F32) / half)
    ang = pos.astype(F32)[:, None] * inv[None, :]
    cos, sin = jnp.cos(ang), jnp.sin(ang)
    P = pos.shape[0]
    c = jnp.concatenate([cos, cos, jnp.ones((P, hd - n_rot), F32)], axis=1)
    s = jnp.concatenate([-sin, sin, jnp.zeros((P, hd - n_rot), F32)], axis=1)
    return jnp.tile(c, (1, n_heads)), jnp.tile(s, (1, n_heads))


def _proj_kernel(x_ref, g_ref, w_ref, cr_ref, sr_ref, cn_ref, sn_ref, ck_ref, sk_ref,
                 rq_ref, rk_ref, rv_ref, rg_ref, nq_ref, kv_ref, kvb_ref, gt_ref):
    hn = _rms(x_ref[...], g_ref[...]).astype(BF16)
    seg = lambda c0, n: _dot(hn, w_ref[:, c0:c0 + n])
    cr, sr = cr_ref[...], sr_ref[...]
    rq_ref[...] = (seg(C_RQ, 512) * cr + seg(C_RQR, 512) * sr).astype(BF16)
    rk_ref[...] = ((seg(C_RK, 512) * cr + seg(C_RKR, 512) * sr) * (HD_RET ** -0.5)).astype(BF16)
    rv_ref[...] = seg(C_RV, 512).astype(BF16)
    rg_ref[...] = seg(C_RG, 512)
    nq_ref[...] = (seg(C_NQ, 512) * cn_ref[...] + seg(C_NQR, 512) * sn_ref[...]).astype(BF16)
    ck, sk = ck_ref[...], sk_ref[...]
    for i in range(6):
        a = seg(C_KV + i * D_KV, D_KV)
        if i % 2 == 0:
            a = a * ck + seg(C_KVR + (i // 2) * D_KV, D_KV) * sk
        kv_ref[:, i * D_KV:(i + 1) * D_KV] = a
        kvb_ref[:, i * D_KV:(i + 1) * D_KV] = a.astype(BF16)
    z = seg(C_NG, LANES)
    gt_ref[...] = 1.0 / (1.0 + jnp.exp(-z))


def _proj_call(x2d, g, w_all, tabs, tm):
    n = x2d.shape[0]
    period = tabs[0].shape[0] // tm
    row = lambda w: pl.BlockSpec((tm, w), lambda i: (i, 0))
    tab = lambda w: pl.BlockSpec((tm, w), lambda i: (i % period, 0))
    outs = [(512, BF16), (512, BF16), (512, BF16), (512, F32), (512, BF16), (6 * D_KV, F32), (6 * D_KV, BF16),
            (LANES, F32)]
    return pl.pallas_call(
        _proj_kernel,
        grid=(n // tm,),
        in_specs=[row(D_MODEL), _full((1, D_MODEL)), _full((D_MODEL, N_PROJ)),
                  tab(512), tab(512), tab(512), tab(512), tab(D_KV), tab(D_KV)],
        out_specs=[row(w) for w, _ in outs],
        out_shape=[jax.ShapeDtypeStruct((n, w), dt) for w, dt in outs],
        compiler_params=_cparams(("parallel",)),
        name="proj",
    )(x2d, g.reshape(1, D_MODEL), w_all, *tabs)


def _ret_consts(C):
    lg = jnp.log(1.0 - 2.0 ** (-5.0 - jnp.arange(H_RET, dtype=F32)))
    idx = jnp.arange(C, dtype=F32)
    diff = idx[:, None] - idx[None, :]
    dmat = jnp.where(diff >= 0, jnp.exp(lg[:, None, None] * jnp.maximum(diff, 0.0)), 0.0)
    xi = jnp.exp(lg[None, :] * (idx[:, None] + 1.0))
    zeta = jnp.exp(lg[:, None] * (C - 1.0 - idx[None, :]))
    g_c = jnp.exp(lg * C)
    return dmat, xi, zeta, g_c


def _ret_kernel(gc_ref, q_ref, k_ref, v_ref, rg_ref, gn_ref, dmat_ref, xi_ref, zeta_ref,
                y_ref, st_ref, s_scr):
    c = pl.program_id(1)

    @pl.when(c == 0)
    def _():
        s_scr[...] = jnp.zeros_like(s_scr)

    q, k, v = q_ref[...], k_ref[...], v_ref[...]
    k_t = k.astype(F32).T
    outs = []
    for h in range(H_RET):
        sl = slice(h * HD_RET, (h + 1) * HD_RET)
        qh, kh, vh = q[:, sl], k[:, sl], v[:, sl]
        att = _dot_nt(qh, kh) * dmat_ref[h]
        inner = _dot(att.astype(BF16), vh)
        s_old = s_scr[h]
        cross = _dot(qh, s_old.astype(BF16)) * xi_ref[:, h:h + 1]
        o = inner + cross
        kz = (k_t[sl, :] * zeta_ref[h:h + 1, :]).astype(BF16)
        s_scr[h] = s_old * gc_ref[h] + _dot(kz, vh)
        mu = jnp.mean(o, axis=-1, keepdims=True)
        d = o - mu
        var = jnp.mean(d * d, axis=-1, keepdims=True)
        outs.append(d * lax.rsqrt(var + EPS))
    on = jnp.concatenate(outs, axis=-1)
    rg = rg_ref[...]
    silu = rg * (1.0 / (1.0 + jnp.exp(-rg)))
    y_ref[...] = (silu * (on * gn_ref[...])).astype(BF16)

    @pl.when(c == pl.num_programs(1) - 1)
    def _():
        st_ref[0] = s_scr[...]


def _ret_call(rq, rk, rv, rg, gn, B, L):
    C = RET_CHUNK
    nC = L // C
    dmat, xi, zeta, g_c = _ret_consts(C)
    blk = lambda: pl.BlockSpec((C, D_RET), lambda b, c: (b * nC + c, 0))
    return pl.pallas_call(
        _ret_kernel,
        grid=(B, nC),
        in_specs=[pl.BlockSpec(memory_space=pltpu.SMEM), blk(), blk(), blk(), blk(), _full((1, D_RET)),
                  _full((H_RET, C, C)), _full((C, H_RET)), _full((H_RET, C))],
        out_specs=[blk(), pl.BlockSpec((1, H_RET, HD_RET, HD_RET), lambda b, c: (b, 0, 0, 0))],
        out_shape=[jax.ShapeDtypeStruct((B * L, D_RET), BF16),
                   jax.ShapeDtypeStruct((B, H_RET, HD_RET, HD_RET), F32)],
        scratch_shapes=[pltpu.VMEM((H_RET, HD_RET, HD_RET), F32)],
        compiler_params=_cparams(("parallel", "arbitrary")),
        name="retention",
    )(g_c, rq, rk, rv, rg, gn.reshape(1, D_RET), dmat, xi, zeta)


def _ret1_kernel(q_ref, k_ref, v_ref, s_ref, gam_ref, rg_ref, gn_ref, y_ref, so_ref):
    q, k, v, s, gam = q_ref[...], k_ref[...], v_ref[...], s_ref[...], gam_ref[...]
    qk = jnp.sum(q * k, axis=1, keepdims=True)
    cross = jnp.sum(q * s, axis=1, keepdims=True) * gam
    o = qk * v + cross
    so_ref[...] = s * gam + k * v
    mu = jnp.mean(o, axis=-1, keepdims=True)
    d = o - mu
    var = jnp.mean(d * d, axis=-1, keepdims=True)
    rg = rg_ref[...]
    silu = rg * (1.0 / (1.0 + jnp.exp(-rg)))
    y_ref[...] = silu * (d * lax.rsqrt(var + EPS) * gn_ref[...])


def _ret1_call(rq, rk, rv, rg, gn, state):
    DB = rq.shape[0]
    n = DB * H_RET
    _, _, _, g_c = _ret_consts(1)
    col = lambda t: t.astype(F32).reshape(n, HD_RET, 1)
    rowv = lambda t: t.astype(F32).reshape(n, 1, HD_RET)
    gam = jnp.tile(g_c, DB).reshape(n, 1, 1)
    gn3 = jnp.tile(gn.reshape(H_RET, 1, HD_RET), (DB, 1, 1))
    tb = 128
    b3 = lambda a, b: pl.BlockSpec((tb, a, b), lambda i: (i, 0, 0))
    y, s_new = pl.pallas_call(
        _ret1_kernel,
        grid=(n // tb,),
        in_specs=[b3(HD_RET, 1), b3(HD_RET, 1), b3(1, HD_RET), b3(HD_RET, HD_RET), b3(1, 1), b3(1, HD_RET),
                  b3(1, HD_RET)],
        out_specs=[b3(1, HD_RET), b3(HD_RET, HD_RET)],
        out_shape=[jax.ShapeDtypeStruct((n, 1, HD_RET), F32),
                   jax.ShapeDtypeStruct((n, HD_RET, HD_RET), F32)],
        compiler_params=_cparams(("parallel",)),
        name="retention_step",
    )(col(rq), col(rk), rowv(rv), state.astype(F32).reshape(n, HD_RET, HD_RET), gam, rowv(rg), gn3)
    return y.reshape(DB, D_RET).astype(BF16), s_new.reshape(DB, H_RET, HD_RET, HD_RET)


NSA_TQ = 128
NSA_TK = 128


def _prep_cmp_w(cmp_w):
    z = jnp.zeros_like(cmp_w)
    top = jnp.concatenate([cmp_w, z], axis=-1)
    bot = jnp.concatenate([z, cmp_w], axis=-1)
    return jnp.concatenate([top, bot], axis=-2).astype(BF16)


def _iota(shape, dim):
    return lax.broadcasted_iota(jnp.int32, shape, dim)


def _group_queries(nqf, g, tq):
    lane_g = _iota((tq, LANES), 1) // HD_NSA
    parts = []
    for r in range(R_NSA):
        h = g * R_NSA + r
        x = nqf[:, (h // 2) * LANES:(h // 2 + 1) * LANES]
        if h % 2 != g:
            x = pltpu.roll(x, HD_NSA, axis=1)
        parts.append(jnp.where(lane_g == g, x, 0.0))
    return jnp.concatenate(parts, axis=0).astype(BF16)


def _select_blocks(score, nb, n_sel):
    s_t = score.T[:nb, :]
    n_i = _iota(s_t.shape, 0)
    rank = jnp.zeros(s_t.shape, F32)
    for m in range(nb):
        row = s_t[m:m + 1, :]
        ahead = (row > s_t) | ((row == s_t) & (n_i > m))
        rank = rank + jnp.where(ahead, 1.0, 0.0)
    sel_t = jnp.where((rank < n_sel) & (s_t >= 0.0), 1.0, 0.0)
    sel_t = jnp.concatenate([sel_t, jnp.zeros((LANES - nb, s_t.shape[1]), F32)], axis=0)
    return sel_t.T


def _flash_step(q, kt, vt, bias, carry):
    m, l, acc = carry
    tq, tk = bias.shape
    s = (_dot_nt(q, kt) * (HD_NSA ** -0.5)).reshape(R_NSA, tq, tk) + bias[None]
    m_new = jnp.maximum(m, jnp.max(s, axis=-1, keepdims=True))
    alpha = jnp.exp(m - m_new)
    p = jnp.where(s > 0.5 * NEG, jnp.exp(s - m_new), 0.0)
    l = alpha * l + jnp.sum(p, axis=-1, keepdims=True)
    pv = _dot(p.reshape(R_NSA * tq, tk).astype(BF16), vt).reshape(R_NSA, tq, LANES)
    return m_new, l, alpha * acc + pv


def _flash_init(tq):
    return (jnp.full((R_NSA, tq, 1), NEG, F32), jnp.zeros((R_NSA, tq, 1), F32),
            jnp.zeros((R_NSA, tq, LANES), F32))


def _nsa_kernel(nq_ref, ckf_ref, cvf_ref, sk_ref, sv_ref, wk_ref, wv_ref, gt_ref, cw_ref, cb_ref,
                o_ref, ck_scr, cv_scr, *, nb):
    qi = pl.program_id(1)
    tq, tk = NSA_TQ, NSA_TK
    n_sel = min(N_SEL, nb)

    @pl.when(qi == 0)
    def _compress():
        for which, (src, dst) in enumerate(((ckf_ref, ck_scr), (cvf_ref, cv_scr))):
            def body(j, acc):
                x = src[pl.ds(j, nb, stride=CMP_BLK), :].astype(BF16)
                return acc + _dot(x, cw_ref[which, j])
            acc = lax.fori_loop(0, CMP_BLK, body, jnp.zeros((nb, LANES), F32))
            dst[...] = jnp.zeros_like(dst)
            dst[0:nb, :] = (acc + cb_ref[which]).astype(BF16)

    t0 = qi * tq
    nqf = nq_ref[...].astype(F32)
    gt = gt_ref[...]
    pos = t0 + _iota((tq, 1), 0)
    blk = _iota((1, LANES), 1)
    vis = ((blk * CMP_BLK + CMP_BLK - 1) <= pos) & (blk < nb)
    forced = ((blk == 0) | (blk == pos // CMP_BLK)) & (blk < nb)
    vis_bias = jnp.where(vis, 0.0, NEG)
    lane = _iota((tq, LANES), 1)
    kcol = _iota((1, tk), 1)
    blk_row = _iota((LANES, tk), 0)
    blk_of_key = _iota((LANES, tk), 1) // CMP_BLK

    for g in range(G_NSA):
        qp = _group_queries(nqf, g, tq)

        sc = (_dot_nt(qp, ck_scr[...]) * (HD_NSA ** -0.5)).reshape(R_NSA, tq, LANES) + vis_bias[None]
        mc = jnp.max(sc, axis=-1, keepdims=True)
        pc = jnp.where(sc > 0.5 * NEG, jnp.exp(sc - mc), 0.0)
        pc = pc / jnp.maximum(jnp.sum(pc, axis=-1, keepdims=True), 1e-30)
        o_cmp = _dot(pc.reshape(R_NSA * tq, LANES).astype(BF16), cv_scr[...]).reshape(R_NSA, tq, LANES)
        imp = pc[0] + pc[1] + pc[2] + pc[3]

        score = jnp.where(forced, SEL_FORCE, jnp.where(vis, imp, -1.0))
        score = jnp.where(blk < nb, score, -2.0)
        sel = _select_blocks(score, nb, n_sel).astype(BF16)

        def sel_body(j, carry):
            expand = jnp.where(blk_row == blk_of_key + j * (tk // CMP_BLK), 1.0, 0.0).astype(BF16)
            chosen = _dot(sel, expand)
            kpos = j * tk + kcol
            bias = jnp.where((chosen > 0.5) & (kpos <= pos), 0.0, NEG)
            return _flash_step(qp, sk_ref[pl.ds(pl.multiple_of(j * tk, tk), tk), :],
                               sv_ref[pl.ds(pl.multiple_of(j * tk, tk), tk), :], bias, carry)

        _, l_s, a_s = lax.fori_loop(0, qi * (tq // tk) + tq // tk, sel_body, _flash_init(tq))
        o_slc = a_s / jnp.maximum(l_s, 1e-30)

        def win_body(j, carry):
            dist = pos - (j * tk + kcol)
            bias = jnp.where((dist >= 0) & (dist < WINDOW), 0.0, NEG)
            return _flash_step(qp, wk_ref[pl.ds(pl.multiple_of(j * tk, tk), tk), :],
                               wv_ref[pl.ds(pl.multiple_of(j * tk, tk), tk), :], bias, carry)

        j_lo = jnp.maximum(t0 - WINDOW + 1, 0) // tk
        _, l_w, a_w = lax.fori_loop(j_lo, qi * (tq // tk) + tq // tk, win_body, _flash_init(tq))
        o_win = a_w / jnp.maximum(l_w, 1e-30)

        mixed = []
        for r in range(R_NSA):
            h = g * R_NSA + r
            o_h = (gt[:, 3 * h:3 * h + 1] * o_cmp[r] + gt[:, 3 * h + 1:3 * h + 2] * o_slc[r]
                   + gt[:, 3 * h + 2:3 * h + 3] * o_win[r])
            if h % 2 != g:
                o_h = pltpu.roll(o_h, HD_NSA, axis=1)
            mixed.append(o_h)
        for c in range(R_NSA // 2):
            pair = jnp.where(lane < HD_NSA, mixed[2 * c], mixed[2 * c + 1])
            col = (g * (R_NSA // 2) + c) * LANES
            o_ref[:, col:col + LANES] = pair.astype(BF16)


def _nsa_call(nq, kv, kvb, gates, cw_bd, cb2, B, L):
    tq = NSA_TQ
    nQ = L // tq
    nb = L // CMP_BLK
    assert nb <= LANES and L % tq == 0
    rows = lambda w: pl.BlockSpec((tq, w), lambda b, q: (b * nQ + q, 0))
    seq = lambda c: pl.BlockSpec((L, LANES), lambda b, q: (b, c))
    return pl.pallas_call(
        functools.partial(_nsa_kernel, nb=nb),
        grid=(B, nQ),
        in_specs=[rows(D_NSA), seq(0), seq(1), seq(2), seq(3), seq(4), seq(5), rows(LANES),
                  _full((2, CMP_BLK, LANES, LANES)), _full((2, 1, LANES))],
        out_specs=rows(D_NSA),
        out_shape=jax.ShapeDtypeStruct((B * L, D_NSA), BF16),
        scratch_shapes=[pltpu.VMEM((LANES, LANES), BF16), pltpu.VMEM((LANES, LANES), BF16)],
        compiler_params=_cparams(("parallel", "arbitrary")),
        name="nsa_prompt",
    )(nq, kv, kv, kvb, kvb, kvb, kvb, gates, cw_bd, cb2)


def _normmm_kernel(x_ref, g_ref, w_ref, o_ref, ob_ref):
    y = _dot(_rms(x_ref[...], g_ref[...]).astype(BF16), w_ref[...])
    o_ref[...] = y
    ob_ref[...] = y.astype(BF16)


def _normmm_call(x2d, g, w, tm):
    n, d = x2d.shape
    m = w.shape[1]
    return pl.pallas_call(
        _normmm_kernel,
        grid=(n // tm,),
        in_specs=[pl.BlockSpec((tm, d), lambda i: (i, 0)), _full((1, d)), _full((d, m))],
        out_specs=[pl.BlockSpec((tm, m), lambda i: (i, 0))] * 2,
        out_shape=[jax.ShapeDtypeStruct((n, m), F32), jax.ShapeDtypeStruct((n, m), BF16)],
        compiler_params=_cparams(("parallel",)),
        name="norm_matmul",
    )(x2d, g.reshape(1, d), w)


def _mixout_kernel(yr_ref, on_ref, h_ref, wo_ref, g_ref, wq_ref, h1_ref, mq_ref):
    h1 = h_ref[...] + _dot(yr_ref[...], wo_ref[0:D_RET, :]) + _dot(on_ref[...], wo_ref[D_RET:, :])
    h1_ref[...] = h1
    mq_ref[...] = _dot(_rms(h1, g_ref[...]).astype(BF16), wq_ref[...]).astype(BF16)


def _mixout_call(yret, onsa, h, w_out, g_mem, w_mq, tm):
    n = h.shape[0]
    row = lambda w: pl.BlockSpec((tm, w), lambda i: (i, 0))
    return pl.pallas_call(
        _mixout_kernel,
        grid=(n // tm,),
        in_specs=[row(D_RET), row(D_NSA), row(D_MODEL), _full((D_RET + D_NSA, D_MODEL)), _full((1, D_MODEL)),
                  _full((D_MODEL, D_MEM))],
        out_specs=[row(D_MODEL), row(D_MEM)],
        out_shape=[jax.ShapeDtypeStruct((n, D_MODEL), F32), jax.ShapeDtypeStruct((n, D_MEM), BF16)],
        compiler_params=_cparams(("parallel",)),
        name="mixer_out",
    )(yret, onsa, h, w_out, g_mem.reshape(1, D_MODEL), w_mq)


def _memattn_kernel(q_ref, k_ref, v_ref, o_ref):
    q, k, v = q_ref[...], k_ref[...], v_ref[...]
    for h in range(H_MEM):
        sl = slice(h * HD_MEM, (h + 1) * HD_MEM)
        s = _dot_nt(q[:, sl], k[:, sl]) * (HD_MEM ** -0.5)
        p = jnp.exp(s - jnp.max(s, axis=-1, keepdims=True))
        p = p / jnp.sum(p, axis=-1, keepdims=True)
        o_ref[:, sl] = _dot(p.astype(BF16), v[:, sl]).astype(BF16)


def _memattn_call(mq, mkvb, B, L, n_mem, tm):
    nT = L // tm
    return pl.pallas_call(
        _memattn_kernel,
        grid=(B, nT),
        in_specs=[pl.BlockSpec((tm, D_MEM), lambda b, i: (b * nT + i, 0)),
                  pl.BlockSpec((n_mem, D_MEM), lambda b, i: (b, 0)),
                  pl.BlockSpec((n_mem, D_MEM), lambda b, i: (b, 1))],
        out_specs=pl.BlockSpec((tm, D_MEM), lambda b, i: (b * nT + i, 0)),
        out_shape=jax.ShapeDtypeStruct((B * L, D_MEM), BF16),
        compiler_params=_cparams(("parallel", "parallel")),
        name="mem_attention",
    )(mq, mkvb, mkvb)


def _memattn1_kernel(q_ref, k_ref, v_ref, o_ref):
    tb = q_ref.shape[0]
    for b in range(tb):
        q = q_ref[b]
        prod = k_ref[b] * q
        outs = []
        for h in range(H_MEM):
            sl = slice(h * HD_MEM, (h + 1) * HD_MEM)
            s = jnp.sum(prod[:, sl], axis=-1, keepdims=True) * (HD_MEM ** -0.5)
            p = jnp.exp(s - jnp.max(s, axis=0, keepdims=True))
            p = p / jnp.sum(p, axis=0, keepdims=True)
            outs.append(jnp.sum(p * v_ref[b][:, sl], axis=0, keepdims=True))
        o_ref[b] = jnp.concatenate(outs, axis=-1)


def _memattn1_call(mq, cache_k, cache_v):
    DB, n_mem = cache_k.shape[0], cache_k.shape[1]
    tb = 8
    blk = pl.BlockSpec((tb, n_mem, D_MEM), lambda i: (i, 0, 0))
    q3 = pl.BlockSpec((tb, 1, D_MEM), lambda i: (i, 0, 0))
    o = pl.pallas_call(
        _memattn1_kernel,
        grid=(DB // tb,),
        in_specs=[q3, blk, blk],
        out_specs=q3,
        out_shape=jax.ShapeDtypeStruct((DB, 1, D_MEM), F32),
        compiler_params=_cparams(("parallel",)),
        name="mem_attention_step",
    )(mq.astype(F32).reshape(DB, 1, D_MEM), cache_k.reshape(DB, n_mem, D_MEM), cache_v.reshape(DB, n_mem, D_MEM))
    return o.reshape(DB, D_MEM).astype(BF16)


def _prep_peer_keys(subkeys):
    half = PEER_DKEY // 2
    z = jnp.zeros_like(subkeys[:, 0])
    k0 = jnp.concatenate([subkeys[:, 0], z], axis=-1)
    k1 = jnp.concatenate([z, subkeys[:, 1]], axis=-1)
    return jnp.concatenate([k0, k1], axis=1).astype(BF16)


def _top_rows(x, k, payload=None):
    n = x.shape[-2]
    ri = _iota(x.shape, x.ndim - 2)
    vals, picks = [], []
    for _ in range(k):
        m = jnp.max(x, axis=-2, keepdims=True)
        i = jnp.min(jnp.where(x == m, ri, n), axis=-2, keepdims=True)
        hit = ri == i
        vals.append(m)
        picks.append(i if payload is None else jnp.max(jnp.where(hit, payload, -1), axis=-2, keepdims=True))
        x = jnp.where(hit, -jnp.inf, x)
    return jnp.concatenate(vals, axis=-2), jnp.concatenate(picks, axis=-2)


def _route_kernel(om_ref, h1_ref, wo_ref, g_ref, wq_ref, sk_ref, h2_ref, xn_ref, idx_ref, gw_ref,
                  idx_scr, gw_scr):
    tm = h1_ref.shape[0]
    h2 = h1_ref[...] + _dot(om_ref[...], wo_ref[...])
    h2_ref[...] = h2
    xn = _rms(h2, g_ref[...])
    xn_ref[...] = xn
    xb = xn.astype(BF16)

    k = PEER_TOPK
    n_b = [k // (a + 1) for a in range(k)]
    pad = -sum(n_b) % 8
    lanes = min(tm, LANES)

    def head(h, _):
        pq = _dot(xb, wq_ref[h]).astype(BF16)
        s_all = _dot_nt(sk_ref[h], pq)
        rows = pl.ds(pl.multiple_of(h * k, k), k)
        for c in range(tm // lanes):
            s = s_all[:, c * lanes:(c + 1) * lanes].reshape(2, PEER_KEYS, lanes)
            v12, i12 = _top_rows(s, k)
            cand = jnp.concatenate([v12[0][a:a + 1] + v12[1][0:n_b[a]] for a in range(k)]
                                   + [jnp.full((pad, lanes), -jnp.inf, F32)], axis=0)
            cidx = jnp.concatenate([i12[0][a:a + 1] * PEER_KEYS + i12[1][0:n_b[a]] for a in range(k)]
                                   + [jnp.full((pad, lanes), -1, jnp.int32)], axis=0)
            top, expert = _top_rows(cand, k, payload=cidx)
            e = jnp.exp(top - top[0:1, :])
            idx_scr[rows, c * lanes:(c + 1) * lanes] = expert * PEER_ROW
            gw_scr[rows, c * lanes:(c + 1) * lanes] = e / jnp.sum(e, axis=0, keepdims=True)
        return 0

    lax.fori_loop(0, PEER_HEADS, head, 0)
    idx_ref[...] = idx_scr[...].T
    gw_ref[...] = gw_scr[...].T


def _route_call(omem, h1, w_mo, g_ffn, wq_h, sk_pad, tm):
    n = h1.shape[0]
    nk = PEER_HEADS * PEER_TOPK
    row = lambda w: pl.BlockSpec((tm, w), lambda i: (i, 0))
    return pl.pallas_call(
        _route_kernel,
        grid=(n // tm,),
        in_specs=[row(D_MEM), row(D_MODEL), _full((D_MEM, D_MODEL)), _full((1, D_MODEL)),
                  _full((PEER_HEADS, D_MODEL, PEER_DKEY)), _full((PEER_HEADS, 2 * PEER_KEYS, PEER_DKEY))],
        out_specs=[row(D_MODEL), row(D_MODEL), row(nk), row(nk)],
        out_shape=[jax.ShapeDtypeStruct((n, D_MODEL), F32), jax.ShapeDtypeStruct((n, D_MODEL), F32),
                   jax.ShapeDtypeStruct((n, nk), jnp.int32), jax.ShapeDtypeStruct((n, nk), F32)],
        scratch_shapes=[pltpu.VMEM((nk, tm), jnp.int32), pltpu.VMEM((nk, tm), F32)],
        compiler_params=_cparams(("parallel",)),
        name="peer_route",
    )(omem, h1, w_mo, g_ffn.reshape(1, D_MODEL), wq_h, sk_pad)


def _gelu_tanh(x):
    return 0.5 * x * (1.0 + jnp.tanh(0.7978845608028654 * (x + 0.044715 * x * x * x)))


PEER_TT = 64


def _pack_table(t):
    e, d = t.shape
    b = lax.bitcast_convert_type(t.astype(BF16), jnp.uint16).astype(jnp.uint32)
    w = b[:, :d // 2] | (b[:, d // 2:] << 16)
    return w.reshape(e * PEER_ROW, LANES)


def _expert_row(tab_ref, off):
    w = tab_ref[pl.ds(pl.multiple_of(off, PEER_ROW), PEER_ROW), :]
    lo = pltpu.bitcast(w << 16, F32)
    hi = pltpu.bitcast(w & jnp.uint32(0xFFFF0000), F32)
    return lo, hi


def _peer_act_kernel(idx_ref, x_ref, gw_ref, tab_ref, c_ref, part_scr, act_scr):
    tt, nk = gw_ref.shape
    sub = _iota((8, LANES), 0)
    keep_pairs = (sub % 4) < 2
    keep_even = (sub % 2) == 0
    feed = (0, 4, 2, 6, 1, 5, 3, 7)

    def row_sums8(p):
        p = [p[i] for i in feed]
        v = [jnp.concatenate([p[2 * i], p[2 * i + 1]], axis=0) for i in range(4)]
        w = [x + pltpu.roll(x, 6, axis=0) for x in v]
        u = [jnp.where(keep_pairs, w[2 * i], pltpu.roll(w[2 * i + 1], 2, axis=0)) for i in range(2)]
        z = [x + pltpu.roll(x, 7, axis=0) for x in u]
        return jnp.where(keep_even, z[0], pltpu.roll(z[1], 1, axis=0))

    def token(t, _):
        xt = x_ref[t]
        xlo, xhi = xt[0:PEER_ROW], xt[PEER_ROW:]

        for j0 in range(0, nk, 8):
            prods = []
            for j in range(j0, j0 + 8):
                lo, hi = _expert_row(tab_ref, idx_ref[t, j])
                prods.append(lo * xlo + hi * xhi)
            part_scr[t, j0:j0 + 8, :] = row_sums8(prods)
        return 0

    lax.fori_loop(0, tt, token, 0)

    def finish(t, _):
        act_scr[pl.ds(t, 1), :] = jnp.sum(part_scr[t].T, axis=0, keepdims=True)
        return 0

    lax.fori_loop(0, tt, finish, 0, unroll=4)
    c_ref[...] = gw_ref[...] * _gelu_tanh(act_scr[...])


def _peer_out_kernel(idx_ref, c_ref, h2_ref, gf_ref, tab_ref, y_ref, splat_scr):
    tt, nk = idx_ref.shape
    n_acc = 8

    def token(t, _):
        splat_scr[...] = jnp.broadcast_to(c_ref[pl.ds(t, 1), :], (nk, nk)).T
        acc_lo = [jnp.zeros((PEER_ROW, LANES), F32)] * n_acc
        acc_hi = [jnp.zeros((PEER_ROW, LANES), F32)] * n_acc
        for j in range(nk):
            lo, hi = _expert_row(tab_ref, idx_ref[t, j])
            c = splat_scr[j:j + 1, :]
            acc_lo[j % n_acc] = acc_lo[j % n_acc] + c * lo
            acc_hi[j % n_acc] = acc_hi[j % n_acc] + c * hi
        tree = lambda v: v[0] if len(v) == 1 else tree([a + b for a, b in zip(v[0::2], v[1::2])])
        y_ref[t] = h2_ref[t] + jnp.concatenate([tree(acc_lo), tree(acc_hi)], axis=0)
        return 0

    lax.fori_loop(0, tt, token, 0)
    h3 = y_ref[...]
    ms = jnp.sum(jnp.sum(h3 * h3, axis=2, keepdims=True), axis=1, keepdims=True) * (1.0 / D_MODEL)
    y_ref[...] = h3 * lax.rsqrt(ms + EPS) * gf_ref[...]


def _peer_call(idx, xn, gw, h2, g_final, u_tab, v_tab, tt):
    n, nk = idx.shape
    assert n % tt == 0 and nk % 8 == 0
    smem = lambda: pl.BlockSpec((tt, nk), lambda i: (i, 0), memory_space=pltpu.SMEM)
    tile = lambda: pl.BlockSpec((tt, 8, LANES), lambda i: (i, 0, 0))
    table = lambda t: pl.BlockSpec(t.shape, lambda i: (0, 0), pipeline_mode=pl.Buffered(1))
    as_tiles = lambda a: a.reshape(n, 8, LANES)
    c = pl.pallas_call(
        _peer_act_kernel,
        grid=(n // tt,),
        in_specs=[smem(), tile(), pl.BlockSpec((tt, nk), lambda i: (i, 0)), table(u_tab)],
        out_specs=pl.BlockSpec((tt, nk), lambda i: (i, 0)),
        out_shape=jax.ShapeDtypeStruct((n, nk), F32),
        scratch_shapes=[pltpu.VMEM((tt, nk, LANES), F32), pltpu.VMEM((tt, nk), F32)],
        compiler_params=_cparams(("arbitrary",)),
        name="peer_act",
    )(idx, as_tiles(xn), gw, u_tab)
    y = pl.pallas_call(
        _peer_out_kernel,
        grid=(n // tt,),
        in_specs=[smem(), pl.BlockSpec((tt, nk), lambda i: (i, 0)), tile(), _full((8, LANES)), table(v_tab)],
        out_specs=tile(),
        out_shape=jax.ShapeDtypeStruct((n, 8, LANES), F32),
        scratch_shapes=[pltpu.VMEM((nk, nk), F32)],
        compiler_params=_cparams(("arbitrary",)),
        name="peer_out",
    )(idx, c, as_tiles(h2), g_final.reshape(8, LANES), v_tab)
    return y.reshape(n, D_MODEL)


CMP_PAGES = 128


def _cmp_pages_kernel(x_ref, w_ref, b_ref, o_ref):
    n_rows = o_ref.shape[0]

    def body(d, acc):
        x = x_ref[pl.ds(d, n_rows, stride=HD_NSA), :].astype(BF16)
        return acc + _dot(x, w_ref[d])

    acc = lax.fori_loop(0, HD_NSA, body, jnp.zeros((n_rows, LANES), F32))
    o_ref[...] = acc + b_ref[...]


def _cmp_pages_call(pool_t, w_bd, b):
    n_phys = pool_t.shape[0]
    rows = n_phys * G_NSA
    step = CMP_PAGES * G_NSA
    assert rows % step == 0
    return pl.pallas_call(
        _cmp_pages_kernel,
        grid=(rows // step,),
        in_specs=[pl.BlockSpec((step * HD_NSA, PAGE_SIZE), lambda i: (i, 0)),
                  _full((HD_NSA, PAGE_SIZE, LANES)), _full((1, LANES))],
        out_specs=pl.BlockSpec((step, LANES), lambda i: (i, 0)),
        out_shape=jax.ShapeDtypeStruct((rows, LANES), F32),
        compiler_params=_cparams(("parallel",)),
        name="compress_pages",
    )(pool_t.reshape(rows * HD_NSA, PAGE_SIZE), w_bd, jnp.tile(b, 2).reshape(1, LANES))


def _nsa1_cmp_kernel(pt_ref, nq_ref, ckn_ref, cvn_ref, tk_ref, tv_ref, cw_ref, cb_ref, ocmp_ref, idx_ref,
                     kg_scr, vg_scr, *, n_pages, q_pos):
    tb = nq_ref.shape[0]
    nb_past = n_pages * (PAGE_SIZE // CMP_BLK)
    base = pl.program_id(0) * tb
    lane = _iota((1, LANES), 1)
    blk_n = 2 * (lane % HD_NSA) + lane // HD_NSA
    forced = (blk_n == 0) | (blk_n == q_pos // CMP_BLK)
    vis = (blk_n * CMP_BLK + CMP_BLK - 1) <= q_pos
    new_vis = (nb_past * CMP_BLK + CMP_BLK - 1) <= q_pos
    new_forced = nb_past == q_pos // CMP_BLK
    n_row = jnp.broadcast_to(blk_n, (LANES, LANES))
    n_col = 2 * (_iota((LANES, LANES), 0) % HD_NSA) + _iota((LANES, LANES), 0) // HD_NSA
    lane8 = _iota((1, LANES), 1) // HD_NSA

    def sample(b, _):
        qrow = nq_ref[pl.ds(b, 1), :].astype(F32)
        new_k = _dot(ckn_ref[pl.ds(b, 1), :].astype(BF16), cw_ref[0]) + cb_ref[0]
        new_v = _dot(cvn_ref[pl.ds(b, 1), :].astype(BF16), cw_ref[1]) + cb_ref[1]
        o_row, idx_row = [], jnp.full((1, LANES), -1, jnp.int32)
        for g in range(G_NSA):
            def gather(i, _):
                r = pt_ref[base + b, i] * G_NSA + g
                kg_scr[pl.ds(i, 1), :] = tk_ref[pl.ds(r, 1), :]
                vg_scr[pl.ds(i, 1), :] = tv_ref[pl.ds(r, 1), :]
                return 0
            lax.fori_loop(0, n_pages, gather, 0)
            rows = []
            for c in range(2):
                for r in range(R_NSA):
                    h = g * R_NSA + r
                    x = qrow[:, (h // 2) * LANES:(h // 2 + 1) * LANES]
                    if h % 2 != c:
                        x = pltpu.roll(x, HD_NSA, axis=1)
                    rows.append(jnp.where(lane8 == c, x, 0.0))
            qpad = jnp.concatenate(rows, axis=0)
            s = _dot_nt(qpad.astype(BF16), kg_scr[...].astype(BF16)) * (HD_NSA ** -0.5)
            s3 = s.reshape(2, R_NSA, n_pages)
            qg = qpad[R_NSA * g:R_NSA * (g + 1), :]
            nk_g = jnp.where(lane8 == g, new_k.astype(BF16).astype(F32), 0.0)
            s_new = jnp.sum(qg.astype(BF16).astype(F32) * nk_g, axis=-1, keepdims=True) * (HD_NSA ** -0.5)
            s_new = jnp.where(new_vis, s_new, NEG)[None]
            m = jnp.maximum(jnp.max(jnp.max(s3, axis=2, keepdims=True), axis=0, keepdims=True), s_new)
            p = jnp.exp(s3 - m)
            p_new = jnp.where(s_new > 0.5 * NEG, jnp.exp(s_new - m), 0.0)
            l = jnp.sum(jnp.sum(p, axis=2, keepdims=True), axis=0, keepdims=True) + p_new
            inv = 1.0 / jnp.maximum(l, 1e-30)
            p = p * inv
            p_new = p_new * inv
            res = _dot(p.reshape(2 * R_NSA, n_pages).astype(BF16), vg_scr[...].astype(BF16))
            o4 = res[0:R_NSA] + pltpu.roll(res[R_NSA:], HD_NSA, axis=1)
            nv_g = new_v.astype(BF16).astype(F32)
            if g == 1:
                nv_g = pltpu.roll(nv_g, HD_NSA, axis=1)
            o4 = o4 + p_new[0].astype(BF16).astype(F32) * nv_g
            o_row += [o4[r:r + 1, 0:HD_NSA] for r in range(R_NSA)]
            imp2 = jnp.sum(p, axis=1)
            imp = jnp.concatenate([imp2[0:1], imp2[1:2]], axis=-1)
            score = jnp.where(forced, SEL_FORCE, jnp.where(vis, imp, -1.0))
            imp_new = jnp.sum(p_new)
            sc_new = SEL_FORCE if new_forced else jnp.where(new_vis, imp_new, -1.0)
            a = jnp.broadcast_to(score, (LANES, LANES))
            bt = a.T
            ahead = (bt > a) | ((bt == a) & (n_col < n_row))
            rank = jnp.sum(jnp.where(ahead, 1.0, 0.0), axis=0, keepdims=True) + jnp.where(sc_new > score, 1.0, 0.0)
            rank_new = jnp.sum(jnp.where(score >= sc_new, 1.0, 0.0))
            for r in range(N_SEL):
                hit = (rank == r) & (score >= 0.0)
                val = jnp.sum(jnp.where(hit, blk_n + 1, 0)) - 1
                val = jnp.where((rank_new == r) & (sc_new >= 0.0), nb_past, val)
                idx_row = jnp.where(lane == g * N_SEL + r, val, idx_row)
        ocmp_ref[pl.ds(b, 1), :] = jnp.concatenate(o_row, axis=-1)
        idx_ref[pl.ds(b, 1), :] = idx_row
        return 0

    lax.fori_loop(0, tb, sample, 0)


def _nsa1_cmp_call(page_table, nq, ck_new, cv_new, tbl_k, tbl_v, cw_bd, cb2, q_pos):
    DB, n_pages = page_table.shape
    assert n_pages * (PAGE_SIZE // CMP_BLK) == LANES
    tb = 8
    row = lambda w: pl.BlockSpec((tb, w), lambda i, pt: (i, 0))
    whole = lambda a: pl.BlockSpec(a.shape, lambda i, pt: (0,) * a.ndim, pipeline_mode=pl.Buffered(1))
    return pl.pallas_call(
        functools.partial(_nsa1_cmp_kernel, n_pages=n_pages, q_pos=q_pos),
        grid_spec=pltpu.PrefetchScalarGridSpec(
            num_scalar_prefetch=1,
            grid=(DB // tb,),
            in_specs=[row(D_NSA), row(LANES), row(LANES), whole(tbl_k), whole(tbl_v),
                      pl.BlockSpec((2, LANES, LANES), lambda i, pt: (0, 0, 0)),
                      pl.BlockSpec((2, 1, LANES), lambda i, pt: (0, 0, 0))],
            out_specs=[row(D_NSA), row(LANES)],
            scratch_shapes=[pltpu.VMEM((n_pages, LANES), F32), pltpu.VMEM((n_pages, LANES), F32)]),
        out_shape=[jax.ShapeDtypeStruct((DB, D_NSA), F32), jax.ShapeDtypeStruct((DB, LANES), jnp.int32)],
        compiler_params=_cparams(("arbitrary",)),
        name="nsa_step_compressed",
    )(page_table, nq.astype(F32), ck_new, cv_new, tbl_k, tbl_v, cw_bd[:, 0], cb2)


def _nsa1_attn_kernel(idx_ref, pt_ref, nq_ref, kvn_ref, gt_ref, ocmp_ref, wk_ref, wv_ref, sk_hbm, sv_hbm,
                      o_ref, kbuf, vbuf, sem, *, n_pages, q_pos, past_len):
    b = pl.program_id(0)
    bpp = PAGE_SIZE // CMP_BLK
    nb_past = n_pages * bpp
    wb = wk_ref.shape[3]

    def block_copies(bb, slot, g, r):
        n = idx_ref[bb, g * N_SEL + r]
        past = (n >= 0) & (n < nb_past)
        page = pt_ref[bb, jnp.clip(n, 0, nb_past - 1) // bpp]
        ck = pltpu.make_async_copy(sk_hbm.at[page, g], kbuf.at[slot, g, r], sem.at[slot, 0])
        cv = pltpu.make_async_copy(sv_hbm.at[page, g], vbuf.at[slot, g, r], sem.at[slot, 1])
        return past, ck, cv

    def issue(bb, slot):
        for g in range(G_NSA):
            for r in range(N_SEL):
                past, ck, cv = block_copies(bb, slot, g, r)

                @pl.when(past)
                def _():
                    ck.start()
                    cv.start()

                @pl.when(jnp.logical_not(past))
                def _():
                    kbuf[slot, g, r] = jnp.zeros((HD_NSA, PAGE_SIZE), F32)
                    vbuf[slot, g, r] = jnp.zeros((HD_NSA, PAGE_SIZE), F32)

    @pl.when(b == 0)
    def _():
        issue(0, 0)

    @pl.when(b + 1 < pl.num_programs(0))
    def _():
        issue(b + 1, (b + 1) % 2)

    slot = b % 2
    for g in range(G_NSA):
        for r in range(N_SEL):
            past, ck, cv = block_copies(b, slot, g, r)

            @pl.when(past)
            def _():
                ck.wait()
                cv.wait()

    qrow = nq_ref[0].astype(F32)
    kvn = kvn_ref[0]
    gt = gt_ref[0]
    ocmp = ocmp_ref[0]
    scale = HD_NSA ** -0.5
    blk_in_page = _iota((1, PAGE_SIZE), 1) // CMP_BLK
    wpos = past_len - wb + _iota((1, wb), 1)
    wdist = q_pos - wpos
    w_ok = (wdist >= 0) & (wdist < WINDOW) & (wpos >= 0)
    bfr = lambda t: t.astype(BF16).astype(F32)
    pieces = []
    for g in range(G_NSA):
        q4 = jnp.concatenate([qrow[:, (g * R_NSA + r) * HD_NSA:(g * R_NSA + r + 1) * HD_NSA]
                              for r in range(R_NSA)], axis=0)
        q4b = q4.astype(BF16)
        new = lambda i: kvn[:, i * D_KV + g * HD_NSA:i * D_KV + (g + 1) * HD_NSA]
        scores, keeps = [], []
        has_new = jnp.int32(0)
        for r in range(N_SEL):
            n = idx_ref[b, g * N_SEL + r]
            past = ((n >= 0) & (n < nb_past)).astype(jnp.int32)
            keep = (blk_in_page == n % bpp) & (past > 0)
            has_new = has_new | (n == nb_past).astype(jnp.int32)
            keeps.append(keep)
            scores.append(jnp.where(keep, _dot(q4b, kbuf[slot, g, r].astype(BF16)) * scale, NEG))
        s_new = jnp.sum(bfr(q4) * bfr(new(2)), axis=-1, keepdims=True) * scale
        s_new = jnp.where(has_new > 0, s_new, NEG)
        m = s_new
        for s in scores:
            m = jnp.maximum(m, jnp.max(s, axis=-1, keepdims=True))
        p_new = jnp.where(s_new > 0.5 * NEG, jnp.exp(s_new - m), 0.0)
        l = p_new
        acc = bfr(p_new) * bfr(new(3))
        for r in range(N_SEL):
            p = jnp.where(scores[r] > 0.5 * NEG, jnp.exp(scores[r] - m), 0.0)
            l = l + jnp.sum(p, axis=-1, keepdims=True)
            v_t = jnp.where(keeps[r], vbuf[slot, g, r], 0.0).astype(BF16)
            acc = acc + _dot_nt(p.astype(BF16), v_t)
        o_slc = acc / jnp.maximum(l, 1e-30)
        wk = wk_ref[0, g].astype(BF16)
        wv = wv_ref[0, g].astype(BF16)
        s = jnp.where(w_ok, _dot(q4b, wk) * scale, NEG)
        s_new = jnp.sum(bfr(q4) * bfr(new(4)), axis=-1, keepdims=True) * scale
        m = jnp.maximum(jnp.max(s, axis=-1, keepdims=True), s_new)
        p = jnp.where(s > 0.5 * NEG, jnp.exp(s - m), 0.0)
        p_new = jnp.exp(s_new - m)
        l = jnp.sum(p, axis=-1, keepdims=True) + p_new
        o_win = (_dot_nt(p.astype(BF16), wv) + bfr(p_new) * bfr(new(5))) / jnp.maximum(l, 1e-30)
        for r in range(R_NSA):
            h = g * R_NSA + r
            pieces.append(gt[:, 3 * h:3 * h + 1] * ocmp[:, h * HD_NSA:(h + 1) * HD_NSA]
                          + gt[:, 3 * h + 1:3 * h + 2] * o_slc[r:r + 1]
                          + gt[:, 3 * h + 2:3 * h + 3] * o_win[r:r + 1])
    o_ref[0] = jnp.concatenate(pieces, axis=-1)


def _nsa1_attn_call(idx, page_table, nq, kv_new, gates, ocmp, win_k, win_v, slc_k, slc_v, q_pos, past_len):
    DB, n_pages = page_table.shape
    wb = win_k.shape[3]
    r3 = lambda a: a.reshape(DB, 1, a.shape[-1])
    row = lambda w: pl.BlockSpec((1, 1, w), lambda b, *_: (b, 0, 0))
    win = pl.BlockSpec((1, G_NSA, HD_NSA, wb), lambda b, *_: (b, 0, 0, 0))
    o = pl.pallas_call(
        functools.partial(_nsa1_attn_kernel, n_pages=n_pages, q_pos=q_pos, past_len=past_len),
        grid_spec=pltpu.PrefetchScalarGridSpec(
            num_scalar_prefetch=2,
            grid=(DB,),
            in_specs=[row(D_NSA), row(6 * D_KV), row(LANES), row(D_NSA), win, win,
                      pl.BlockSpec(memory_space=pl.ANY), pl.BlockSpec(memory_space=pl.ANY)],
            out_specs=row(D_NSA),
            scratch_shapes=[pltpu.VMEM((2, G_NSA, N_SEL, HD_NSA, PAGE_SIZE), F32),
                            pltpu.VMEM((2, G_NSA, N_SEL, HD_NSA, PAGE_SIZE), F32),
                            pltpu.SemaphoreType.DMA((2, 2))]),
        out_shape=jax.ShapeDtypeStruct((DB, 1, D_NSA), F32),
        compiler_params=_cparams(("arbitrary",)),
        name="nsa_step_attend",
    )(idx, page_table, r3(nq), r3(kv_new), r3(gates), r3(ocmp), win_k, win_v, slc_k, slc_v)
    return o.reshape(DB, D_NSA).astype(BF16)


def _token_tail(yret, onsa, h, mem_attend, lw, tm):
    w_out, g_mem, w_mq, w_mo, g_ffn, wq_h, sk_pad, u, v, g_final = lw
    h1, mq = _mixout_call(yret, onsa, h, w_out, g_mem, w_mq, tm)
    omem = mem_attend(mq)
    h2, xn, idx, gw = _route_call(omem, h1, w_mo, g_ffn, wq_h, sk_pad, tm)
    return _peer_call(idx, xn, gw, h2, g_final, u, v, min(PEER_TT, h.shape[0]))


def _all_tables(pos):
    return (_rot_tables(pos, H_RET, HD_RET, HD_RET, RET_THETA)
            + _rot_tables(pos, H_NSA, HD_NSA, ROPE_DIMS, ROPE_THETA)
            + _rot_tables(pos, G_NSA, HD_NSA, ROPE_DIMS, ROPE_THETA))


def kernel(x_prompt, x_sample, mem_prompt, state_ret, cache_cmp_k, cache_cmp_v, cache_slc_k, cache_slc_v,
           cache_win_k, cache_win_v, cache_mem_k, cache_mem_v, page_table, norm_mix_g, w_in, ret_gn_g,
           cmp_w, cmp_b, w_out, norm_mem_g, mem_norm_g, w_mq, w_mk, w_mv, w_mo, norm_ffn_g,
           peer_wq, peer_subkeys, peer_u, peer_v, norm_final_g):
    B, L, D = x_prompt.shape
    DB, LS, _ = x_sample.shape
    n_mem = mem_prompt.shape[1]
    n_pages = page_table.shape[1]
    past_len = n_pages * PAGE_SIZE
    assert w_in.shape[0] == 1 and LS == 1 and D == D_MODEL
    l = 0
    tm = 256

    w_all = _prep_w_in(w_in[l])
    cw_bd = _prep_cmp_w(cmp_w[l])
    cb2 = jnp.tile(cmp_b[l], (1, G_NSA)).reshape(2, 1, LANES)
    w_kv = jnp.concatenate([w_mk[l], w_mv[l]], axis=1).astype(BF16)
    wq_h = peer_wq[l].reshape(D, PEER_HEADS, PEER_DKEY).transpose(1, 0, 2).astype(BF16)
    lw = (w_out[l].astype(BF16), norm_mem_g[l], w_mq[l].astype(BF16), w_mo[l].astype(BF16), norm_ffn_g[l],
          wq_h, _prep_peer_keys(peer_subkeys[l]), _pack_table(peer_u[l]), _pack_table(peer_v[l]), norm_final_g)

    xs = x_sample.reshape(DB, D)
    rq, rk, rv, rg, nq, kvs, _, gt = _proj_call(xs, norm_mix_g[l], w_all,
                                                _all_tables(jnp.full((DB,), past_len, jnp.int32)), DB)
    yret, s_state = _ret1_call(rq, rk, rv, rg, ret_gn_g[l], state_ret[l])
    fm = lambda cache: jnp.swapaxes(cache[l], -1, -2)
    cw_pg = _prep_cmp_w(cmp_w[l].transpose(0, 2, 1, 3))
    tbl_k = _cmp_pages_call(fm(cache_cmp_k), cw_pg[0], cmp_b[l, 0])
    tbl_v = _cmp_pages_call(fm(cache_cmp_v), cw_pg[1], cmp_b[l, 1])
    ocmp, sel_idx = _nsa1_cmp_call(page_table, nq, kvs[:, 0:D_KV], kvs[:, D_KV:2 * D_KV], tbl_k, tbl_v, cw_bd, cb2,
                                   past_len)
    onsa = _nsa1_attn_call(sel_idx, page_table, nq, kvs, gt, ocmp, fm(cache_win_k), fm(cache_win_v),
                           fm(cache_slc_k), fm(cache_slc_v), past_len, past_len)
    y_s = _token_tail(yret, onsa, xs, lambda mq: _memattn1_call(mq, cache_mem_k[l], cache_mem_v[l]), lw, DB)

    new = lambda i: kvs[:, i * D_KV:(i + 1) * D_KV].reshape(DB, G_NSA, 1, HD_NSA)
    wb = cache_win_k.shape[3]
    keep_s = min(WINDOW, wb + 1)
    win = lambda cache, i: jnp.concatenate([cache[l], new(i)], axis=2)[:, :, wb + 1 - keep_s:]

    xp = x_prompt.reshape(B * L, D)
    rq, rk, rv, rg, nq, kv, kvb, gt = _proj_call(xp, norm_mix_g[l], w_all,
                                                 _all_tables(jnp.arange(L, dtype=jnp.int32)), tm)
    yret, p_state = _ret_call(rq, rk, rv, rg, ret_gn_g[l], B, L)
    onsa = _nsa_call(nq, kv, kvb, gt, cw_bd, cb2, B, L)
    mkv, mkvb = _normmm_call(mem_prompt.reshape(B * n_mem, D), mem_norm_g[l], w_kv, tm)
    y_p = _token_tail(yret, onsa, xp, lambda mq: _memattn_call(mq, mkvb, B, L, n_mem, tm), lw, tm)

    kv6 = kv.reshape(B, L, 6, G_NSA, HD_NSA)
    pages = lambda i: kv6[:, :, i].reshape(B, L // PAGE_SIZE, PAGE_SIZE, G_NSA, HD_NSA).transpose(0, 1, 3, 2, 4)
    keep = min(WINDOW, L)
    tail = lambda i: kv6[:, L - keep:, i].transpose(0, 2, 1, 3)
    mem4 = lambda t: t.reshape(B, n_mem, H_MEM, HD_MEM)
    st = lambda t: t[None]
    return (y_p.reshape(B, L, D), y_s.reshape(DB, 1, D), st(p_state),
            st(pages(0)), st(pages(1)), st(pages(2)), st(pages(3)), st(tail(4)), st(tail(5)),
            st(mem4(mkv[:, :D_MEM])), st(mem4(mkv[:, D_MEM:])), st(s_state),
            st(new(0)), st(new(1)), st(new(2)), st(new(3)), st(win(cache_win_k, 4)), st(win(cache_win_v, 5)))
```

```python
import functools

import numpy as np
import jax
import jax.numpy as jnp
from jax import lax
from jax.experimental import pallas as pl
from jax.experimental.pallas import tpu as pltpu

F32 = jnp.float32
BF16 = jnp.bfloat16

D_MODEL = 1024
PAGE_SIZE = 128
H_RET = 8
HD_RET = 64
RET_CHUNK = 128
RET_THETA = 10000.0
H_NSA = 8
G_NSA = 2
R_NSA = H_NSA // G_NSA
HD_NSA = 64
CMP_BLK = 64
N_SEL = 16
WINDOW = 512
SEL_FORCE = 1.0e4
ROPE_THETA = 500000.0
ROPE_DIMS = HD_NSA // 4
H_MEM = 4
HD_MEM = 128
PEER_KEYS = 128
PEER_HEADS = 8
PEER_DKEY = 128
PEER_TOPK = 16
PEER_ROW = 4
EPS = 1e-6

D_RET = H_RET * HD_RET
D_NSA = H_NSA * HD_NSA
D_KV = G_NSA * HD_NSA
D_MEM = H_MEM * HD_MEM
NEG = -1.0e30

LANES = 128
VMEM_LIMIT = 56 * 1024 * 1024

C_RQ, C_RQR, C_RK, C_RKR, C_RV, C_RG, C_NQ, C_NQR = (i * 512 for i in range(8))
C_KV = 4096
C_KVR = C_KV + 6 * D_KV
C_NG = C_KVR + 3 * D_KV
N_PROJ = C_NG + LANES


def _cparams(sem):
    return pltpu.CompilerParams(dimension_semantics=sem, vmem_limit_bytes=VMEM_LIMIT)


def _rms(x, g):
    return x * lax.rsqrt(jnp.mean(x * x, axis=-1, keepdims=True) + EPS) * g


def _dot(a, b):
    return jnp.dot(a, b, preferred_element_type=F32)


def _dot_nt(a, b):
    return lax.dot_general(a, b, (((1,), (1,)), ((), ())), preferred_element_type=F32)


def _full(shape):
    n = len(shape)
    return pl.BlockSpec(shape, lambda *_: (0,) * n)


def _partner_cols(n_heads, hd, n_rot):
    half = n_rot // 2
    j = np.arange(hd)
    p = np.where(j < half, j + half, np.where(j < n_rot, j - half, j))
    return (np.arange(n_heads)[:, None] * hd + p[None, :]).reshape(-1)


def _prep_w_in(w_in):
    o = 0
    seg = {}
    for name, size in (("rq", D_RET), ("rk", D_RET), ("rv", D_RET), ("rg", D_RET), ("nq", D_NSA),
                       ("ck", D_KV), ("cv", D_KV), ("sk", D_KV), ("sv", D_KV), ("wk", D_KV), ("wv", D_KV),
                       ("ng", 3 * H_NSA)):
        seg[name] = (o, size)
        o += size
    cols = lambda n: np.arange(seg[n][0], seg[n][0] + seg[n][1])
    pr = _partner_cols(H_RET, HD_RET, HD_RET)
    pn = _partner_cols(H_NSA, HD_NSA, ROPE_DIMS)
    pk = _partner_cols(G_NSA, HD_NSA, ROPE_DIMS)
    order = np.concatenate([
        cols("rq"), cols("rq")[pr], cols("rk"), cols("rk")[pr], cols("rv"), cols("rg"),
        cols("nq"), cols("nq")[pn],
        cols("ck"), cols("cv"), cols("sk"), cols("sv"), cols("wk"), cols("wv"),
        cols("ck")[pk], cols("sk")[pk], cols("wk")[pk], cols("ng")])
    w = jnp.take(w_in, jnp.asarray(order, jnp.int32), axis=1)
    w = jnp.pad(w, ((0, 0), (0, N_PROJ - w.shape[1])))
    return w.astype(BF16)


def _rot_tables(pos, n_heads, hd, n_rot, theta):
    half = n_rot // 2
    inv = theta ** (-jnp.arange(half, dtype=F32) / half)
    ang = pos.astype(F32)[:, None] * inv[None, :]
    cos, sin = jnp.cos(ang), jnp.sin(ang)
    P = pos.shape[0]
    c = jnp.concatenate([cos, cos, jnp.ones((P, hd - n_rot), F32)], axis=1)
    s = jnp.concatenate([-sin, sin, jnp.zeros((P, hd - n_rot), F32)], axis=1)
    return jnp.tile(c, (1, n_heads)), jnp.tile(s, (1, n_heads))


def _proj_kernel(x_ref, g_ref, w_ref, cr_ref, sr_ref, cn_ref, sn_ref, ck_ref, sk_ref,
                 rq_ref, rk_ref, rv_ref, rg_ref, nq_ref, kv_ref, kvb_ref, gt_ref):
    hn = _rms(x_ref[...], g_ref[...]).astype(BF16)
    seg = lambda c0, n: _dot(hn, w_ref[:, c0:c0 + n])
    cr, sr = cr_ref[...], sr_ref[...]
    rq_ref[...] = (seg(C_RQ, 512) * cr + seg(C_RQR, 512) * sr).astype(BF16)
    rk_ref[...] = ((seg(C_RK, 512) * cr + seg(C_RKR, 512) * sr) * (HD_RET ** -0.5)).astype(BF16)
    rv_ref[...] = seg(C_RV, 512).astype(BF16)
    rg_ref[...] = seg(C_RG, 512)
    nq_ref[...] = (seg(C_NQ, 512) * cn_ref[...] + seg(C_NQR, 512) * sn_ref[...]).astype(BF16)
    ck, sk = ck_ref[...], sk_ref[...]
    for i in range(6):
        a = seg(C_KV + i * D_KV, D_KV)
        if i % 2 == 0:
            a = a * ck + seg(C_KVR + (i // 2) * D_KV, D_KV) * sk
        kv_ref[:, i * D_KV:(i + 1) * D_KV] = a
        kvb_ref[:, i * D_KV:(i + 1) * D_KV] = a.astype(BF16)
    z = seg(C_NG, LANES)
    gt_ref[...] = 1.0 / (1.0 + jnp.exp(-z))


def _proj_call(x2d, g, w_all, tabs, tm):
    n = x2d.shape[0]
    period = tabs[0].shape[0] // tm
    row = lambda w: pl.BlockSpec((tm, w), lambda i: (i, 0))
    tab = lambda w: pl.BlockSpec((tm, w), lambda i: (i % period, 0))
    outs = [(512, BF16), (512, BF16), (512, BF16), (512, F32), (512, BF16), (6 * D_KV, F32), (6 * D_KV, BF16),
            (LANES, F32)]
    return pl.pallas_call(
        _proj_kernel,
        grid=(n // tm,),
        in_specs=[row(D_MODEL), _full((1, D_MODEL)), _full((D_MODEL, N_PROJ)),
                  tab(512), tab(512), tab(512), tab(512), tab(D_KV), tab(D_KV)],
        out_specs=[row(w) for w, _ in outs],
        out_shape=[jax.ShapeDtypeStruct((n, w), dt) for w, dt in outs],
        compiler_params=_cparams(("parallel",)),
        name="proj",
    )(x2d, g.reshape(1, D_MODEL), w_all, *tabs)


def _ret_consts(C):
    lg = jnp.log(1.0 - 2.0 ** (-5.0 - jnp.arange(H_RET, dtype=F32)))
    idx = jnp.arange(C, dtype=F32)
    diff = idx[:, None] - idx[None, :]
    dmat = jnp.where(diff >= 0, jnp.exp(lg[:, None, None] * jnp.maximum(diff, 0.0)), 0.0)
    xi = jnp.exp(lg[None, :] * (idx[:, None] + 1.0))
    zeta = jnp.exp(lg[:, None] * (C - 1.0 - idx[None, :]))
    g_c = jnp.exp(lg * C)
    return dmat, xi, zeta, g_c


def _ret_kernel(gc_ref, q_ref, k_ref, v_ref, rg_ref, gn_ref, dmat_ref, xi_ref, zeta_ref,
                y_ref, st_ref, s_scr):
    c = pl.program_id(1)

    @pl.when(c == 0)
    def _():
        s_scr[...] = jnp.zeros_like(s_scr)

    q, k, v = q_ref[...], k_ref[...], v_ref[...]
    k_t = k.astype(F32).T
    outs = []
    for h in range(H_RET):
        sl = slice(h * HD_RET, (h + 1) * HD_RET)
        qh, kh, vh = q[:, sl], k[:, sl], v[:, sl]
        att = _dot_nt(qh, kh) * dmat_ref[h]
        inner = _dot(att.astype(BF16), vh)
        s_old = s_scr[h]
        cross = _dot(qh, s_old.astype(BF16)) * xi_ref[:, h:h + 1]
        o = inner + cross
        kz = (k_t[sl, :] * zeta_ref[h:h + 1, :]).astype(BF16)
        s_scr[h] = s_old * gc_ref[h] + _dot(kz, vh)
        mu = jnp.mean(o, axis=-1, keepdims=True)
        d = o - mu
        var = jnp.mean(d * d, axis=-1, keepdims=True)
        outs.append(d * lax.rsqrt(var + EPS))
    on = jnp.concatenate(outs, axis=-1)
    rg = rg_ref[...]
    silu = rg * (1.0 / (1.0 + jnp.exp(-rg)))
    y_ref[...] = (silu * (on * gn_ref[...])).astype(BF16)

    @pl.when(c == pl.num_programs(1) - 1)
    def _():
        st_ref[0] = s_scr[...]


def _ret_call(rq, rk, rv, rg, gn, B, L):
    C = RET_CHUNK
    nC = L // C
    dmat, xi, zeta, g_c = _ret_consts(C)
    blk = lambda: pl.BlockSpec((C, D_RET), lambda b, c: (b * nC + c, 0))
    return pl.pallas_call(
        _ret_kernel,
        grid=(B, nC),
        in_specs=[pl.BlockSpec(memory_space=pltpu.SMEM), blk(), blk(), blk(), blk(), _full((1, D_RET)),
                  _full((H_RET, C, C)), _full((C, H_RET)), _full((H_RET, C))],
        out_specs=[blk(), pl.BlockSpec((1, H_RET, HD_RET, HD_RET), lambda b, c: (b, 0, 0, 0))],
        out_shape=[jax.ShapeDtypeStruct((B * L, D_RET), BF16),
                   jax.ShapeDtypeStruct((B, H_RET, HD_RET, HD_RET), F32)],
        scratch_shapes=[pltpu.VMEM((H_RET, HD_RET, HD_RET), F32)],
        compiler_params=_cparams(("parallel", "arbitrary")),
        name="retention",
    )(g_c, rq, rk, rv, rg, gn.reshape(1, D_RET), dmat, xi, zeta)


def _ret1_kernel(q_ref, k_ref, v_ref, s_ref, gam_ref, rg_ref, gn_ref, y_ref, so_ref):
    q, k, v, s, gam = q_ref[...], k_ref[...], v_ref[...], s_ref[...], gam_ref[...]
    qk = jnp.sum(q * k, axis=1, keepdims=True)
    cross = jnp.sum(q * s, axis=1, keepdims=True) * gam
    o = qk * v + cross
    so_ref[...] = s * gam + k * v
    mu = jnp.mean(o, axis=-1, keepdims=True)
    d = o - mu
    var = jnp.mean(d * d, axis=-1, keepdims=True)
    rg = rg_ref[...]
    silu = rg * (1.0 / (1.0 + jnp.exp(-rg)))
    y_ref[...] = silu * (d * lax.rsqrt(var + EPS) * gn_ref[...])


def _ret1_call(rq, rk, rv, rg, gn, state):
    DB = rq.shape[0]
    n = DB * H_RET
    _, _, _, g_c = _ret_consts(1)
    col = lambda t: t.astype(F32).reshape(n, HD_RET, 1)
    rowv = lambda t: t.astype(F32).reshape(n, 1, HD_RET)
    gam = jnp.tile(g_c, DB).reshape(n, 1, 1)
    gn3 = jnp.tile(gn.reshape(H_RET, 1, HD_RET), (DB, 1, 1))
    tb = 128
    b3 = lambda a, b: pl.BlockSpec((tb, a, b), lambda i: (i, 0, 0))
    y, s_new = pl.pallas_call(
        _ret1_kernel,
        grid=(n // tb,),
        in_specs=[b3(HD_RET, 1), b3(HD_RET, 1), b3(1, HD_RET), b3(HD_RET, HD_RET), b3(1, 1), b3(1, HD_RET),
                  b3(1, HD_RET)],
        out_specs=[b3(1, HD_RET), b3(HD_RET, HD_RET)],
        out_shape=[jax.ShapeDtypeStruct((n, 1, HD_RET), F32),
                   jax.ShapeDtypeStruct((n, HD_RET, HD_RET), F32)],
        compiler_params=_cparams(("parallel",)),
        name="retention_step",
    )(col(rq), col(rk), rowv(rv), state.astype(F32).reshape(n, HD_RET, HD_RET), gam, rowv(rg), gn3)
    return y.reshape(DB, D_RET).astype(BF16), s_new.reshape(DB, H_RET, HD_RET, HD_RET)


NSA_TQ = 128
NSA_TK = 512


def _prep_cmp_w(cmp_w):
    z = jnp.zeros_like(cmp_w)
    top = jnp.concatenate([cmp_w, z], axis=-1)
    bot = jnp.concatenate([z, cmp_w], axis=-1)
    return jnp.concatenate([top, bot], axis=-2).astype(BF16)


def _iota(shape, dim):
    return lax.broadcasted_iota(jnp.int32, shape, dim)


def _group_queries(nqf, g, tq):
    lane_g = _iota((tq, LANES), 1) // HD_NSA
    parts = []
    for r in range(R_NSA):
        h = g * R_NSA + r
        x = nqf[:, (h // 2) * LANES:(h // 2 + 1) * LANES]
        if h % 2 != g:
            x = pltpu.roll(x, HD_NSA, axis=1)
        parts.append((jnp.where(lane_g == g, x, 0.0) * (HD_NSA ** -0.5)).astype(BF16))
    return parts


def _select_blocks(score, nb, n_sel):
    s_t = score.T[:nb, :]
    n_i = _iota(s_t.shape, 0)
    rank = jnp.zeros(s_t.shape, F32)
    for m in range(nb):
        row = s_t[m:m + 1, :]
        ahead = (row > s_t) | ((row == s_t) & (n_i > m))
        rank = rank + jnp.where(ahead, 1.0, 0.0)
    sel_t = jnp.where((rank < n_sel) & (s_t >= 0.0), 1.0, 0.0)
    sel_t = jnp.concatenate([sel_t, jnp.zeros((LANES - nb, s_t.shape[1]), F32)], axis=0)
    return sel_t.T


def _flash_step(q, kt, vt, bias, carry):
    m, l, acc = carry
    s = _dot_nt(q, kt) + bias
    m_new = jnp.maximum(m, jnp.max(s, axis=-1, keepdims=True))
    alpha = jnp.exp(m - m_new)
    p = jnp.exp(s - m_new)
    l = alpha * l + jnp.sum(p, axis=-1, keepdims=True)
    return m_new, l, alpha * acc + _dot(p.astype(BF16), vt)


def _flash_run(qs, k_ref, v_ref, bias_ref, n_steps, tq, tk):
    n = len(qs)
    q_all = jnp.concatenate(qs, axis=0)

    def body(j, carry):
        m, l, acc = carry
        rows = pl.ds(pl.multiple_of(j * tk, tk), tk)
        s = _dot_nt(q_all, k_ref[rows, :]).reshape(n, tq, tk) + bias_ref[j][None]
        m_new = jnp.maximum(m, jnp.max(s, axis=-1, keepdims=True))
        alpha = jnp.exp(m - m_new)
        p = jnp.exp(s - m_new)
        l = alpha * l + jnp.sum(p, axis=-1, keepdims=True)
        pv = _dot(p.reshape(n * tq, tk).astype(BF16), v_ref[rows, :]).reshape(n, tq, LANES)
        return m_new, l, alpha * acc + pv

    init = (jnp.full((n, tq, 1), NEG, F32), jnp.zeros((n, tq, 1), F32), jnp.zeros((n, tq, LANES), F32))
    _, l, acc = lax.fori_loop(0, n_steps, body, init)
    out = acc / jnp.maximum(l, 1e-30)
    return [out[r] for r in range(n)]


def _window_bias(tq):
    n = WINDOW // tq + 1
    dist = (jnp.arange(tq)[None, :, None] - jnp.arange(WINDOW + tq)[None, None, :]
            + jnp.arange(n)[:, None, None] * tq)
    return jnp.where((dist >= 0) & (dist < WINDOW), 0.0, NEG).astype(F32)


def _nsa_kernel(nq_ref, ckf_ref, cvf_ref, sk_ref, sv_ref, wk_ref, wv_ref, gt_ref, cw_ref, cb_ref, wb_ref,
                o_ref, ck_scr, cv_scr, bias_scr, *, nb):
    qi = pl.program_id(1)
    tq, tk = NSA_TQ, NSA_TK
    n_sel = min(N_SEL, nb)

    @pl.when(qi == 0)
    def _compress():
        for which, (src, dst) in enumerate(((ckf_ref, ck_scr), (cvf_ref, cv_scr))):
            def body(j, acc):
                x = src[pl.ds(j, nb, stride=CMP_BLK), :].astype(BF16)
                return acc + _dot(x, cw_ref[which, j])
            acc = lax.fori_loop(0, CMP_BLK, body, jnp.zeros((nb, LANES), F32))
            dst[...] = jnp.zeros_like(dst)
            dst[0:nb, :] = (acc + cb_ref[which]).astype(BF16)

    t0 = qi * tq
    nqf = nq_ref[...].astype(F32)
    gt = gt_ref[...]
    pos = t0 + _iota((tq, 1), 0)
    blk = _iota((1, LANES), 1)
    vis = ((blk * CMP_BLK + CMP_BLK - 1) <= pos) & (blk < nb)
    forced = ((blk == 0) | (blk == pos // CMP_BLK)) & (blk < nb)
    vis_bias = jnp.where(vis, 0.0, NEG)
    lane = _iota((tq, LANES), 1)
    kcol = _iota((1, tk), 1)
    blk_row = _iota((LANES, tk), 0)
    blk_of_key = _iota((LANES, tk), 1) // CMP_BLK

    n_kv = (t0 + tq + tk - 1) // tk

    for g in range(G_NSA):
        qp = _group_queries(nqf, g, tq)

        pcs, o_cmp = [], []
        for r in range(R_NSA):
            sc = _dot_nt(qp[r], ck_scr[...]) + vis_bias
            pc = jnp.where(sc > 0.5 * NEG, jnp.exp(sc - jnp.max(sc, axis=-1, keepdims=True)), 0.0)
            pc = pc / jnp.maximum(jnp.sum(pc, axis=-1, keepdims=True), 1e-30)
            pcs.append(pc)
            o_cmp.append(_dot(pc.astype(BF16), cv_scr[...]))
        imp = pcs[0] + pcs[1] + pcs[2] + pcs[3]

        score = jnp.where(forced, SEL_FORCE, jnp.where(vis, imp, -1.0))
        score = jnp.where(blk < nb, score, -2.0)
        sel = _select_blocks(score, nb, n_sel).astype(BF16)

        def mask_tile(j, _):
            expand = jnp.where(blk_row == blk_of_key + j * (tk // CMP_BLK), 1.0, 0.0).astype(BF16)
            chosen = _dot(sel, expand)
            bias_scr[j] = jnp.where((chosen > 0.5) & (j * tk + kcol <= pos), 0.0, NEG)
            return 0

        lax.fori_loop(0, n_kv, mask_tile, 0)
        o_slc = _flash_run(qp, sk_ref, sv_ref, bias_scr, n_kv, tq, tk)

        slab = pl.ds(pl.multiple_of(jnp.maximum(t0 - WINDOW, 0), tq), WINDOW + tq)
        wk, wv = wk_ref[slab, :], wv_ref[slab, :]
        wbias = wb_ref[jnp.minimum(qi, WINDOW // tq)]
        sw = _dot_nt(jnp.concatenate(qp, axis=0), wk).reshape(R_NSA, tq, WINDOW + tq) + wbias[None]
        pw = jnp.exp(sw - jnp.max(sw, axis=-1, keepdims=True))
        o_win = (_dot(pw.reshape(R_NSA * tq, WINDOW + tq).astype(BF16), wv).reshape(R_NSA, tq, LANES)
                 / jnp.maximum(jnp.sum(pw, axis=-1, keepdims=True), 1e-30))

        mixed = []
        for r in range(R_NSA):
            h = g * R_NSA + r
            o_h = (gt[:, 3 * h:3 * h + 1] * o_cmp[r] + gt[:, 3 * h + 1:3 * h + 2] * o_slc[r]
                   + gt[:, 3 * h + 2:3 * h + 3] * o_win[r])
            if h % 2 != g:
                o_h = pltpu.roll(o_h, HD_NSA, axis=1)
            mixed.append(o_h)
        for c in range(R_NSA // 2):
            pair = jnp.where(lane < HD_NSA, mixed[2 * c], mixed[2 * c + 1])
            col = (g * (R_NSA // 2) + c) * LANES
            o_ref[:, col:col + LANES] = pair.astype(BF16)


def _nsa_call(nq, kv, kvb, gates, cw_bd, cb2, B, L):
    tq = NSA_TQ
    nQ = L // tq
    nb = L // CMP_BLK
    assert nb <= LANES and L % NSA_TK == 0 and NSA_TK % tq == 0 and WINDOW % tq == 0 and L >= WINDOW + tq
    win_bias = _window_bias(tq)
    rows = lambda w: pl.BlockSpec((tq, w), lambda b, q: (b * nQ + q, 0))
    seq = lambda c: pl.BlockSpec((L, LANES), lambda b, q: (b, c))
    return pl.pallas_call(
        functools.partial(_nsa_kernel, nb=nb),
        grid=(B, nQ),
        in_specs=[rows(D_NSA), seq(0), seq(1), seq(2), seq(3), seq(4), seq(5), rows(LANES),
                  _full((2, CMP_BLK, LANES, LANES)), _full((2, 1, LANES)), _full(win_bias.shape)],
        out_specs=rows(D_NSA),
        out_shape=jax.ShapeDtypeStruct((B * L, D_NSA), BF16),
        scratch_shapes=[pltpu.VMEM((LANES, LANES), BF16), pltpu.VMEM((LANES, LANES), BF16),
                        pltpu.VMEM((L // NSA_TK, tq, NSA_TK), F32)],
        compiler_params=_cparams(("parallel", "arbitrary")),
        name="nsa_prompt",
    )(nq, kv, kv, kvb, kvb, kvb, kvb, gates, cw_bd, cb2, win_bias)


def _normmm_kernel(x_ref, g_ref, w_ref, o_ref, ob_ref):
    y = _dot(_rms(x_ref[...], g_ref[...]).astype(BF16), w_ref[...])
    o_ref[...] = y
    ob_ref[...] = y.astype(BF16)


def _normmm_call(x2d, g, w, tm):
    n, d = x2d.shape
    m = w.shape[1]
    return pl.pallas_call(
        _normmm_kernel,
        grid=(n // tm,),
        in_specs=[pl.BlockSpec((tm, d), lambda i: (i, 0)), _full((1, d)), _full((d, m))],
        out_specs=[pl.BlockSpec((tm, m), lambda i: (i, 0))] * 2,
        out_shape=[jax.ShapeDtypeStruct((n, m), F32), jax.ShapeDtypeStruct((n, m), BF16)],
        compiler_params=_cparams(("parallel",)),
        name="norm_matmul",
    )(x2d, g.reshape(1, d), w)


def _mixout_kernel(yr_ref, on_ref, h_ref, wo_ref, g_ref, wq_ref, h1_ref, mq_ref):
    h1 = h_ref[...] + _dot(yr_ref[...], wo_ref[0:D_RET, :]) + _dot(on_ref[...], wo_ref[D_RET:, :])
    h1_ref[...] = h1
    mq_ref[...] = _dot(_rms(h1, g_ref[...]).astype(BF16), wq_ref[...]).astype(BF16)


def _mixout_call(yret, onsa, h, w_out, g_mem, w_mq, tm):
    n = h.shape[0]
    row = lambda w: pl.BlockSpec((tm, w), lambda i: (i, 0))
    return pl.pallas_call(
        _mixout_kernel,
        grid=(n // tm,),
        in_specs=[row(D_RET), row(D_NSA), row(D_MODEL), _full((D_RET + D_NSA, D_MODEL)), _full((1, D_MODEL)),
                  _full((D_MODEL, D_MEM))],
        out_specs=[row(D_MODEL), row(D_MEM)],
        out_shape=[jax.ShapeDtypeStruct((n, D_MODEL), F32), jax.ShapeDtypeStruct((n, D_MEM), BF16)],
        compiler_params=_cparams(("parallel",)),
        name="mixer_out",
    )(yret, onsa, h, w_out, g_mem.reshape(1, D_MODEL), w_mq)


def _memattn_kernel(q_ref, k_ref, v_ref, o_ref):
    q, k, v = q_ref[...], k_ref[...], v_ref[...]
    for h in range(H_MEM):
        sl = slice(h * HD_MEM, (h + 1) * HD_MEM)
        s = _dot_nt(q[:, sl], k[:, sl]) * (HD_MEM ** -0.5)
        p = jnp.exp(s - jnp.max(s, axis=-1, keepdims=True))
        p = p / jnp.sum(p, axis=-1, keepdims=True)
        o_ref[:, sl] = _dot(p.astype(BF16), v[:, sl]).astype(BF16)


def _memattn_call(mq, mkvb, B, L, n_mem, tm):
    nT = L // tm
    return pl.pallas_call(
        _memattn_kernel,
        grid=(B, nT),
        in_specs=[pl.BlockSpec((tm, D_MEM), lambda b, i: (b * nT + i, 0)),
                  pl.BlockSpec((n_mem, D_MEM), lambda b, i: (b, 0)),
                  pl.BlockSpec((n_mem, D_MEM), lambda b, i: (b, 1))],
        out_specs=pl.BlockSpec((tm, D_MEM), lambda b, i: (b * nT + i, 0)),
        out_shape=jax.ShapeDtypeStruct((B * L, D_MEM), BF16),
        compiler_params=_cparams(("parallel", "parallel")),
        name="mem_attention",
    )(mq, mkvb, mkvb)


def _memattn1_kernel(q_ref, k_ref, v_ref, o_ref):
    tb = q_ref.shape[0]
    for b in range(tb):
        q = q_ref[b]
        prod = k_ref[b] * q
        outs = []
        for h in range(H_MEM):
            sl = slice(h * HD_MEM, (h + 1) * HD_MEM)
            s = jnp.sum(prod[:, sl], axis=-1, keepdims=True) * (HD_MEM ** -0.5)
            p = jnp.exp(s - jnp.max(s, axis=0, keepdims=True))
            p = p / jnp.sum(p, axis=0, keepdims=True)
            outs.append(jnp.sum(p * v_ref[b][:, sl], axis=0, keepdims=True))
        o_ref[b] = jnp.concatenate(outs, axis=-1)


def _memattn1_call(mq, cache_k, cache_v):
    DB, n_mem = cache_k.shape[0], cache_k.shape[1]
    tb = 8
    blk = pl.BlockSpec((tb, n_mem, D_MEM), lambda i: (i, 0, 0))
    q3 = pl.BlockSpec((tb, 1, D_MEM), lambda i: (i, 0, 0))
    o = pl.pallas_call(
        _memattn1_kernel,
        grid=(DB // tb,),
        in_specs=[q3, blk, blk],
        out_specs=q3,
        out_shape=jax.ShapeDtypeStruct((DB, 1, D_MEM), F32),
        compiler_params=_cparams(("parallel",)),
        name="mem_attention_step",
    )(mq.astype(F32).reshape(DB, 1, D_MEM), cache_k.reshape(DB, n_mem, D_MEM), cache_v.reshape(DB, n_mem, D_MEM))
    return o.reshape(DB, D_MEM).astype(BF16)


def _prep_peer_keys(subkeys):
    half = PEER_DKEY // 2
    z = jnp.zeros_like(subkeys[:, 0])
    k0 = jnp.concatenate([subkeys[:, 0], z], axis=-1)
    k1 = jnp.concatenate([z, subkeys[:, 1]], axis=-1)
    return jnp.concatenate([k0, k1], axis=1).astype(BF16)


def _top_rows(x, k, payload=None):
    n = x.shape[-2]
    ri = _iota(x.shape, x.ndim - 2)
    vals, picks = [], []
    for _ in range(k):
        m = jnp.max(x, axis=-2, keepdims=True)
        i = jnp.min(jnp.where(x == m, ri, n), axis=-2, keepdims=True)
        hit = ri == i
        vals.append(m)
        picks.append(i if payload is None else jnp.max(jnp.where(hit, payload, -1), axis=-2, keepdims=True))
        x = jnp.where(hit, -jnp.inf, x)
    return jnp.concatenate(vals, axis=-2), jnp.concatenate(picks, axis=-2)


def _route_kernel(om_ref, h1_ref, wo_ref, g_ref, wq_ref, sk_ref, h2_ref, xn_ref, idx_ref, gw_ref,
                  idx_scr, gw_scr):
    tm = h1_ref.shape[0]
    h2 = h1_ref[...] + _dot(om_ref[...], wo_ref[...])
    h2_ref[...] = h2
    xn = _rms(h2, g_ref[...])
    xn_ref[...] = xn
    xb = xn.astype(BF16)

    k = PEER_TOPK
    n_b = [k // (a + 1) for a in range(k)]
    pad = -sum(n_b) % 8
    lanes = min(tm, LANES)

    def head(h, _):
        pq = _dot(xb, wq_ref[h]).astype(BF16)
        s_all = _dot_nt(sk_ref[h], pq)
        rows = pl.ds(pl.multiple_of(h * k, k), k)
        for c in range(tm // lanes):
            s = s_all[:, c * lanes:(c + 1) * lanes].reshape(2, PEER_KEYS, lanes)
            v12, i12 = _top_rows(s, k)
            cand = jnp.concatenate([v12[0][a:a + 1] + v12[1][0:n_b[a]] for a in range(k)]
                                   + [jnp.full((pad, lanes), -jnp.inf, F32)], axis=0)
            cidx = jnp.concatenate([i12[0][a:a + 1] * PEER_KEYS + i12[1][0:n_b[a]] for a in range(k)]
                                   + [jnp.full((pad, lanes), -1, jnp.int32)], axis=0)
            top, expert = _top_rows(cand, k, payload=cidx)
            e = jnp.exp(top - top[0:1, :])
            idx_scr[rows, c * lanes:(c + 1) * lanes] = expert * PEER_ROW
            gw_scr[rows, c * lanes:(c + 1) * lanes] = e / jnp.sum(e, axis=0, keepdims=True)
        return 0

    lax.fori_loop(0, PEER_HEADS, head, 0)
    idx_ref[...] = idx_scr[...].T
    gw_ref[...] = gw_scr[...].T


def _route_call(omem, h1, w_mo, g_ffn, wq_h, sk_pad, tm):
    n = h1.shape[0]
    nk = PEER_HEADS * PEER_TOPK
    row = lambda w: pl.BlockSpec((tm, w), lambda i: (i, 0))
    return pl.pallas_call(
        _route_kernel,
        grid=(n // tm,),
        in_specs=[row(D_MEM), row(D_MODEL), _full((D_MEM, D_MODEL)), _full((1, D_MODEL)),
                  _full((PEER_HEADS, D_MODEL, PEER_DKEY)), _full((PEER_HEADS, 2 * PEER_KEYS, PEER_DKEY))],
        out_specs=[row(D_MODEL), row(D_MODEL), row(nk), row(nk)],
        out_shape=[jax.ShapeDtypeStruct((n, D_MODEL), F32), jax.ShapeDtypeStruct((n, D_MODEL), F32),
                   jax.ShapeDtypeStruct((n, nk), jnp.int32), jax.ShapeDtypeStruct((n, nk), F32)],
        scratch_shapes=[pltpu.VMEM((nk, tm), jnp.int32), pltpu.VMEM((nk, tm), F32)],
        compiler_params=_cparams(("parallel",)),
        name="peer_route",
    )(omem, h1, w_mo, g_ffn.reshape(1, D_MODEL), wq_h, sk_pad)


def _gelu_tanh(x):
    return 0.5 * x * (1.0 + jnp.tanh(0.7978845608028654 * (x + 0.044715 * x * x * x)))


PEER_TT = 64


def _pack_table(t):
    e, d = t.shape
    b = lax.bitcast_convert_type(t.astype(BF16), jnp.uint16).astype(jnp.uint32)
    w = b[:, :d // 2] | (b[:, d // 2:] << 16)
    return w.reshape(e * PEER_ROW, LANES)


def _expert_row(tab_ref, off):
    w = tab_ref[pl.ds(pl.multiple_of(off, PEER_ROW), PEER_ROW), :]
    lo = pltpu.bitcast(w << 16, F32)
    hi = pltpu.bitcast(w & jnp.uint32(0xFFFF0000), F32)
    return lo, hi


def _peer_act_kernel(idx_ref, x_ref, gw_ref, tab_ref, c_ref, part_scr, act_scr):
    tt, nk = gw_ref.shape
    sub = _iota((8, LANES), 0)
    keep_pairs = (sub % 4) < 2
    keep_even = (sub % 2) == 0
    feed = (0, 4, 2, 6, 1, 5, 3, 7)

    def row_sums8(p):
        p = [p[i] for i in feed]
        v = [jnp.concatenate([p[2 * i], p[2 * i + 1]], axis=0) for i in range(4)]
        w = [x + pltpu.roll(x, 6, axis=0) for x in v]
        u = [jnp.where(keep_pairs, w[2 * i], pltpu.roll(w[2 * i + 1], 2, axis=0)) for i in range(2)]
        z = [x + pltpu.roll(x, 7, axis=0) for x in u]
        return jnp.where(keep_even, z[0], pltpu.roll(z[1], 1, axis=0))

    def token(t, _):
        xt = x_ref[t]
        xlo, xhi = xt[0:PEER_ROW], xt[PEER_ROW:]

        for j0 in range(0, nk, 8):
            prods = []
            for j in range(j0, j0 + 8):
                lo, hi = _expert_row(tab_ref, idx_ref[t, j])
                prods.append(lo * xlo + hi * xhi)
            part_scr[t, j0:j0 + 8, :] = row_sums8(prods)
        return 0

    lax.fori_loop(0, tt, token, 0)

    def finish(t, _):
        act_scr[pl.ds(t, 1), :] = jnp.sum(part_scr[t].T, axis=0, keepdims=True)
        return 0

    lax.fori_loop(0, tt, finish, 0, unroll=4)
    c_ref[...] = gw_ref[...] * _gelu_tanh(act_scr[...])


def _peer_out_kernel(idx_ref, c_ref, h2_ref, gf_ref, tab_ref, y_ref, splat_scr):
    tt, nk = idx_ref.shape
    n_acc = 8

    def token(t, _):
        splat_scr[...] = jnp.broadcast_to(c_ref[pl.ds(t, 1), :], (nk, nk)).T
        acc_lo = [jnp.zeros((PEER_ROW, LANES), F32)] * n_acc
        acc_hi = [jnp.zeros((PEER_ROW, LANES), F32)] * n_acc
        for j in range(nk):
            lo, hi = _expert_row(tab_ref, idx_ref[t, j])
            c = splat_scr[j:j + 1, :]
            acc_lo[j % n_acc] = acc_lo[j % n_acc] + c * lo
            acc_hi[j % n_acc] = acc_hi[j % n_acc] + c * hi
        tree = lambda v: v[0] if len(v) == 1 else tree([a + b for a, b in zip(v[0::2], v[1::2])])
        y_ref[t] = h2_ref[t] + jnp.concatenate([tree(acc_lo), tree(acc_hi)], axis=0)
        return 0

    lax.fori_loop(0, tt, token, 0)
    h3 = y_ref[...]
    ms = jnp.sum(jnp.sum(h3 * h3, axis=2, keepdims=True), axis=1, keepdims=True) * (1.0 / D_MODEL)
    y_ref[...] = h3 * lax.rsqrt(ms + EPS) * gf_ref[...]


def _peer_call(idx, xn, gw, h2, g_final, u_tab, v_tab, tt):
    n, nk = idx.shape
    assert n % tt == 0 and nk % 8 == 0
    smem = lambda: pl.BlockSpec((tt, nk), lambda i: (i, 0), memory_space=pltpu.SMEM)
    tile = lambda: pl.BlockSpec((tt, 8, LANES), lambda i: (i, 0, 0))
    table = lambda t: pl.BlockSpec(t.shape, lambda i: (0, 0), pipeline_mode=pl.Buffered(1))
    as_tiles = lambda a: a.reshape(n, 8, LANES)
    c = pl.pallas_call(
        _peer_act_kernel,
        grid=(n // tt,),
        in_specs=[smem(), tile(), pl.BlockSpec((tt, nk), lambda i: (i, 0)), table(u_tab)],
        out_specs=pl.BlockSpec((tt, nk), lambda i: (i, 0)),
        out_shape=jax.ShapeDtypeStruct((n, nk), F32),
        scratch_shapes=[pltpu.VMEM((tt, nk, LANES), F32), pltpu.VMEM((tt, nk), F32)],
        compiler_params=_cparams(("arbitrary",)),
        name="peer_act",
    )(idx, as_tiles(xn), gw, u_tab)
    y = pl.pallas_call(
        _peer_out_kernel,
        grid=(n // tt,),
        in_specs=[smem(), pl.BlockSpec((tt, nk), lambda i: (i, 0)), tile(), _full((8, LANES)), table(v_tab)],
        out_specs=tile(),
        out_shape=jax.ShapeDtypeStruct((n, 8, LANES), F32),
        scratch_shapes=[pltpu.VMEM((nk, nk), F32)],
        compiler_params=_cparams(("arbitrary",)),
        name="peer_out",
    )(idx, c, as_tiles(h2), g_final.reshape(8, LANES), v_tab)
    return y.reshape(n, D_MODEL)


CMP_PAGES = 128


def _cmp_pages_kernel(x_ref, w_ref, b_ref, o_ref):
    n_rows = o_ref.shape[0]

    def body(d, acc):
        x = x_ref[pl.ds(d, n_rows, stride=HD_NSA), :].astype(BF16)
        return acc + _dot(x, w_ref[d])

    acc = lax.fori_loop(0, HD_NSA, body, jnp.zeros((n_rows, LANES), F32), unroll=2)
    o_ref[...] = acc + b_ref[...]


def _cmp_pages_call(pool_t, w_bd, b):
    n_phys = pool_t.shape[0]
    rows = n_phys * G_NSA
    step = CMP_PAGES * G_NSA
    assert rows % step == 0
    return pl.pallas_call(
        _cmp_pages_kernel,
        grid=(rows // step,),
        in_specs=[pl.BlockSpec((step * HD_NSA, PAGE_SIZE), lambda i: (i, 0)),
                  _full((HD_NSA, PAGE_SIZE, LANES)), _full((1, LANES))],
        out_specs=pl.BlockSpec((step, LANES), lambda i: (i, 0)),
        out_shape=jax.ShapeDtypeStruct((rows, LANES), F32),
        compiler_params=_cparams(("parallel",)),
        name="compress_pages",
    )(pool_t.reshape(rows * HD_NSA, PAGE_SIZE), w_bd, jnp.tile(b, 2).reshape(1, LANES))


def _nsa1_cmp_kernel(pt_ref, nq_ref, ckn_ref, cvn_ref, tk_ref, tv_ref, cw_ref, cb_ref, ocmp_ref, idx_ref,
                     kg_scr, vg_scr, *, n_pages, q_pos):
    tb = nq_ref.shape[0]
    nb_past = n_pages * (PAGE_SIZE // CMP_BLK)
    base = pl.program_id(0) * tb
    lane = _iota((1, LANES), 1)
    blk_n = 2 * (lane % HD_NSA) + lane // HD_NSA
    forced = (blk_n == 0) | (blk_n == q_pos // CMP_BLK)
    vis = (blk_n * CMP_BLK + CMP_BLK - 1) <= q_pos
    new_vis = (nb_past * CMP_BLK + CMP_BLK - 1) <= q_pos
    new_forced = nb_past == q_pos // CMP_BLK
    n_row = jnp.broadcast_to(blk_n, (LANES, LANES))
    n_col = 2 * (_iota((LANES, LANES), 0) % HD_NSA) + _iota((LANES, LANES), 0) // HD_NSA
    lane8 = _iota((1, LANES), 1) // HD_NSA

    def sample(b, _):
        qrow = nq_ref[pl.ds(b, 1), :].astype(F32)
        new_k = _dot(ckn_ref[pl.ds(b, 1), :].astype(BF16), cw_ref[0]) + cb_ref[0]
        new_v = _dot(cvn_ref[pl.ds(b, 1), :].astype(BF16), cw_ref[1]) + cb_ref[1]
        o_row, idx_row = [], jnp.full((1, LANES), -1, jnp.int32)
        for g in range(G_NSA):
            def gather(i, _):
                r = pt_ref[base + b, i] * G_NSA + g
                kg_scr[pl.ds(i, 1), :] = tk_ref[pl.ds(r, 1), :]
                vg_scr[pl.ds(i, 1), :] = tv_ref[pl.ds(r, 1), :]
                return 0
            lax.fori_loop(0, n_pages, gather, 0)
            rows = []
            for c in range(2):
                for r in range(R_NSA):
                    h = g * R_NSA + r
                    x = qrow[:, (h // 2) * LANES:(h // 2 + 1) * LANES]
                    if h % 2 != c:
                        x = pltpu.roll(x, HD_NSA, axis=1)
                    rows.append(jnp.where(lane8 == c, x, 0.0))
            qpad = jnp.concatenate(rows, axis=0)
            s = _dot_nt(qpad.astype(BF16), kg_scr[...].astype(BF16)) * (HD_NSA ** -0.5)
            s3 = s.reshape(2, R_NSA, n_pages)
            qg = qpad[R_NSA * g:R_NSA * (g + 1), :]
            nk_g = jnp.where(lane8 == g, new_k.astype(BF16).astype(F32), 0.0)
            s_new = jnp.sum(qg.astype(BF16).astype(F32) * nk_g, axis=-1, keepdims=True) * (HD_NSA ** -0.5)
            s_new = jnp.where(new_vis, s_new, NEG)[None]
            m = jnp.maximum(jnp.max(jnp.max(s3, axis=2, keepdims=True), axis=0, keepdims=True), s_new)
            p = jnp.exp(s3 - m)
            p_new = jnp.where(s_new > 0.5 * NEG, jnp.exp(s_new - m), 0.0)
            l = jnp.sum(jnp.sum(p, axis=2, keepdims=True), axis=0, keepdims=True) + p_new
            inv = 1.0 / jnp.maximum(l, 1e-30)
            p = p * inv
            p_new = p_new * inv
            res = _dot(p.reshape(2 * R_NSA, n_pages).astype(BF16), vg_scr[...].astype(BF16))
            o4 = res[0:R_NSA] + pltpu.roll(res[R_NSA:], HD_NSA, axis=1)
            nv_g = new_v.astype(BF16).astype(F32)
            if g == 1:
                nv_g = pltpu.roll(nv_g, HD_NSA, axis=1)
            o4 = o4 + p_new[0].astype(BF16).astype(F32) * nv_g
            o_row += [o4[r:r + 1, 0:HD_NSA] for r in range(R_NSA)]
            imp2 = jnp.sum(p, axis=1)
            imp = jnp.concatenate([imp2[0:1], imp2[1:2]], axis=-1)
            score = jnp.where(forced, SEL_FORCE, jnp.where(vis, imp, -1.0))
            imp_new = jnp.sum(p_new)
            sc_new = SEL_FORCE if new_forced else jnp.where(new_vis, imp_new, -1.0)
            a = jnp.broadcast_to(score, (LANES, LANES))
            bt = a.T
            ahead = (bt > a) | ((bt == a) & (n_col < n_row))
            rank = jnp.sum(jnp.where(ahead, 1.0, 0.0), axis=0, keepdims=True) + jnp.where(sc_new > score, 1.0, 0.0)
            rank_new = jnp.sum(jnp.where(score >= sc_new, 1.0, 0.0))
            for r in range(N_SEL):
                hit = (rank == r) & (score >= 0.0)
                val = jnp.sum(jnp.where(hit, blk_n + 1, 0)) - 1
                val = jnp.where((rank_new == r) & (sc_new >= 0.0), nb_past, val)
                idx_row = jnp.where(lane == g * N_SEL + r, val, idx_row)
        ocmp_ref[pl.ds(b, 1), :] = jnp.concatenate(o_row, axis=-1)
        idx_ref[pl.ds(b, 1), :] = idx_row
        return 0

    lax.fori_loop(0, tb, sample, 0)


def _nsa1_cmp_call(page_table, nq, ck_new, cv_new, tbl_k, tbl_v, cw_bd, cb2, q_pos):
    DB, n_pages = page_table.shape
    assert n_pages * (PAGE_SIZE // CMP_BLK) == LANES
    tb = 8
    row = lambda w: pl.BlockSpec((tb, w), lambda i, pt: (i, 0))
    whole = lambda a: pl.BlockSpec(a.shape, lambda i, pt: (0,) * a.ndim, pipeline_mode=pl.Buffered(1))
    return pl.pallas_call(
        functools.partial(_nsa1_cmp_kernel, n_pages=n_pages, q_pos=q_pos),
        grid_spec=pltpu.PrefetchScalarGridSpec(
            num_scalar_prefetch=1,
            grid=(DB // tb,),
            in_specs=[row(D_NSA), row(LANES), row(LANES), whole(tbl_k), whole(tbl_v),
                      pl.BlockSpec((2, LANES, LANES), lambda i, pt: (0, 0, 0)),
                      pl.BlockSpec((2, 1, LANES), lambda i, pt: (0, 0, 0))],
            out_specs=[row(D_NSA), row(LANES)],
            scratch_shapes=[pltpu.VMEM((n_pages, LANES), F32), pltpu.VMEM((n_pages, LANES), F32)]),
        out_shape=[jax.ShapeDtypeStruct((DB, D_NSA), F32), jax.ShapeDtypeStruct((DB, LANES), jnp.int32)],
        compiler_params=_cparams(("arbitrary",)),
        name="nsa_step_compressed",
    )(page_table, nq.astype(F32), ck_new, cv_new, tbl_k, tbl_v, cw_bd[:, 0], cb2)


def _nsa1_attn_kernel(idx_ref, pt_ref, nq_ref, kvn_ref, gt_ref, ocmp_ref, wk_ref, wv_ref, sk_hbm, sv_hbm,
                      o_ref, kbuf, vbuf, sem, *, n_pages, q_pos, past_len):
    b = pl.program_id(0)
    bpp = PAGE_SIZE // CMP_BLK
    nb_past = n_pages * bpp
    wb = wk_ref.shape[3]

    def block_copies(bb, slot, g, r):
        n = idx_ref[bb, g * N_SEL + r]
        past = (n >= 0) & (n < nb_past)
        page = pt_ref[bb, jnp.clip(n, 0, nb_past - 1) // bpp]
        ck = pltpu.make_async_copy(sk_hbm.at[page, g], kbuf.at[slot, g, r], sem.at[slot, 0])
        cv = pltpu.make_async_copy(sv_hbm.at[page, g], vbuf.at[slot, g, r], sem.at[slot, 1])
        return past, ck, cv

    def issue(bb, slot):
        for g in range(G_NSA):
            for r in range(N_SEL):
                past, ck, cv = block_copies(bb, slot, g, r)

                @pl.when(past)
                def _():
                    ck.start()
                    cv.start()

                @pl.when(jnp.logical_not(past))
                def _():
                    kbuf[slot, g, r] = jnp.zeros((HD_NSA, PAGE_SIZE), F32)
                    vbuf[slot, g, r] = jnp.zeros((HD_NSA, PAGE_SIZE), F32)

    @pl.when(b == 0)
    def _():
        issue(0, 0)

    @pl.when(b + 1 < pl.num_programs(0))
    def _():
        issue(b + 1, (b + 1) % 2)

    slot = b % 2
    for g in range(G_NSA):
        for r in range(N_SEL):
            past, ck, cv = block_copies(b, slot, g, r)

            @pl.when(past)
            def _():
                ck.wait()
                cv.wait()

    qrow = nq_ref[0].astype(F32)
    kvn = kvn_ref[0]
    gt = gt_ref[0]
    ocmp = ocmp_ref[0]
    scale = HD_NSA ** -0.5
    blk_in_page = _iota((1, PAGE_SIZE), 1) // CMP_BLK
    wpos = past_len - wb + _iota((1, wb), 1)
    wdist = q_pos - wpos
    w_ok = (wdist >= 0) & (wdist < WINDOW) & (wpos >= 0)
    bfr = lambda t: t.astype(BF16).astype(F32)
    pieces = []
    for g in range(G_NSA):
        q4 = jnp.concatenate([qrow[:, (g * R_NSA + r) * HD_NSA:(g * R_NSA + r + 1) * HD_NSA]
                              for r in range(R_NSA)], axis=0)
        q4b = q4.astype(BF16)
        new = lambda i: kvn[:, i * D_KV + g * HD_NSA:i * D_KV + (g + 1) * HD_NSA]
        scores, keeps = [], []
        has_new = jnp.int32(0)
        for r in range(N_SEL):
            n = idx_ref[b, g * N_SEL + r]
            past = ((n >= 0) & (n < nb_past)).astype(jnp.int32)
            keep = (blk_in_page == n % bpp) & (past > 0)
            has_new = has_new | (n == nb_past).astype(jnp.int32)
            keeps.append(keep)
            scores.append(jnp.where(keep, _dot(q4b, kbuf[slot, g, r].astype(BF16)) * scale, NEG))
        s_new = jnp.sum(bfr(q4) * bfr(new(2)), axis=-1, keepdims=True) * scale
        s_new = jnp.where(has_new > 0, s_new, NEG)
        m = s_new
        for s in scores:
            m = jnp.maximum(m, jnp.max(s, axis=-1, keepdims=True))
        p_new = jnp.where(s_new > 0.5 * NEG, jnp.exp(s_new - m), 0.0)
        l = p_new
        acc = bfr(p_new) * bfr(new(3))
        for r in range(N_SEL):
            p = jnp.where(scores[r] > 0.5 * NEG, jnp.exp(scores[r] - m), 0.0)
            l = l + jnp.sum(p, axis=-1, keepdims=True)
            v_t = jnp.where(keeps[r], vbuf[slot, g, r], 0.0).astype(BF16)
            acc = acc + _dot_nt(p.astype(BF16), v_t)
        o_slc = acc / jnp.maximum(l, 1e-30)
        wk = wk_ref[0, g].astype(BF16)
        wv = wv_ref[0, g].astype(BF16)
        s = jnp.where(w_ok, _dot(q4b, wk) * scale, NEG)
        s_new = jnp.sum(bfr(q4) * bfr(new(4)), axis=-1, keepdims=True) * scale
        m = jnp.maximum(jnp.max(s, axis=-1, keepdims=True), s_new)
        p = jnp.where(s > 0.5 * NEG, jnp.exp(s - m), 0.0)
        p_new = jnp.exp(s_new - m)
        l = jnp.sum(p, axis=-1, keepdims=True) + p_new
        o_win = (_dot_nt(p.astype(BF16), wv) + bfr(p_new) * bfr(new(5))) / jnp.maximum(l, 1e-30)
        for r in range(R_NSA):
            h = g * R_NSA + r
            pieces.append(gt[:, 3 * h:3 * h + 1] * ocmp[:, h * HD_NSA:(h + 1) * HD_NSA]
                          + gt[:, 3 * h + 1:3 * h + 2] * o_slc[r:r + 1]
                          + gt[:, 3 * h + 2:3 * h + 3] * o_win[r:r + 1])
    o_ref[0] = jnp.concatenate(pieces, axis=-1)


def _nsa1_attn_call(idx, page_table, nq, kv_new, gates, ocmp, win_k, win_v, slc_k, slc_v, q_pos, past_len):
    DB, n_pages = page_table.shape
    wb = win_k.shape[3]
    r3 = lambda a: a.reshape(DB, 1, a.shape[-1])
    row = lambda w: pl.BlockSpec((1, 1, w), lambda b, *_: (b, 0, 0))
    win = pl.BlockSpec((1, G_NSA, HD_NSA, wb), lambda b, *_: (b, 0, 0, 0))
    o = pl.pallas_call(
        functools.partial(_nsa1_attn_kernel, n_pages=n_pages, q_pos=q_pos, past_len=past_len),
        grid_spec=pltpu.PrefetchScalarGridSpec(
            num_scalar_prefetch=2,
            grid=(DB,),
            in_specs=[row(D_NSA), row(6 * D_KV), row(LANES), row(D_NSA), win, win,
                      pl.BlockSpec(memory_space=pl.ANY), pl.BlockSpec(memory_space=pl.ANY)],
            out_specs=row(D_NSA),
            scratch_shapes=[pltpu.VMEM((2, G_NSA, N_SEL, HD_NSA, PAGE_SIZE), F32),
                            pltpu.VMEM((2, G_NSA, N_SEL, HD_NSA, PAGE_SIZE), F32),
                            pltpu.SemaphoreType.DMA((2, 2))]),
        out_shape=jax.ShapeDtypeStruct((DB, 1, D_NSA), F32),
        compiler_params=_cparams(("arbitrary",)),
        name="nsa_step_attend",
    )(idx, page_table, r3(nq), r3(kv_new), r3(gates), r3(ocmp), win_k, win_v, slc_k, slc_v)
    return o.reshape(DB, D_NSA).astype(BF16)


def _token_tail(yret, onsa, h, mem_attend, lw, tm):
    w_out, g_mem, w_mq, w_mo, g_ffn, wq_h, sk_pad, u, v, g_final = lw
    h1, mq = _mixout_call(yret, onsa, h, w_out, g_mem, w_mq, tm)
    omem = mem_attend(mq)
    h2, xn, idx, gw = _route_call(omem, h1, w_mo, g_ffn, wq_h, sk_pad, tm)
    return _peer_call(idx, xn, gw, h2, g_final, u, v, min(PEER_TT, h.shape[0]))


def _all_tables(pos):
    return (_rot_tables(pos, H_RET, HD_RET, HD_RET, RET_THETA)
            + _rot_tables(pos, H_NSA, HD_NSA, ROPE_DIMS, ROPE_THETA)
            + _rot_tables(pos, G_NSA, HD_NSA, ROPE_DIMS, ROPE_THETA))


def kernel(x_prompt, x_sample, mem_prompt, state_ret, cache_cmp_k, cache_cmp_v, cache_slc_k, cache_slc_v,
           cache_win_k, cache_win_v, cache_mem_k, cache_mem_v, page_table, norm_mix_g, w_in, ret_gn_g,
           cmp_w, cmp_b, w_out, norm_mem_g, mem_norm_g, w_mq, w_mk, w_mv, w_mo, norm_ffn_g,
           peer_wq, peer_subkeys, peer_u, peer_v, norm_final_g):
    B, L, D = x_prompt.shape
    DB, LS, _ = x_sample.shape
    n_mem = mem_prompt.shape[1]
    n_pages = page_table.shape[1]
    past_len = n_pages * PAGE_SIZE
    assert w_in.shape[0] == 1 and LS == 1 and D == D_MODEL
    l = 0
    tm = 256

    w_all = _prep_w_in(w_in[l])
    cw_bd = _prep_cmp_w(cmp_w[l])
    cb2 = jnp.tile(cmp_b[l], (1, G_NSA)).reshape(2, 1, LANES)
    w_kv = jnp.concatenate([w_mk[l], w_mv[l]], axis=1).astype(BF16)
    wq_h = peer_wq[l].reshape(D, PEER_HEADS, PEER_DKEY).transpose(1, 0, 2).astype(BF16)
    lw = (w_out[l].astype(BF16), norm_mem_g[l], w_mq[l].astype(BF16), w_mo[l].astype(BF16), norm_ffn_g[l],
          wq_h, _prep_peer_keys(peer_subkeys[l]), _pack_table(peer_u[l]), _pack_table(peer_v[l]), norm_final_g)

    xs = x_sample.reshape(DB, D)
    rq, rk, rv, rg, nq, kvs, _, gt = _proj_call(xs, norm_mix_g[l], w_all,
                                                _all_tables(jnp.full((DB,), past_len, jnp.int32)), DB)
    yret, s_state = _ret1_call(rq, rk, rv, rg, ret_gn_g[l], state_ret[l])
    fm = lambda cache: jnp.swapaxes(cache[l], -1, -2)
    cw_pg = _prep_cmp_w(cmp_w[l].transpose(0, 2, 1, 3))
    tbl_k = _cmp_pages_call(fm(cache_cmp_k), cw_pg[0], cmp_b[l, 0])
    tbl_v = _cmp_pages_call(fm(cache_cmp_v), cw_pg[1], cmp_b[l, 1])
    ocmp, sel_idx = _nsa1_cmp_call(page_table, nq, kvs[:, 0:D_KV], kvs[:, D_KV:2 * D_KV], tbl_k, tbl_v, cw_bd, cb2,
                                   past_len)
    onsa = _nsa1_attn_call(sel_idx, page_table, nq, kvs, gt, ocmp, fm(cache_win_k), fm(cache_win_v),
                           fm(cache_slc_k), fm(cache_slc_v), past_len, past_len)
    y_s = _token_tail(yret, onsa, xs, lambda mq: _memattn1_call(mq, cache_mem_k[l], cache_mem_v[l]), lw, DB)

    new = lambda i: kvs[:, i * D_KV:(i + 1) * D_KV].reshape(DB, G_NSA, 1, HD_NSA)
    wb = cache_win_k.shape[3]
    keep_s = min(WINDOW, wb + 1)
    win = lambda cache, i: jnp.concatenate([cache[l], new(i)], axis=2)[:, :, wb + 1 - keep_s:]

    xp = x_prompt.reshape(B * L, D)
    rq, rk, rv, rg, nq, kv, kvb, gt = _proj_call(xp, norm_mix_g[l], w_all,
                                                 _all_tables(jnp.arange(L, dtype=jnp.int32)), tm)
    yret, p_state = _ret_call(rq, rk, rv, rg, ret_gn_g[l], B, L)
    onsa = _nsa_call(nq, kv, kvb, gt, cw_bd, cb2, B, L)
    mkv, mkvb = _normmm_call(mem_prompt.reshape(B * n_mem, D), mem_norm_g[l], w_kv, tm)
    y_p = _token_tail(yret, onsa, xp, lambda mq: _memattn_call(mq, mkvb, B, L, n_mem, tm), lw, tm)

    kv6 = kv.reshape(B, L, 6, G_NSA, HD_NSA)
    pages = lambda i: kv6[:, :, i].reshape(B, L // PAGE_SIZE, PAGE_SIZE, G_NSA, HD_NSA).transpose(0, 1, 3, 2, 4)
    keep = min(WINDOW, L)
    tail = lambda i: kv6[:, L - keep:, i].transpose(0, 2, 1, 3)
    mem4 = lambda t: t.reshape(B, n_mem, H_MEM, HD_MEM)
    st = lambda t: t[None]
    return (y_p.reshape(B, L, D), y_s.reshape(DB, 1, D), st(p_state),
            st(pages(0)), st(pages(1)), st(pages(2)), st(pages(3)), st(tail(4)), st(tail(5)),
            st(mem4(mkv[:, :D_MEM])), st(mem4(mkv[:, D_MEM:])), st(s_state),
            st(new(0)), st(new(1)), st(new(2)), st(new(3)), st(win(cache_win_k, 4)), st(win(cache_win_v, 5)))
```

```python
import functools

import numpy as np
import jax
import jax.numpy as jnp
from jax import lax
from jax.experimental import pallas as pl
from jax.experimental.pallas import tpu as pltpu

F32 = jnp.float32
BF16 = jnp.bfloat16

D_MODEL = 1024
PAGE_SIZE = 128
H_RET = 8
HD_RET = 64
RET_CHUNK = 128
RET_THETA = 10000.0
H_NSA = 8
G_NSA = 2
R_NSA = H_NSA // G_NSA
HD_NSA = 64
CMP_BLK = 64
N_SEL = 16
WINDOW = 512
SEL_FORCE = 1.0e4
ROPE_THETA = 500000.0
ROPE_DIMS = HD_NSA // 4
H_MEM = 4
HD_MEM = 128
PEER_KEYS = 128
PEER_HEADS = 8
PEER_DKEY = 128
PEER_TOPK = 16
PEER_ROW = 4
EPS = 1e-6

D_RET = H_RET * HD_RET
D_NSA = H_NSA * HD_NSA
D_KV = G_NSA * HD_NSA
D_MEM = H_MEM * HD_MEM
NEG = -1.0e30

LANES = 128
VMEM_LIMIT = 56 * 1024 * 1024

C_RQ, C_RQR, C_RK, C_RKR, C_RV, C_RG, C_NQ, C_NQR = (i * 512 for i in range(8))
C_KV = 4096
C_KVR = C_KV + 6 * D_KV
C_NG = C_KVR + 3 * D_KV
N_PROJ = C_NG + LANES


def _cparams(sem):
    return pltpu.CompilerParams(dimension_semantics=sem, vmem_limit_bytes=VMEM_LIMIT)


def _rms(x, g):
    return x * lax.rsqrt(jnp.mean(x * x, axis=-1, keepdims=True) + EPS) * g


def _dot(a, b):
    return jnp.dot(a, b, preferred_element_type=F32)


def _dot_nt(a, b):
    return lax.dot_general(a, b, (((1,), (1,)), ((), ())), preferred_element_type=F32)


def _full(shape):
    n = len(shape)
    return pl.BlockSpec(shape, lambda *_: (0,) * n)


def _partner_cols(n_heads, hd, n_rot):
    half = n_rot // 2
    j = np.arange(hd)
    p = np.where(j < half, j + half, np.where(j < n_rot, j - half, j))
    return (np.arange(n_heads)[:, None] * hd + p[None, :]).reshape(-1)


def _prep_w_in(w_in):
    o = 0
    seg = {}
    for name, size in (("rq", D_RET), ("rk", D_RET), ("rv", D_RET), ("rg", D_RET), ("nq", D_NSA),
                       ("ck", D_KV), ("cv", D_KV), ("sk", D_KV), ("sv", D_KV), ("wk", D_KV), ("wv", D_KV),
                       ("ng", 3 * H_NSA)):
        seg[name] = (o, size)
        o += size
    cols = lambda n: np.arange(seg[n][0], seg[n][0] + seg[n][1])
    pr = _partner_cols(H_RET, HD_RET, HD_RET)
    pn = _partner_cols(H_NSA, HD_NSA, ROPE_DIMS)
    pk = _partner_cols(G_NSA, HD_NSA, ROPE_DIMS)
    order = np.concatenate([
        cols("rq"), cols("rq")[pr], cols("rk"), cols("rk")[pr], cols("rv"), cols("rg"),
        cols("nq"), cols("nq")[pn],
        cols("ck"), cols("cv"), cols("sk"), cols("sv"), cols("wk"), cols("wv"),
        cols("ck")[pk], cols("sk")[pk], cols("wk")[pk], cols("ng")])
    w = jnp.take(w_in, jnp.asarray(order, jnp.int32), axis=1)
    w = jnp.pad(w, ((0, 0), (0, N_PROJ - w.shape[1])))
    return w.astype(BF16)


def _rot_tables(pos, n_heads, hd, n_rot, theta):
    half = n_rot // 2
    inv = theta ** (-jnp.arange(half, dtype=F32) / half)
    ang = pos.astype(F32)[:, None] * inv[None, :]
    cos, sin = jnp.cos(ang), jnp.sin(ang)
    P = pos.shape[0]
    c = jnp.concatenate([cos, cos, jnp.ones((P, hd - n_rot), F32)], axis=1)
    s = jnp.concatenate([-sin, sin, jnp.zeros((P, hd - n_rot), F32)], axis=1)
    return jnp.tile(c, (1, n_heads)), jnp.tile(s, (1, n_heads))


def _proj_kernel(x_ref, g_ref, w_ref, cr_ref, sr_ref, cn_ref, sn_ref, ck_ref, sk_ref,
                 rq_ref, rk_ref, rv_ref, rg_ref, nq_ref, kv_ref, kvb_ref, gt_ref):
    hn = _rms(x_ref[...], g_ref[...]).astype(BF16)
    seg = lambda c0, n: _dot(hn, w_ref[:, c0:c0 + n])
    cr, sr = cr_ref[...], sr_ref[...]
    rq_ref[...] = (seg(C_RQ, 512) * cr + seg(C_RQR, 512) * sr).astype(BF16)
    rk_ref[...] = ((seg(C_RK, 512) * cr + seg(C_RKR, 512) * sr) * (HD_RET ** -0.5)).astype(BF16)
    rv_ref[...] = seg(C_RV, 512).astype(BF16)
    rg_ref[...] = seg(C_RG, 512)
    nq_ref[...] = (seg(C_NQ, 512) * cn_ref[...] + seg(C_NQR, 512) * sn_ref[...]).astype(BF16)
    ck, sk = ck_ref[...], sk_ref[...]
    for i in range(6):
        a = seg(C_KV + i * D_KV, D_KV)
        if i % 2 == 0:
            a = a * ck + seg(C_KVR + (i // 2) * D_KV, D_KV) * sk
        kv_ref[:, i * D_KV:(i + 1) * D_KV] = a
        kvb_ref[:, i * D_KV:(i + 1) * D_KV] = a.astype(BF16)
    z = seg(C_NG, LANES)
    gt_ref[...] = 1.0 / (1.0 + jnp.exp(-z))


def _proj_call(x2d, g, w_all, tabs, tm):
    n = x2d.shape[0]
    period = tabs[0].shape[0] // tm
    row = lambda w: pl.BlockSpec((tm, w), lambda i: (i, 0))
    tab = lambda w: pl.BlockSpec((tm, w), lambda i: (i % period, 0))
    outs = [(512, BF16), (512, BF16), (512, BF16), (512, F32), (512, BF16), (6 * D_KV, F32), (6 * D_KV, BF16),
            (LANES, F32)]
    return pl.pallas_call(
        _proj_kernel,
        grid=(n // tm,),
        in_specs=[row(D_MODEL), _full((1, D_MODEL)), _full((D_MODEL, N_PROJ)),
                  tab(512), tab(512), tab(512), tab(512), tab(D_KV), tab(D_KV)],
        out_specs=[row(w) for w, _ in outs],
        out_shape=[jax.ShapeDtypeStruct((n, w), dt) for w, dt in outs],
        compiler_params=_cparams(("parallel",)),
        name="proj",
    )(x2d, g.reshape(1, D_MODEL), w_all, *tabs)


def _ret_consts(C):
    lg = jnp.log(1.0 - 2.0 ** (-5.0 - jnp.arange(H_RET, dtype=F32)))
    idx = jnp.arange(C, dtype=F32)
    diff = idx[:, None] - idx[None, :]
    dmat = jnp.where(diff >= 0, jnp.exp(lg[:, None, None] * jnp.maximum(diff, 0.0)), 0.0)
    xi = jnp.exp(lg[None, :] * (idx[:, None] + 1.0))
    zeta = jnp.exp(lg[:, None] * (C - 1.0 - idx[None, :]))
    g_c = jnp.exp(lg * C)
    return dmat, xi, zeta, g_c


def _ret_kernel(gc_ref, q_ref, k_ref, v_ref, rg_ref, gn_ref, dmat_ref, xi_ref, zeta_ref,
                y_ref, st_ref, s_scr):
    c = pl.program_id(1)

    @pl.when(c == 0)
    def _():
        s_scr[...] = jnp.zeros_like(s_scr)

    q, k, v = q_ref[...], k_ref[...], v_ref[...]
    k_t = k.astype(F32).T
    outs = []
    for h in range(H_RET):
        sl = slice(h * HD_RET, (h + 1) * HD_RET)
        qh, kh, vh = q[:, sl], k[:, sl], v[:, sl]
        att = _dot_nt(qh, kh) * dmat_ref[h]
        inner = _dot(att.astype(BF16), vh)
        s_old = s_scr[h]
        cross = _dot(qh, s_old.astype(BF16)) * xi_ref[:, h:h + 1]
        o = inner + cross
        kz = (k_t[sl, :] * zeta_ref[h:h + 1, :]).astype(BF16)
        s_scr[h] = s_old * gc_ref[h] + _dot(kz, vh)
        mu = jnp.mean(o, axis=-1, keepdims=True)
        d = o - mu
        var = jnp.mean(d * d, axis=-1, keepdims=True)
        outs.append(d * lax.rsqrt(var + EPS))
    on = jnp.concatenate(outs, axis=-1)
    rg = rg_ref[...]
    silu = rg * (1.0 / (1.0 + jnp.exp(-rg)))
    y_ref[...] = (silu * (on * gn_ref[...])).astype(BF16)

    @pl.when(c == pl.num_programs(1) - 1)
    def _():
        st_ref[0] = s_scr[...]


def _ret_call(rq, rk, rv, rg, gn, B, L):
    C = RET_CHUNK
    nC = L // C
    dmat, xi, zeta, g_c = _ret_consts(C)
    blk = lambda: pl.BlockSpec((C, D_RET), lambda b, c: (b * nC + c, 0))
    return pl.pallas_call(
        _ret_kernel,
        grid=(B, nC),
        in_specs=[pl.BlockSpec(memory_space=pltpu.SMEM), blk(), blk(), blk(), blk(), _full((1, D_RET)),
                  _full((H_RET, C, C)), _full((C, H_RET)), _full((H_RET, C))],
        out_specs=[blk(), pl.BlockSpec((1, H_RET, HD_RET, HD_RET), lambda b, c: (b, 0, 0, 0))],
        out_shape=[jax.ShapeDtypeStruct((B * L, D_RET), BF16),
                   jax.ShapeDtypeStruct((B, H_RET, HD_RET, HD_RET), F32)],
        scratch_shapes=[pltpu.VMEM((H_RET, HD_RET, HD_RET), F32)],
        compiler_params=_cparams(("parallel", "arbitrary")),
        name="retention",
    )(g_c, rq, rk, rv, rg, gn.reshape(1, D_RET), dmat, xi, zeta)


def _ret1_kernel(q_ref, k_ref, v_ref, s_ref, gam_ref, rg_ref, gn_ref, y_ref, so_ref):
    q, k, v, s, gam = q_ref[...], k_ref[...], v_ref[...], s_ref[...], gam_ref[...]
    qk = jnp.sum(q * k, axis=1, keepdims=True)
    cross = jnp.sum(q * s, axis=1, keepdims=True) * gam
    o = qk * v + cross
    so_ref[...] = s * gam + k * v
    mu = jnp.mean(o, axis=-1, keepdims=True)
    d = o - mu
    var = jnp.mean(d * d, axis=-1, keepdims=True)
    rg = rg_ref[...]
    silu = rg * (1.0 / (1.0 + jnp.exp(-rg)))
    y_ref[...] = silu * (d * lax.rsqrt(var + EPS) * gn_ref[...])


def _ret1_call(rq, rk, rv, rg, gn, state):
    DB = rq.shape[0]
    n = DB * H_RET
    _, _, _, g_c = _ret_consts(1)
    col = lambda t: t.astype(F32).reshape(n, HD_RET, 1)
    rowv = lambda t: t.astype(F32).reshape(n, 1, HD_RET)
    gam = jnp.tile(g_c, DB).reshape(n, 1, 1)
    gn3 = jnp.tile(gn.reshape(H_RET, 1, HD_RET), (DB, 1, 1))
    tb = 128
    b3 = lambda a, b: pl.BlockSpec((tb, a, b), lambda i: (i, 0, 0))
    y, s_new = pl.pallas_call(
        _ret1_kernel,
        grid=(n // tb,),
        in_specs=[b3(HD_RET, 1), b3(HD_RET, 1), b3(1, HD_RET), b3(HD_RET, HD_RET), b3(1, 1), b3(1, HD_RET),
                  b3(1, HD_RET)],
        out_specs=[b3(1, HD_RET), b3(HD_RET, HD_RET)],
        out_shape=[jax.ShapeDtypeStruct((n, 1, HD_RET), F32),
                   jax.ShapeDtypeStruct((n, HD_RET, HD_RET), F32)],
        compiler_params=_cparams(("parallel",)),
        name="retention_step",
    )(col(rq), col(rk), rowv(rv), state.astype(F32).reshape(n, HD_RET, HD_RET), gam, rowv(rg), gn3)
    return y.reshape(DB, D_RET).astype(BF16), s_new.reshape(DB, H_RET, HD_RET, HD_RET)


NSA_TQ = 128
NSA_TK = 512


def _prep_cmp_w(cmp_w):
    z = jnp.zeros_like(cmp_w)
    top = jnp.concatenate([cmp_w, z], axis=-1)
    bot = jnp.concatenate([z, cmp_w], axis=-1)
    return jnp.concatenate([top, bot], axis=-2).astype(BF16)


def _iota(shape, dim):
    return lax.broadcasted_iota(jnp.int32, shape, dim)


def _group_queries(nqf, g, tq):
    lane_g = _iota((tq, LANES), 1) // HD_NSA
    parts = []
    for r in range(R_NSA):
        h = g * R_NSA + r
        x = nqf[:, (h // 2) * LANES:(h // 2 + 1) * LANES]
        if h % 2 != g:
            x = pltpu.roll(x, HD_NSA, axis=1)
        parts.append((jnp.where(lane_g == g, x, 0.0) * (HD_NSA ** -0.5)).astype(BF16))
    return parts


def _select_blocks(score, nb, n_sel):
    s_t = score.T[:nb, :]
    n_i = _iota(s_t.shape, 0)
    rank = jnp.zeros(s_t.shape, F32)
    for m in range(nb):
        row = s_t[m:m + 1, :]
        ahead = (row > s_t) | ((row == s_t) & (n_i > m))
        rank = rank + jnp.where(ahead, 1.0, 0.0)
    sel_t = jnp.where((rank < n_sel) & (s_t >= 0.0), 1.0, 0.0)
    sel_t = jnp.concatenate([sel_t, jnp.zeros((LANES - nb, s_t.shape[1]), F32)], axis=0)
    return sel_t.T


def _flash_run(q_all, k_ref, v_ref, bias_ref, n_steps, tq, tk):
    def body(j, carry):
        m, l, acc = carry
        rows = pl.ds(pl.multiple_of(j * tk, tk), tk)
        s = _dot_nt(q_all, k_ref[rows, :]).reshape(G_NSA, R_NSA, tq, tk) + bias_ref[j][:, None]
        s = s.reshape(H_NSA, tq, tk)
        m_new = jnp.maximum(m, jnp.max(s, axis=-1, keepdims=True))
        alpha = jnp.exp(m - m_new)
        p = jnp.exp(s - m_new)
        l = alpha * l + jnp.sum(p, axis=-1, keepdims=True)
        pv = _dot(p.reshape(H_NSA * tq, tk).astype(BF16), v_ref[rows, :]).reshape(H_NSA, tq, LANES)
        return m_new, l, alpha * acc + pv

    init = (jnp.full((H_NSA, tq, 1), NEG, F32), jnp.zeros((H_NSA, tq, 1), F32),
            jnp.zeros((H_NSA, tq, LANES), F32))
    _, l, acc = lax.fori_loop(0, n_steps, body, init)
    return acc / jnp.maximum(l, 1e-30)


def _window_bias(tq):
    n = WINDOW // tq + 1
    dist = (jnp.arange(tq)[None, :, None] - jnp.arange(WINDOW + tq)[None, None, :]
            + jnp.arange(n)[:, None, None] * tq)
    return jnp.where((dist >= 0) & (dist < WINDOW), 0.0, NEG).astype(F32)


def _nsa_kernel(nq_ref, ckf_ref, cvf_ref, sk_ref, sv_ref, wk_ref, wv_ref, gt_ref, cw_ref, cb_ref, wb_ref,
                o_ref, ck_scr, cv_scr, bias_scr, *, nb):
    qi = pl.program_id(1)
    tq, tk = NSA_TQ, NSA_TK
    n_sel = min(N_SEL, nb)

    @pl.when(qi == 0)
    def _compress():
        for which, (src, dst) in enumerate(((ckf_ref, ck_scr), (cvf_ref, cv_scr))):
            def body(j, acc):
                x = src[pl.ds(j, nb, stride=CMP_BLK), :].astype(BF16)
                return acc + _dot(x, cw_ref[which, j])
            acc = lax.fori_loop(0, CMP_BLK, body, jnp.zeros((nb, LANES), F32))
            dst[...] = jnp.zeros_like(dst)
            dst[0:nb, :] = (acc + cb_ref[which]).astype(BF16)

    t0 = qi * tq
    nqf = nq_ref[...].astype(F32)
    gt = gt_ref[...]
    pos = t0 + _iota((tq, 1), 0)
    blk = _iota((1, LANES), 1)
    vis = ((blk * CMP_BLK + CMP_BLK - 1) <= pos) & (blk < nb)
    forced = ((blk == 0) | (blk == pos // CMP_BLK)) & (blk < nb)
    vis_bias = jnp.where(vis, 0.0, NEG)
    lane = _iota((tq, LANES), 1)
    kcol = _iota((1, tk), 1)
    blk_row = _iota((LANES, tk), 0)
    blk_of_key = _iota((LANES, tk), 1) // CMP_BLK

    n_kv = (t0 + tq + tk - 1) // tk

    q_all = jnp.concatenate(_group_queries(nqf, 0, tq) + _group_queries(nqf, 1, tq), axis=0)

    sc = _dot_nt(q_all, ck_scr[...]).reshape(H_NSA, tq, LANES) + vis_bias[None]
    pc = jnp.where(sc > 0.5 * NEG, jnp.exp(sc - jnp.max(sc, axis=-1, keepdims=True)), 0.0)
    pc = pc / jnp.maximum(jnp.sum(pc, axis=-1, keepdims=True), 1e-30)
    o_cmp = _dot(pc.reshape(H_NSA * tq, LANES).astype(BF16), cv_scr[...]).reshape(H_NSA, tq, LANES)

    sels = []
    for g in range(G_NSA):
        imp = pc[g * R_NSA] + pc[g * R_NSA + 1] + pc[g * R_NSA + 2] + pc[g * R_NSA + 3]
        score = jnp.where(forced, SEL_FORCE, jnp.where(vis, imp, -1.0))
        score = jnp.where(blk < nb, score, -2.0)
        sels.append(_select_blocks(score, nb, n_sel).astype(BF16))

    def mask_tile(j, _):
        expand = jnp.where(blk_row == blk_of_key + j * (tk // CMP_BLK), 1.0, 0.0).astype(BF16)
        causal = j * tk + kcol <= pos
        for g in range(G_NSA):
            chosen = _dot(sels[g], expand)
            bias_scr[j, g] = jnp.where((chosen > 0.5) & causal, 0.0, NEG)
        return 0

    lax.fori_loop(0, n_kv, mask_tile, 0)
    o_slc = _flash_run(q_all, sk_ref, sv_ref, bias_scr, n_kv, tq, tk)

    slab = pl.ds(pl.multiple_of(jnp.maximum(t0 - WINDOW, 0), tq), WINDOW + tq)
    wbias = wb_ref[jnp.minimum(qi, WINDOW // tq)]
    sw = _dot_nt(q_all, wk_ref[slab, :]).reshape(H_NSA, tq, WINDOW + tq) + wbias[None]
    pw = jnp.exp(sw - jnp.max(sw, axis=-1, keepdims=True))
    o_win = (_dot(pw.reshape(H_NSA * tq, WINDOW + tq).astype(BF16), wv_ref[slab, :]).reshape(H_NSA, tq, LANES)
             / jnp.maximum(jnp.sum(pw, axis=-1, keepdims=True), 1e-30))

    for c in range(H_NSA // 2):
        pair = []
        for h in (2 * c, 2 * c + 1):
            o_h = (gt[:, 3 * h:3 * h + 1] * o_cmp[h] + gt[:, 3 * h + 1:3 * h + 2] * o_slc[h]
                   + gt[:, 3 * h + 2:3 * h + 3] * o_win[h])
            if h % 2 != h // R_NSA:
                o_h = pltpu.roll(o_h, HD_NSA, axis=1)
            pair.append(o_h)
        o_ref[:, c * LANES:(c + 1) * LANES] = jnp.where(lane < HD_NSA, pair[0], pair[1]).astype(BF16)


def _nsa_call(nq, kv, kvb, gates, cw_bd, cb2, B, L):
    tq = NSA_TQ
    nQ = L // tq
    nb = L // CMP_BLK
    assert nb <= LANES and L % NSA_TK == 0 and NSA_TK % tq == 0 and WINDOW % tq == 0 and L >= WINDOW + tq
    win_bias = _window_bias(tq)
    rows = lambda w: pl.BlockSpec((tq, w), lambda b, q: (b * nQ + q, 0))
    seq = lambda c: pl.BlockSpec((L, LANES), lambda b, q: (b, c))
    return pl.pallas_call(
        functools.partial(_nsa_kernel, nb=nb),
        grid=(B, nQ),
        in_specs=[rows(D_NSA), seq(0), seq(1), seq(2), seq(3), seq(4), seq(5), rows(LANES),
                  _full((2, CMP_BLK, LANES, LANES)), _full((2, 1, LANES)), _full(win_bias.shape)],
        out_specs=rows(D_NSA),
        out_shape=jax.ShapeDtypeStruct((B * L, D_NSA), BF16),
        scratch_shapes=[pltpu.VMEM((LANES, LANES), BF16), pltpu.VMEM((LANES, LANES), BF16),
                        pltpu.VMEM((L // NSA_TK, G_NSA, tq, NSA_TK), F32)],
        compiler_params=_cparams(("parallel", "arbitrary")),
        name="nsa_prompt",
    )(nq, kv, kv, kvb, kvb, kvb, kvb, gates, cw_bd, cb2, win_bias)


def _normmm_kernel(x_ref, g_ref, w_ref, o_ref, ob_ref):
    y = _dot(_rms(x_ref[...], g_ref[...]).astype(BF16), w_ref[...])
    o_ref[...] = y
    ob_ref[...] = y.astype(BF16)


def _normmm_call(x2d, g, w, tm):
    n, d = x2d.shape
    m = w.shape[1]
    return pl.pallas_call(
        _normmm_kernel,
        grid=(n // tm,),
        in_specs=[pl.BlockSpec((tm, d), lambda i: (i, 0)), _full((1, d)), _full((d, m))],
        out_specs=[pl.BlockSpec((tm, m), lambda i: (i, 0))] * 2,
        out_shape=[jax.ShapeDtypeStruct((n, m), F32), jax.ShapeDtypeStruct((n, m), BF16)],
        compiler_params=_cparams(("parallel",)),
        name="norm_matmul",
    )(x2d, g.reshape(1, d), w)


def _mixout_kernel(yr_ref, on_ref, h_ref, wo_ref, g_ref, wq_ref, h1_ref, mq_ref):
    h1 = h_ref[...] + _dot(yr_ref[...], wo_ref[0:D_RET, :]) + _dot(on_ref[...], wo_ref[D_RET:, :])
    h1_ref[...] = h1
    mq_ref[...] = _dot(_rms(h1, g_ref[...]).astype(BF16), wq_ref[...]).astype(BF16)


def _mixout_call(yret, onsa, h, w_out, g_mem, w_mq, tm):
    n = h.shape[0]
    row = lambda w: pl.BlockSpec((tm, w), lambda i: (i, 0))
    return pl.pallas_call(
        _mixout_kernel,
        grid=(n // tm,),
        in_specs=[row(D_RET), row(D_NSA), row(D_MODEL), _full((D_RET + D_NSA, D_MODEL)), _full((1, D_MODEL)),
                  _full((D_MODEL, D_MEM))],
        out_specs=[row(D_MODEL), row(D_MEM)],
        out_shape=[jax.ShapeDtypeStruct((n, D_MODEL), F32), jax.ShapeDtypeStruct((n, D_MEM), BF16)],
        compiler_params=_cparams(("parallel",)),
        name="mixer_out",
    )(yret, onsa, h, w_out, g_mem.reshape(1, D_MODEL), w_mq)


def _memattn_kernel(q_ref, k_ref, v_ref, o_ref):
    q, k, v = q_ref[...], k_ref[...], v_ref[...]
    for h in range(H_MEM):
        sl = slice(h * HD_MEM, (h + 1) * HD_MEM)
        s = _dot_nt(q[:, sl], k[:, sl]) * (HD_MEM ** -0.5)
        p = jnp.exp(s - jnp.max(s, axis=-1, keepdims=True))
        p = p / jnp.sum(p, axis=-1, keepdims=True)
        o_ref[:, sl] = _dot(p.astype(BF16), v[:, sl]).astype(BF16)


def _memattn_call(mq, mkvb, B, L, n_mem, tm):
    nT = L // tm
    return pl.pallas_call(
        _memattn_kernel,
        grid=(B, nT),
        in_specs=[pl.BlockSpec((tm, D_MEM), lambda b, i: (b * nT + i, 0)),
                  pl.BlockSpec((n_mem, D_MEM), lambda b, i: (b, 0)),
                  pl.BlockSpec((n_mem, D_MEM), lambda b, i: (b, 1))],
        out_specs=pl.BlockSpec((tm, D_MEM), lambda b, i: (b * nT + i, 0)),
        out_shape=jax.ShapeDtypeStruct((B * L, D_MEM), BF16),
        compiler_params=_cparams(("parallel", "parallel")),
        name="mem_attention",
    )(mq, mkvb, mkvb)


def _memattn1_kernel(q_ref, k_ref, v_ref, o_ref):
    tb = q_ref.shape[0]
    for b in range(tb):
        q = q_ref[b]
        prod = k_ref[b] * q
        outs = []
        for h in range(H_MEM):
            sl = slice(h * HD_MEM, (h + 1) * HD_MEM)
            s = jnp.sum(prod[:, sl], axis=-1, keepdims=True) * (HD_MEM ** -0.5)
            p = jnp.exp(s - jnp.max(s, axis=0, keepdims=True))
            p = p / jnp.sum(p, axis=0, keepdims=True)
            outs.append(jnp.sum(p * v_ref[b][:, sl], axis=0, keepdims=True))
        o_ref[b] = jnp.concatenate(outs, axis=-1)


def _memattn1_call(mq, cache_k, cache_v):
    DB, n_mem = cache_k.shape[0], cache_k.shape[1]
    tb = 8
    blk = pl.BlockSpec((tb, n_mem, D_MEM), lambda i: (i, 0, 0))
    q3 = pl.BlockSpec((tb, 1, D_MEM), lambda i: (i, 0, 0))
    o = pl.pallas_call(
        _memattn1_kernel,
        grid=(DB // tb,),
        in_specs=[q3, blk, blk],
        out_specs=q3,
        out_shape=jax.ShapeDtypeStruct((DB, 1, D_MEM), F32),
        compiler_params=_cparams(("parallel",)),
        name="mem_attention_step",
    )(mq.astype(F32).reshape(DB, 1, D_MEM), cache_k.reshape(DB, n_mem, D_MEM), cache_v.reshape(DB, n_mem, D_MEM))
    return o.reshape(DB, D_MEM).astype(BF16)


def _prep_peer_keys(subkeys):
    half = PEER_DKEY // 2
    z = jnp.zeros_like(subkeys[:, 0])
    k0 = jnp.concatenate([subkeys[:, 0], z], axis=-1)
    k1 = jnp.concatenate([z, subkeys[:, 1]], axis=-1)
    return jnp.concatenate([k0, k1], axis=1).astype(BF16)


def _top_rows(x, k, payload=None):
    n = x.shape[-2]
    ri = _iota(x.shape, x.ndim - 2)
    vals, picks = [], []
    for _ in range(k):
        m = jnp.max(x, axis=-2, keepdims=True)
        i = jnp.min(jnp.where(x == m, ri, n), axis=-2, keepdims=True)
        hit = ri == i
        vals.append(m)
        picks.append(i if payload is None else jnp.max(jnp.where(hit, payload, -1), axis=-2, keepdims=True))
        x = jnp.where(hit, -jnp.inf, x)
    return jnp.concatenate(vals, axis=-2), jnp.concatenate(picks, axis=-2)


def _route_kernel(om_ref, h1_ref, wo_ref, g_ref, wq_ref, sk_ref, h2_ref, xn_ref, idx_ref, gw_ref,
                  idx_scr, gw_scr):
    tm = h1_ref.shape[0]
    h2 = h1_ref[...] + _dot(om_ref[...], wo_ref[...])
    h2_ref[...] = h2
    xn = _rms(h2, g_ref[...])
    xn_ref[...] = xn
    xb = xn.astype(BF16)

    k = PEER_TOPK
    n_b = [k // (a + 1) for a in range(k)]
    pad = -sum(n_b) % 8
    lanes = min(tm, LANES)

    def head(h, _):
        pq = _dot(xb, wq_ref[h]).astype(BF16)
        s_all = _dot_nt(sk_ref[h], pq)
        rows = pl.ds(pl.multiple_of(h * k, k), k)
        for c in range(tm // lanes):
            s = s_all[:, c * lanes:(c + 1) * lanes].reshape(2, PEER_KEYS, lanes)
            v12, i12 = _top_rows(s, k)
            cand = jnp.concatenate([v12[0][a:a + 1] + v12[1][0:n_b[a]] for a in range(k)]
                                   + [jnp.full((pad, lanes), -jnp.inf, F32)], axis=0)
            cidx = jnp.concatenate([i12[0][a:a + 1] * PEER_KEYS + i12[1][0:n_b[a]] for a in range(k)]
                                   + [jnp.full((pad, lanes), -1, jnp.int32)], axis=0)
            top, expert = _top_rows(cand, k, payload=cidx)
            e = jnp.exp(top - top[0:1, :])
            idx_scr[rows, c * lanes:(c + 1) * lanes] = expert * PEER_ROW
            gw_scr[rows, c * lanes:(c + 1) * lanes] = e / jnp.sum(e, axis=0, keepdims=True)
        return 0

    lax.fori_loop(0, PEER_HEADS, head, 0)
    idx_ref[...] = idx_scr[...].T
    gw_ref[...] = gw_scr[...].T


def _route_call(omem, h1, w_mo, g_ffn, wq_h, sk_pad, tm):
    n = h1.shape[0]
    nk = PEER_HEADS * PEER_TOPK
    row = lambda w: pl.BlockSpec((tm, w), lambda i: (i, 0))
    return pl.pallas_call(
        _route_kernel,
        grid=(n // tm,),
        in_specs=[row(D_MEM), row(D_MODEL), _full((D_MEM, D_MODEL)), _full((1, D_MODEL)),
                  _full((PEER_HEADS, D_MODEL, PEER_DKEY)), _full((PEER_HEADS, 2 * PEER_KEYS, PEER_DKEY))],
        out_specs=[row(D_MODEL), row(D_MODEL), row(nk), row(nk)],
        out_shape=[jax.ShapeDtypeStruct((n, D_MODEL), F32), jax.ShapeDtypeStruct((n, D_MODEL), F32),
                   jax.ShapeDtypeStruct((n, nk), jnp.int32), jax.ShapeDtypeStruct((n, nk), F32)],
        scratch_shapes=[pltpu.VMEM((nk, tm), jnp.int32), pltpu.VMEM((nk, tm), F32)],
        compiler_params=_cparams(("parallel",)),
        name="peer_route",
    )(omem, h1, w_mo, g_ffn.reshape(1, D_MODEL), wq_h, sk_pad)


def _gelu_tanh(x):
    return 0.5 * x * (1.0 + jnp.tanh(0.7978845608028654 * (x + 0.044715 * x * x * x)))


PEER_TT = 64


def _pack_table(t):
    e, d = t.shape
    b = lax.bitcast_convert_type(t.astype(BF16), jnp.uint16).astype(jnp.uint32)
    w = b[:, :d // 2] | (b[:, d // 2:] << 16)
    return w.reshape(e * PEER_ROW, LANES)


def _expert_row(tab_ref, off):
    w = tab_ref[pl.ds(pl.multiple_of(off, PEER_ROW), PEER_ROW), :]
    lo = pltpu.bitcast(w << 16, F32)
    hi = pltpu.bitcast(w & jnp.uint32(0xFFFF0000), F32)
    return lo, hi


def _peer_act_kernel(idx_ref, x_ref, gw_ref, tab_ref, c_ref, part_scr, act_scr):
    tt, nk = gw_ref.shape
    sub = _iota((8, LANES), 0)
    keep_pairs = (sub % 4) < 2
    keep_even = (sub % 2) == 0
    feed = (0, 4, 2, 6, 1, 5, 3, 7)

    def row_sums8(p):
        p = [p[i] for i in feed]
        v = [jnp.concatenate([p[2 * i], p[2 * i + 1]], axis=0) for i in range(4)]
        w = [x + pltpu.roll(x, 6, axis=0) for x in v]
        u = [jnp.where(keep_pairs, w[2 * i], pltpu.roll(w[2 * i + 1], 2, axis=0)) for i in range(2)]
        z = [x + pltpu.roll(x, 7, axis=0) for x in u]
        return jnp.where(keep_even, z[0], pltpu.roll(z[1], 1, axis=0))

    def lane_sums(t):
        act_scr[pl.ds(t, 1), :] = jnp.sum(part_scr[t].T, axis=0, keepdims=True)

    part_scr[0] = jnp.zeros((nk, LANES), F32)

    def token(t, _):
        lane_sums(jnp.maximum(t - 1, 0))
        xt = x_ref[t]
        xlo, xhi = xt[0:PEER_ROW], xt[PEER_ROW:]
        for j0 in range(0, nk, 8):
            prods = []
            for j in range(j0, j0 + 8):
                lo, hi = _expert_row(tab_ref, idx_ref[t, j])
                prods.append(lo * xlo + hi * xhi)
            part_scr[t, j0:j0 + 8, :] = row_sums8(prods)
        return 0

    lax.fori_loop(0, tt, token, 0)
    lane_sums(tt - 1)
    c_ref[...] = gw_ref[...] * _gelu_tanh(act_scr[...])


def _peer_out_kernel(idx_ref, c_ref, h2_ref, gf_ref, tab_ref, y_ref, splat_scr):
    tt, nk = idx_ref.shape
    n_acc = 8

    def splat(buf, t):
        splat_scr[buf] = jnp.broadcast_to(c_ref[pl.ds(t, 1), :], (nk, nk)).T

    def accumulate(buf, t):
        acc_lo = [jnp.zeros((PEER_ROW, LANES), F32)] * n_acc
        acc_hi = [jnp.zeros((PEER_ROW, LANES), F32)] * n_acc
        for j in range(nk):
            lo, hi = _expert_row(tab_ref, idx_ref[t, j])
            c = splat_scr[buf, j:j + 1, :]
            acc_lo[j % n_acc] = acc_lo[j % n_acc] + c * lo
            acc_hi[j % n_acc] = acc_hi[j % n_acc] + c * hi
        tree = lambda v: v[0] if len(v) == 1 else tree([a + b for a, b in zip(v[0::2], v[1::2])])
        y_ref[t] = h2_ref[t] + jnp.concatenate([tree(acc_lo), tree(acc_hi)], axis=0)

    splat(0, 0)

    def token_pair(i, _):
        t = 2 * i
        splat(1, t + 1)
        accumulate(0, t)
        splat(0, jnp.minimum(t + 2, tt - 1))
        accumulate(1, t + 1)
        return 0

    lax.fori_loop(0, tt // 2, token_pair, 0)
    h3 = y_ref[...]
    ms = jnp.sum(jnp.sum(h3 * h3, axis=2, keepdims=True), axis=1, keepdims=True) * (1.0 / D_MODEL)
    y_ref[...] = h3 * lax.rsqrt(ms + EPS) * gf_ref[...]


def _peer_call(idx, xn, gw, h2, g_final, u_tab, v_tab, tt):
    n, nk = idx.shape
    assert n % tt == 0 and nk % 8 == 0 and tt % 2 == 0
    smem = lambda: pl.BlockSpec((tt, nk), lambda i: (i, 0), memory_space=pltpu.SMEM)
    tile = lambda: pl.BlockSpec((tt, 8, LANES), lambda i: (i, 0, 0))
    table = lambda t: pl.BlockSpec(t.shape, lambda i: (0, 0), pipeline_mode=pl.Buffered(1))
    as_tiles = lambda a: a.reshape(n, 8, LANES)
    c = pl.pallas_call(
        _peer_act_kernel,
        grid=(n // tt,),
        in_specs=[smem(), tile(), pl.BlockSpec((tt, nk), lambda i: (i, 0)), table(u_tab)],
        out_specs=pl.BlockSpec((tt, nk), lambda i: (i, 0)),
        out_shape=jax.ShapeDtypeStruct((n, nk), F32),
        scratch_shapes=[pltpu.VMEM((tt, nk, LANES), F32), pltpu.VMEM((tt, nk), F32)],
        compiler_params=_cparams(("arbitrary",)),
        name="peer_act",
    )(idx, as_tiles(xn), gw, u_tab)
    y = pl.pallas_call(
        _peer_out_kernel,
        grid=(n // tt,),
        in_specs=[smem(), pl.BlockSpec((tt, nk), lambda i: (i, 0)), tile(), _full((8, LANES)), table(v_tab)],
        out_specs=tile(),
        out_shape=jax.ShapeDtypeStruct((n, 8, LANES), F32),
        scratch_shapes=[pltpu.VMEM((2, nk, nk), F32)],
        compiler_params=_cparams(("arbitrary",)),
        name="peer_out",
    )(idx, c, as_tiles(h2), g_final.reshape(8, LANES), v_tab)
    return y.reshape(n, D_MODEL)


CMP_PAGES = 128


def _cmp_pages_kernel(x_ref, w_ref, b_ref, o_ref):
    n_rows = o_ref.shape[0]

    def body(d, acc):
        x = x_ref[pl.ds(d, n_rows, stride=HD_NSA), :].astype(BF16)
        return acc + _dot(x, w_ref[d])

    acc = lax.fori_loop(0, HD_NSA, body, jnp.zeros((n_rows, LANES), F32), unroll=4)
    o_ref[...] = acc + b_ref[...]


def _cmp_pages_call(pool_t, w_bd, b):
    n_phys = pool_t.shape[0]
    rows = n_phys * G_NSA
    step = CMP_PAGES * G_NSA
    assert rows % step == 0
    return pl.pallas_call(
        _cmp_pages_kernel,
        grid=(rows // step,),
        in_specs=[pl.BlockSpec((step * HD_NSA, PAGE_SIZE), lambda i: (i, 0)),
                  _full((HD_NSA, PAGE_SIZE, LANES)), _full((1, LANES))],
        out_specs=pl.BlockSpec((step, LANES), lambda i: (i, 0)),
        out_shape=jax.ShapeDtypeStruct((rows, LANES), F32),
        compiler_params=_cparams(("parallel",)),
        name="compress_pages",
    )(pool_t.reshape(rows * HD_NSA, PAGE_SIZE), w_bd, jnp.tile(b, 2).reshape(1, LANES))


def _nsa1_cmp_kernel(pt_ref, nq_ref, ckn_ref, cvn_ref, tk_ref, tv_ref, cw_ref, cb_ref, ocmp_ref, idx_ref,
                     kg_scr, vg_scr, *, n_pages, q_pos):
    tb = nq_ref.shape[0]
    nb_past = n_pages * (PAGE_SIZE // CMP_BLK)
    base = pl.program_id(0) * tb
    lane = _iota((1, LANES), 1)
    blk_n = 2 * (lane % HD_NSA) + lane // HD_NSA
    forced = (blk_n == 0) | (blk_n == q_pos // CMP_BLK)
    vis = (blk_n * CMP_BLK + CMP_BLK - 1) <= q_pos
    new_vis = (nb_past * CMP_BLK + CMP_BLK - 1) <= q_pos
    new_forced = nb_past == q_pos // CMP_BLK
    n_row = jnp.broadcast_to(blk_n, (LANES, LANES))
    n_col = 2 * (_iota((LANES, LANES), 0) % HD_NSA) + _iota((LANES, LANES), 0) // HD_NSA
    lane8 = _iota((1, LANES), 1) // HD_NSA

    def sample(b, _):
        qrow = nq_ref[pl.ds(b, 1), :].astype(F32)
        new_k = _dot(ckn_ref[pl.ds(b, 1), :].astype(BF16), cw_ref[0]) + cb_ref[0]
        new_v = _dot(cvn_ref[pl.ds(b, 1), :].astype(BF16), cw_ref[1]) + cb_ref[1]
        o_row, idx_row = [], jnp.full((1, LANES), -1, jnp.int32)
        for g in range(G_NSA):
            def gather(i, _):
                r = pt_ref[base + b, i] * G_NSA + g
                kg_scr[pl.ds(i, 1), :] = tk_ref[pl.ds(r, 1), :]
                vg_scr[pl.ds(i, 1), :] = tv_ref[pl.ds(r, 1), :]
                return 0
            lax.fori_loop(0, n_pages, gather, 0)
            rows = []
            for c in range(2):
                for r in range(R_NSA):
                    h = g * R_NSA + r
                    x = qrow[:, (h // 2) * LANES:(h // 2 + 1) * LANES]
                    if h % 2 != c:
                        x = pltpu.roll(x, HD_NSA, axis=1)
                    rows.append(jnp.where(lane8 == c, x, 0.0))
            qpad = jnp.concatenate(rows, axis=0)
            s = _dot_nt(qpad.astype(BF16), kg_scr[...].astype(BF16)) * (HD_NSA ** -0.5)
            s3 = s.reshape(2, R_NSA, n_pages)
            qg = qpad[R_NSA * g:R_NSA * (g + 1), :]
            nk_g = jnp.where(lane8 == g, new_k.astype(BF16).astype(F32), 0.0)
            s_new = jnp.sum(qg.astype(BF16).astype(F32) * nk_g, axis=-1, keepdims=True) * (HD_NSA ** -0.5)
            s_new = jnp.where(new_vis, s_new, NEG)[None]
            m = jnp.maximum(jnp.max(jnp.max(s3, axis=2, keepdims=True), axis=0, keepdims=True), s_new)
            p = jnp.exp(s3 - m)
            p_new = jnp.where(s_new > 0.5 * NEG, jnp.exp(s_new - m), 0.0)
            l = jnp.sum(jnp.sum(p, axis=2, keepdims=True), axis=0, keepdims=True) + p_new
            inv = 1.0 / jnp.maximum(l, 1e-30)
            p = p * inv
            p_new = p_new * inv
            res = _dot(p.reshape(2 * R_NSA, n_pages).astype(BF16), vg_scr[...].astype(BF16))
            o4 = res[0:R_NSA] + pltpu.roll(res[R_NSA:], HD_NSA, axis=1)
            nv_g = new_v.astype(BF16).astype(F32)
            if g == 1:
                nv_g = pltpu.roll(nv_g, HD_NSA, axis=1)
            o4 = o4 + p_new[0].astype(BF16).astype(F32) * nv_g
            o_row += [o4[r:r + 1, 0:HD_NSA] for r in range(R_NSA)]
            imp2 = jnp.sum(p, axis=1)
            imp = jnp.concatenate([imp2[0:1], imp2[1:2]], axis=-1)
            score = jnp.where(forced, SEL_FORCE, jnp.where(vis, imp, -1.0))
            imp_new = jnp.sum(p_new)
            sc_new = SEL_FORCE if new_forced else jnp.where(new_vis, imp_new, -1.0)
            a = jnp.broadcast_to(score, (LANES, LANES))
            bt = a.T
            ahead = (bt > a) | ((bt == a) & (n_col < n_row))
            rank = jnp.sum(jnp.where(ahead, 1.0, 0.0), axis=0, keepdims=True) + jnp.where(sc_new > score, 1.0, 0.0)
            rank_new = jnp.sum(jnp.where(score >= sc_new, 1.0, 0.0))
            for r in range(N_SEL):
                hit = (rank == r) & (score >= 0.0)
                val = jnp.sum(jnp.where(hit, blk_n + 1, 0)) - 1
                val = jnp.where((rank_new == r) & (sc_new >= 0.0), nb_past, val)
                idx_row = jnp.where(lane == g * N_SEL + r, val, idx_row)
        ocmp_ref[pl.ds(b, 1), :] = jnp.concatenate(o_row, axis=-1)
        idx_ref[pl.ds(b, 1), :] = idx_row
        return 0

    lax.fori_loop(0, tb, sample, 0)


def _nsa1_cmp_call(page_table, nq, ck_new, cv_new, tbl_k, tbl_v, cw_bd, cb2, q_pos):
    DB, n_pages = page_table.shape
    assert n_pages * (PAGE_SIZE // CMP_BLK) == LANES
    tb = 8
    row = lambda w: pl.BlockSpec((tb, w), lambda i, pt: (i, 0))
    whole = lambda a: pl.BlockSpec(a.shape, lambda i, pt: (0,) * a.ndim, pipeline_mode=pl.Buffered(1))
    return pl.pallas_call(
        functools.partial(_nsa1_cmp_kernel, n_pages=n_pages, q_pos=q_pos),
        grid_spec=pltpu.PrefetchScalarGridSpec(
            num_scalar_prefetch=1,
            grid=(DB // tb,),
            in_specs=[row(D_NSA), row(LANES), row(LANES), whole(tbl_k), whole(tbl_v),
                      pl.BlockSpec((2, LANES, LANES), lambda i, pt: (0, 0, 0)),
                      pl.BlockSpec((2, 1, LANES), lambda i, pt: (0, 0, 0))],
            out_specs=[row(D_NSA), row(LANES)],
            scratch_shapes=[pltpu.VMEM((n_pages, LANES), F32), pltpu.VMEM((n_pages, LANES), F32)]),
        out_shape=[jax.ShapeDtypeStruct((DB, D_NSA), F32), jax.ShapeDtypeStruct((DB, LANES), jnp.int32)],
        compiler_params=_cparams(("arbitrary",)),
        name="nsa_step_compressed",
    )(page_table, nq.astype(F32), ck_new, cv_new, tbl_k, tbl_v, cw_bd[:, 0], cb2)


def _nsa1_attn_kernel(idx_ref, pt_ref, nq_ref, kvn_ref, gt_ref, ocmp_ref, wk_ref, wv_ref, sk_hbm, sv_hbm,
                      o_ref, kbuf, vbuf, sem, *, n_pages, q_pos, past_len):
    b = pl.program_id(0)
    bpp = PAGE_SIZE // CMP_BLK
    nb_past = n_pages * bpp
    wb = wk_ref.shape[3]

    def block_copies(bb, slot, g, r):
        n = idx_ref[bb, g * N_SEL + r]
        past = (n >= 0) & (n < nb_past)
        page = pt_ref[bb, jnp.clip(n, 0, nb_past - 1) // bpp]
        ck = pltpu.make_async_copy(sk_hbm.at[page, g], kbuf.at[slot, g, r], sem.at[slot, 0])
        cv = pltpu.make_async_copy(sv_hbm.at[page, g], vbuf.at[slot, g, r], sem.at[slot, 1])
        return past, ck, cv

    def issue(bb, slot):
        for g in range(G_NSA):
            for r in range(N_SEL):
                past, ck, cv = block_copies(bb, slot, g, r)

                @pl.when(past)
                def _():
                    ck.start()
                    cv.start()

                @pl.when(jnp.logical_not(past))
                def _():
                    kbuf[slot, g, r] = jnp.zeros((HD_NSA, PAGE_SIZE), F32)
                    vbuf[slot, g, r] = jnp.zeros((HD_NSA, PAGE_SIZE), F32)

    @pl.when(b == 0)
    def _():
        issue(0, 0)

    @pl.when(b + 1 < pl.num_programs(0))
    def _():
        issue(b + 1, (b + 1) % 2)

    slot = b % 2
    for g in range(G_NSA):
        for r in range(N_SEL):
            past, ck, cv = block_copies(b, slot, g, r)

            @pl.when(past)
            def _():
                ck.wait()
                cv.wait()

    qrow = nq_ref[0].astype(F32)
    kvn = kvn_ref[0]
    gt = gt_ref[0]
    ocmp = ocmp_ref[0]
    scale = HD_NSA ** -0.5
    blk_in_page = _iota((1, PAGE_SIZE), 1) // CMP_BLK
    wpos = past_len - wb + _iota((1, wb), 1)
    wdist = q_pos - wpos
    w_ok = (wdist >= 0) & (wdist < WINDOW) & (wpos >= 0)
    bfr = lambda t: t.astype(BF16).astype(F32)
    pieces = []
    for g in range(G_NSA):
        q4 = jnp.concatenate([qrow[:, (g * R_NSA + r) * HD_NSA:(g * R_NSA + r + 1) * HD_NSA]
                              for r in range(R_NSA)], axis=0)
        q4b = q4.astype(BF16)
        new = lambda i: kvn[:, i * D_KV + g * HD_NSA:i * D_KV + (g + 1) * HD_NSA]
        scores, keeps = [], []
        has_new = jnp.int32(0)
        for r in range(N_SEL):
            n = idx_ref[b, g * N_SEL + r]
            past = ((n >= 0) & (n < nb_past)).astype(jnp.int32)
            keep = (blk_in_page == n % bpp) & (past > 0)
            has_new = has_new | (n == nb_past).astype(jnp.int32)
            keeps.append(keep)
            scores.append(jnp.where(keep, _dot(q4b, kbuf[slot, g, r].astype(BF16)) * scale, NEG))
        s_new = jnp.sum(bfr(q4) * bfr(new(2)), axis=-1, keepdims=True) * scale
        s_new = jnp.where(has_new > 0, s_new, NEG)
        m = s_new
        for s in scores:
            m = jnp.maximum(m, jnp.max(s, axis=-1, keepdims=True))
        p_new = jnp.where(s_new > 0.5 * NEG, jnp.exp(s_new - m), 0.0)
        l = p_new
        acc = bfr(p_new) * bfr(new(3))
        for r in range(N_SEL):
            p = jnp.where(scores[r] > 0.5 * NEG, jnp.exp(scores[r] - m), 0.0)
            l = l + jnp.sum(p, axis=-1, keepdims=True)
            v_t = jnp.where(keeps[r], vbuf[slot, g, r], 0.0).astype(BF16)
            acc = acc + _dot_nt(p.astype(BF16), v_t)
        o_slc = acc / jnp.maximum(l, 1e-30)
        wk = wk_ref[0, g].astype(BF16)
        wv = wv_ref[0, g].astype(BF16)
        s = jnp.where(w_ok, _dot(q4b, wk) * scale, NEG)
        s_new = jnp.sum(bfr(q4) * bfr(new(4)), axis=-1, keepdims=True) * scale
        m = jnp.maximum(jnp.max(s, axis=-1, keepdims=True), s_new)
        p = jnp.where(s > 0.5 * NEG, jnp.exp(s - m), 0.0)
        p_new = jnp.exp(s_new - m)
        l = jnp.sum(p, axis=-1, keepdims=True) + p_new
        o_win = (_dot_nt(p.astype(BF16), wv) + bfr(p_new) * bfr(new(5))) / jnp.maximum(l, 1e-30)
        for r in range(R_NSA):
            h = g * R_NSA + r
            pieces.append(gt[:, 3 * h:3 * h + 1] * ocmp[:, h * HD_NSA:(h + 1) * HD_NSA]
                          + gt[:, 3 * h + 1:3 * h + 2] * o_slc[r:r + 1]
                          + gt[:, 3 * h + 2:3 * h + 3] * o_win[r:r + 1])
    o_ref[0] = jnp.concatenate(pieces, axis=-1)


def _nsa1_attn_call(idx, page_table, nq, kv_new, gates, ocmp, win_k, win_v, slc_k, slc_v, q_pos, past_len):
    DB, n_pages = page_table.shape
    wb = win_k.shape[3]
    r3 = lambda a: a.reshape(DB, 1, a.shape[-1])
    row = lambda w: pl.BlockSpec((1, 1, w), lambda b, *_: (b, 0, 0))
    win = pl.BlockSpec((1, G_NSA, HD_NSA, wb), lambda b, *_: (b, 0, 0, 0))
    o = pl.pallas_call(
        functools.partial(_nsa1_attn_kernel, n_pages=n_pages, q_pos=q_pos, past_len=past_len),
        grid_spec=pltpu.PrefetchScalarGridSpec(
            num_scalar_prefetch=2,
            grid=(DB,),
            in_specs=[row(D_NSA), row(6 * D_KV), row(LANES), row(D_NSA), win, win,
                      pl.BlockSpec(memory_space=pl.ANY), pl.BlockSpec(memory_space=pl.ANY)],
            out_specs=row(D_NSA),
            scratch_shapes=[pltpu.VMEM((2, G_NSA, N_SEL, HD_NSA, PAGE_SIZE), F32),
                            pltpu.VMEM((2, G_NSA, N_SEL, HD_NSA, PAGE_SIZE), F32),
                            pltpu.SemaphoreType.DMA((2, 2))]),
        out_shape=jax.ShapeDtypeStruct((DB, 1, D_NSA), F32),
        compiler_params=_cparams(("arbitrary",)),
        name="nsa_step_attend",
    )(idx, page_table, r3(nq), r3(kv_new), r3(gates), r3(ocmp), win_k, win_v, slc_k, slc_v)
    return o.reshape(DB, D_NSA).astype(BF16)


def _token_tail(yret, onsa, h, mem_attend, lw, tm):
    w_out, g_mem, w_mq, w_mo, g_ffn, wq_h, sk_pad, u, v, g_final = lw
    h1, mq = _mixout_call(yret, onsa, h, w_out, g_mem, w_mq, tm)
    omem = mem_attend(mq)
    h2, xn, idx, gw = _route_call(omem, h1, w_mo, g_ffn, wq_h, sk_pad, tm)
    return _peer_call(idx, xn, gw, h2, g_final, u, v, min(PEER_TT, h.shape[0]))


def _all_tables(pos):
    return (_rot_tables(pos, H_RET, HD_RET, HD_RET, RET_THETA)
            + _rot_tables(pos, H_NSA, HD_NSA, ROPE_DIMS, ROPE_THETA)
            + _rot_tables(pos, G_NSA, HD_NSA, ROPE_DIMS, ROPE_THETA))


def kernel(x_prompt, x_sample, mem_prompt, state_ret, cache_cmp_k, cache_cmp_v, cache_slc_k, cache_slc_v,
           cache_win_k, cache_win_v, cache_mem_k, cache_mem_v, page_table, norm_mix_g, w_in, ret_gn_g,
           cmp_w, cmp_b, w_out, norm_mem_g, mem_norm_g, w_mq, w_mk, w_mv, w_mo, norm_ffn_g,
           peer_wq, peer_subkeys, peer_u, peer_v, norm_final_g):
    B, L, D = x_prompt.shape
    DB, LS, _ = x_sample.shape
    n_mem = mem_prompt.shape[1]
    n_pages = page_table.shape[1]
    past_len = n_pages * PAGE_SIZE
    assert w_in.shape[0] == 1 and LS == 1 and D == D_MODEL
    l = 0
    tm = 256

    w_all = _prep_w_in(w_in[l])
    cw_bd = _prep_cmp_w(cmp_w[l])
    cb2 = jnp.tile(cmp_b[l], (1, G_NSA)).reshape(2, 1, LANES)
    w_kv = jnp.concatenate([w_mk[l], w_mv[l]], axis=1).astype(BF16)
    wq_h = peer_wq[l].reshape(D, PEER_HEADS, PEER_DKEY).transpose(1, 0, 2).astype(BF16)
    lw = (w_out[l].astype(BF16), norm_mem_g[l], w_mq[l].astype(BF16), w_mo[l].astype(BF16), norm_ffn_g[l],
          wq_h, _prep_peer_keys(peer_subkeys[l]), _pack_table(peer_u[l]), _pack_table(peer_v[l]), norm_final_g)

    xs = x_sample.reshape(DB, D)
    rq, rk, rv, rg, nq, kvs, _, gt = _proj_call(xs, norm_mix_g[l], w_all,
                                                _all_tables(jnp.full((DB,), past_len, jnp.int32)), DB)
    yret, s_state = _ret1_call(rq, rk, rv, rg, ret_gn_g[l], state_ret[l])
    fm = lambda cache: jnp.swapaxes(cache[l], -1, -2)
    cw_pg = _prep_cmp_w(cmp_w[l].transpose(0, 2, 1, 3))
    tbl_k = _cmp_pages_call(fm(cache_cmp_k), cw_pg[0], cmp_b[l, 0])
    tbl_v = _cmp_pages_call(fm(cache_cmp_v), cw_pg[1], cmp_b[l, 1])
    ocmp, sel_idx = _nsa1_cmp_call(page_table, nq, kvs[:, 0:D_KV], kvs[:, D_KV:2 * D_KV], tbl_k, tbl_v, cw_bd, cb2,
                                   past_len)
    onsa = _nsa1_attn_call(sel_idx, page_table, nq, kvs, gt, ocmp, fm(cache_win_k), fm(cache_win_v),
                           fm(cache_slc_k), fm(cache_slc_v), past_len, past_len)
    y_s = _token_tail(yret, onsa, xs, lambda mq: _memattn1_call(mq, cache_mem_k[l], cache_mem_v[l]), lw, DB)

    new = lambda i: kvs[:, i * D_KV:(i + 1) * D_KV].reshape(DB, G_NSA, 1, HD_NSA)
    wb = cache_win_k.shape[3]
    keep_s = min(WINDOW, wb + 1)
    win = lambda cache, i: jnp.concatenate([cache[l], new(i)], axis=2)[:, :, wb + 1 - keep_s:]

    xp = x_prompt.reshape(B * L, D)
    rq, rk, rv, rg, nq, kv, kvb, gt = _proj_call(xp, norm_mix_g[l], w_all,
                                                 _all_tables(jnp.arange(L, dtype=jnp.int32)), tm)
    yret, p_state = _ret_call(rq, rk, rv, rg, ret_gn_g[l], B, L)
    onsa = _nsa_call(nq, kv, kvb, gt, cw_bd, cb2, B, L)
    mkv, mkvb = _normmm_call(mem_prompt.reshape(B * n_mem, D), mem_norm_g[l], w_kv, tm)
    y_p = _token_tail(yret, onsa, xp, lambda mq: _memattn_call(mq, mkvb, B, L, n_mem, tm), lw, tm)

    kv6 = kv.reshape(B, L, 6, G_NSA, HD_NSA)
    pages = lambda i: kv6[:, :, i].reshape(B, L // PAGE_SIZE, PAGE_SIZE, G_NSA, HD_NSA).transpose(0, 1, 3, 2, 4)
    keep = min(WINDOW, L)
    tail = lambda i: kv6[:, L - keep:, i].transpose(0, 2, 1, 3)
    mem4 = lambda t: t.reshape(B, n_mem, H_MEM, HD_MEM)
    st = lambda t: t[None]
    return (y_p.reshape(B, L, D), y_s.reshape(DB, 1, D), st(p_state),
            st(pages(0)), st(pages(1)), st(pages(2)), st(pages(3)), st(tail(4)), st(tail(5)),
            st(mem4(mkv[:, :D_MEM])), st(mem4(mkv[:, D_MEM:])), st(s_state),
            st(new(0)), st(new(1)), st(new(2)), st(new(3)), st(win(cache_win_k, 4)), st(win(cache_win_v, 5)))
```

```python
import functools

import numpy as np
import jax
import jax.numpy as jnp
from jax import lax
from jax.experimental import pallas as pl
from jax.experimental.pallas import tpu as pltpu

F32 = jnp.float32
BF16 = jnp.bfloat16

D_MODEL = 1024
PAGE_SIZE = 128
H_RET = 8
HD_RET = 64
RET_CHUNK = 128
RET_THETA = 10000.0
H_NSA = 8
G_NSA = 2
R_NSA = H_NSA // G_NSA
HD_NSA = 64
CMP_BLK = 64
N_SEL = 16
WINDOW = 512
SEL_FORCE = 1.0e4
ROPE_THETA = 500000.0
ROPE_DIMS = HD_NSA // 4
H_MEM = 4
HD_MEM = 128
PEER_KEYS = 128
PEER_HEADS = 8
PEER_DKEY = 128
PEER_TOPK = 16
PEER_ROW = 4
EPS = 1e-6

D_RET = H_RET * HD_RET
D_NSA = H_NSA * HD_NSA
D_KV = G_NSA * HD_NSA
D_MEM = H_MEM * HD_MEM
NEG = -1.0e30

LANES = 128
VMEM_LIMIT = 56 * 1024 * 1024

C_RQ, C_RQR, C_RK, C_RKR, C_RV, C_RG, C_NQ, C_NQR = (i * 512 for i in range(8))
C_KV = 4096
C_KVR = C_KV + 6 * D_KV
C_NG = C_KVR + 3 * D_KV
N_PROJ = C_NG + LANES


def _cparams(sem):
    return pltpu.CompilerParams(dimension_semantics=sem, vmem_limit_bytes=VMEM_LIMIT)


def _rms(x, g):
    return x * lax.rsqrt(jnp.mean(x * x, axis=-1, keepdims=True) + EPS) * g


def _dot(a, b):
    return jnp.dot(a, b, preferred_element_type=F32)


def _dot_nt(a, b):
    return lax.dot_general(a, b, (((1,), (1,)), ((), ())), preferred_element_type=F32)


def _full(shape):
    n = len(shape)
    return pl.BlockSpec(shape, lambda *_: (0,) * n)


def _partner_cols(n_heads, hd, n_rot):
    half = n_rot // 2
    j = np.arange(hd)
    p = np.where(j < half, j + half, np.where(j < n_rot, j - half, j))
    return (np.arange(n_heads)[:, None] * hd + p[None, :]).reshape(-1)


def _prep_w_in(w_in):
    o = 0
    seg = {}
    for name, size in (("rq", D_RET), ("rk", D_RET), ("rv", D_RET), ("rg", D_RET), ("nq", D_NSA),
                       ("ck", D_KV), ("cv", D_KV), ("sk", D_KV), ("sv", D_KV), ("wk", D_KV), ("wv", D_KV),
                       ("ng", 3 * H_NSA)):
        seg[name] = (o, size)
        o += size
    cols = lambda n: np.arange(seg[n][0], seg[n][0] + seg[n][1])
    pr = _partner_cols(H_RET, HD_RET, HD_RET)
    pn = _partner_cols(H_NSA, HD_NSA, ROPE_DIMS)
    pk = _partner_cols(G_NSA, HD_NSA, ROPE_DIMS)
    order = np.concatenate([
        cols("rq"), cols("rq")[pr], cols("rk"), cols("rk")[pr], cols("rv"), cols("rg"),
        cols("nq"), cols("nq")[pn],
        cols("ck"), cols("cv"), cols("sk"), cols("sv"), cols("wk"), cols("wv"),
        cols("ck")[pk], cols("sk")[pk], cols("wk")[pk], cols("ng")])
    w = jnp.take(w_in, jnp.asarray(order, jnp.int32), axis=1)
    w = jnp.pad(w, ((0, 0), (0, N_PROJ - w.shape[1])))
    return w.astype(BF16)


def _rot_tables(pos, n_heads, hd, n_rot, theta):
    half = n_rot // 2
    inv = theta ** (-jnp.arange(half, dtype=F32) / half)
    ang = pos.astype(F32)[:, None] * inv[None, :]
    cos, sin = jnp.cos(ang), jnp.sin(ang)
    P = pos.shape[0]
    c = jnp.concatenate([cos, cos, jnp.ones((P, hd - n_rot), F32)], axis=1)
    s = jnp.concatenate([-sin, sin, jnp.zeros((P, hd - n_rot), F32)], axis=1)
    return jnp.tile(c, (1, n_heads)), jnp.tile(s, (1, n_heads))


def _proj_kernel(x_ref, g_ref, w_ref, cr_ref, sr_ref, cn_ref, sn_ref, ck_ref, sk_ref,
                 rq_ref, rk_ref, rv_ref, rg_ref, nq_ref, kv_ref, kvb_ref, gt_ref):
    hn = _rms(x_ref[...], g_ref[...]).astype(BF16)
    seg = lambda c0, n: _dot(hn, w_ref[:, c0:c0 + n])
    cr, sr = cr_ref[...], sr_ref[...]
    rq_ref[...] = (seg(C_RQ, 512) * cr + seg(C_RQR, 512) * sr).astype(BF16)
    rk_ref[...] = ((seg(C_RK, 512) * cr + seg(C_RKR, 512) * sr) * (HD_RET ** -0.5)).astype(BF16)
    rv_ref[...] = seg(C_RV, 512).astype(BF16)
    rg_ref[...] = seg(C_RG, 512)
    nq_ref[...] = (seg(C_NQ, 512) * cn_ref[...] + seg(C_NQR, 512) * sn_ref[...]).astype(BF16)
    ck, sk = ck_ref[...], sk_ref[...]
    for i in range(6):
        a = seg(C_KV + i * D_KV, D_KV)
        if i % 2 == 0:
            a = a * ck + seg(C_KVR + (i // 2) * D_KV, D_KV) * sk
        kv_ref[:, i * D_KV:(i + 1) * D_KV] = a
        kvb_ref[:, i * D_KV:(i + 1) * D_KV] = a.astype(BF16)
    z = seg(C_NG, LANES)
    gt_ref[...] = 1.0 / (1.0 + jnp.exp(-z))


def _proj_call(x2d, g, w_all, tabs, tm):
    n = x2d.shape[0]
    period = tabs[0].shape[0] // tm
    row = lambda w: pl.BlockSpec((tm, w), lambda i: (i, 0))
    tab = lambda w: pl.BlockSpec((tm, w), lambda i: (i % period, 0))
    outs = [(512, BF16), (512, BF16), (512, BF16), (512, F32), (512, BF16), (6 * D_KV, F32), (6 * D_KV, BF16),
            (LANES, F32)]
    return pl.pallas_call(
        _proj_kernel,
        grid=(n // tm,),
        in_specs=[row(D_MODEL), _full((1, D_MODEL)), _full((D_MODEL, N_PROJ)),
                  tab(512), tab(512), tab(512), tab(512), tab(D_KV), tab(D_KV)],
        out_specs=[row(w) for w, _ in outs],
        out_shape=[jax.ShapeDtypeStruct((n, w), dt) for w, dt in outs],
        compiler_params=_cparams(("parallel",)),
        name="proj",
    )(x2d, g.reshape(1, D_MODEL), w_all, *tabs)


def _ret_consts(C):
    lg = jnp.log(1.0 - 2.0 ** (-5.0 - jnp.arange(H_RET, dtype=F32)))
    idx = jnp.arange(C, dtype=F32)
    diff = idx[:, None] - idx[None, :]
    dmat = jnp.where(diff >= 0, jnp.exp(lg[:, None, None] * jnp.maximum(diff, 0.0)), 0.0)
    xi = jnp.exp(lg[None, :] * (idx[:, None] + 1.0))
    zeta = jnp.exp(lg[:, None] * (C - 1.0 - idx[None, :]))
    g_c = jnp.exp(lg * C)
    return dmat, xi, zeta, g_c


def _ret_kernel(gc_ref, q_ref, k_ref, v_ref, rg_ref, gn_ref, dmat_ref, xi_ref, zeta_ref,
                y_ref, st_ref, s_scr):
    c = pl.program_id(1)

    @pl.when(c == 0)
    def _():
        s_scr[...] = jnp.zeros_like(s_scr)

    q, k, v = q_ref[...], k_ref[...], v_ref[...]
    k_t = k.astype(F32).T
    outs = []
    for h in range(H_RET):
        sl = slice(h * HD_RET, (h + 1) * HD_RET)
        qh, kh, vh = q[:, sl], k[:, sl], v[:, sl]
        att = _dot_nt(qh, kh) * dmat_ref[h]
        inner = _dot(att.astype(BF16), vh)
        s_old = s_scr[h]
        cross = _dot(qh, s_old.astype(BF16)) * xi_ref[:, h:h + 1]
        o = inner + cross
        kz = (k_t[sl, :] * zeta_ref[h:h + 1, :]).astype(BF16)
        s_scr[h] = s_old * gc_ref[h] + _dot(kz, vh)
        mu = jnp.mean(o, axis=-1, keepdims=True)
        d = o - mu
        var = jnp.mean(d * d, axis=-1, keepdims=True)
        outs.append(d * lax.rsqrt(var + EPS))
    on = jnp.concatenate(outs, axis=-1)
    rg = rg_ref[...]
    silu = rg * (1.0 / (1.0 + jnp.exp(-rg)))
    y_ref[...] = (silu * (on * gn_ref[...])).astype(BF16)

    @pl.when(c == pl.num_programs(1) - 1)
    def _():
        st_ref[0] = s_scr[...]


def _ret_call(rq, rk, rv, rg, gn, B, L):
    C = RET_CHUNK
    nC = L // C
    dmat, xi, zeta, g_c = _ret_consts(C)
    blk = lambda: pl.BlockSpec((C, D_RET), lambda b, c: (b * nC + c, 0))
    return pl.pallas_call(
        _ret_kernel,
        grid=(B, nC),
        in_specs=[pl.BlockSpec(memory_space=pltpu.SMEM), blk(), blk(), blk(), blk(), _full((1, D_RET)),
                  _full((H_RET, C, C)), _full((C, H_RET)), _full((H_RET, C))],
        out_specs=[blk(), pl.BlockSpec((1, H_RET, HD_RET, HD_RET), lambda b, c: (b, 0, 0, 0))],
        out_shape=[jax.ShapeDtypeStruct((B * L, D_RET), BF16),
                   jax.ShapeDtypeStruct((B, H_RET, HD_RET, HD_RET), F32)],
        scratch_shapes=[pltpu.VMEM((H_RET, HD_RET, HD_RET), F32)],
        compiler_params=_cparams(("parallel", "arbitrary")),
        name="retention",
    )(g_c, rq, rk, rv, rg, gn.reshape(1, D_RET), dmat, xi, zeta)


def _ret1_kernel(q_ref, k_ref, v_ref, s_ref, gam_ref, rg_ref, gn_ref, y_ref, so_ref):
    q, k, v, s, gam = q_ref[...], k_ref[...], v_ref[...], s_ref[...], gam_ref[...]
    qk = jnp.sum(q * k, axis=1, keepdims=True)
    cross = jnp.sum(q * s, axis=1, keepdims=True) * gam
    o = qk * v + cross
    so_ref[...] = s * gam + k * v
    mu = jnp.mean(o, axis=-1, keepdims=True)
    d = o - mu
    var = jnp.mean(d * d, axis=-1, keepdims=True)
    rg = rg_ref[...]
    silu = rg * (1.0 / (1.0 + jnp.exp(-rg)))
    y_ref[...] = silu * (d * lax.rsqrt(var + EPS) * gn_ref[...])


def _ret1_call(rq, rk, rv, rg, gn, state):
    DB = rq.shape[0]
    n = DB * H_RET
    _, _, _, g_c = _ret_consts(1)
    col = lambda t: t.astype(F32).reshape(n, HD_RET, 1)
    rowv = lambda t: t.astype(F32).reshape(n, 1, HD_RET)
    gam = jnp.tile(g_c, DB).reshape(n, 1, 1)
    gn3 = jnp.tile(gn.reshape(H_RET, 1, HD_RET), (DB, 1, 1))
    tb = 128
    b3 = lambda a, b: pl.BlockSpec((tb, a, b), lambda i: (i, 0, 0))
    y, s_new = pl.pallas_call(
        _ret1_kernel,
        grid=(n // tb,),
        in_specs=[b3(HD_RET, 1), b3(HD_RET, 1), b3(1, HD_RET), b3(HD_RET, HD_RET), b3(1, 1), b3(1, HD_RET),
                  b3(1, HD_RET)],
        out_specs=[b3(1, HD_RET), b3(HD_RET, HD_RET)],
        out_shape=[jax.ShapeDtypeStruct((n, 1, HD_RET), F32),
                   jax.ShapeDtypeStruct((n, HD_RET, HD_RET), F32)],
        compiler_params=_cparams(("parallel",)),
        name="retention_step",
    )(col(rq), col(rk), rowv(rv), state.astype(F32).reshape(n, HD_RET, HD_RET), gam, rowv(rg), gn3)
    return y.reshape(DB, D_RET).astype(BF16), s_new.reshape(DB, H_RET, HD_RET, HD_RET)


NSA_TQ = 128
NSA_TK = 512


def _prep_cmp_w(cmp_w):
    z = jnp.zeros_like(cmp_w)
    top = jnp.concatenate([cmp_w, z], axis=-1)
    bot = jnp.concatenate([z, cmp_w], axis=-1)
    return jnp.concatenate([top, bot], axis=-2).astype(BF16)


def _iota(shape, dim):
    return lax.broadcasted_iota(jnp.int32, shape, dim)


def _group_queries(nqf, g, tq):
    lane_g = _iota((tq, LANES), 1) // HD_NSA
    parts = []
    for r in range(R_NSA):
        h = g * R_NSA + r
        x = nqf[:, (h // 2) * LANES:(h // 2 + 1) * LANES]
        if h % 2 != g:
            x = pltpu.roll(x, HD_NSA, axis=1)
        parts.append((jnp.where(lane_g == g, x, 0.0) * (HD_NSA ** -0.5)).astype(BF16))
    return parts


def _select_blocks(score, nb, n_sel):
    s_t = score.T[:nb, :]
    n_i = _iota(s_t.shape, 0)
    rank = jnp.zeros(s_t.shape, F32)
    for m in range(nb):
        row = s_t[m:m + 1, :]
        ahead = (row > s_t) | ((row == s_t) & (n_i > m))
        rank = rank + jnp.where(ahead, 1.0, 0.0)
    sel_t = jnp.where((rank < n_sel) & (s_t >= 0.0), 1.0, 0.0)
    sel_t = jnp.concatenate([sel_t, jnp.zeros((LANES - nb, s_t.shape[1]), F32)], axis=0)
    return sel_t.T


def _flash_run(q_all, k_ref, v_ref, bias_ref, n_steps, tq, tk):
    def body(j, carry):
        m, l, acc = carry
        rows = pl.ds(pl.multiple_of(j * tk, tk), tk)
        s = _dot_nt(q_all, k_ref[rows, :]).reshape(G_NSA, R_NSA, tq, tk) + bias_ref[j][:, None]
        s = s.reshape(H_NSA, tq, tk)
        m_new = jnp.maximum(m, jnp.max(s, axis=-1, keepdims=True))
        alpha = jnp.exp(m - m_new)
        p = jnp.exp(s - m_new)
        l = alpha * l + jnp.sum(p, axis=-1, keepdims=True)
        pv = _dot(p.reshape(H_NSA * tq, tk).astype(BF16), v_ref[rows, :]).reshape(H_NSA, tq, LANES)
        return m_new, l, alpha * acc + pv

    init = (jnp.full((H_NSA, tq, 1), NEG, F32), jnp.zeros((H_NSA, tq, 1), F32),
            jnp.zeros((H_NSA, tq, LANES), F32))
    _, l, acc = lax.fori_loop(0, n_steps, body, init)
    return acc / jnp.maximum(l, 1e-30)


def _window_bias(tq):
    n = WINDOW // tq + 1
    dist = (jnp.arange(tq)[None, :, None] - jnp.arange(WINDOW + tq)[None, None, :]
            + jnp.arange(n)[:, None, None] * tq)
    return jnp.where((dist >= 0) & (dist < WINDOW), 0.0, NEG).astype(F32)


def _nsa_kernel(nq_ref, ckf_ref, cvf_ref, sk_ref, sv_ref, wk_ref, wv_ref, gt_ref, cw_ref, cb_ref, wb_ref,
                o_ref, ck_scr, cv_scr, bias_scr, *, nb):
    qi = pl.program_id(1)
    tq, tk = NSA_TQ, NSA_TK
    n_sel = min(N_SEL, nb)

    @pl.when(qi == 0)
    def _compress():
        for which, (src, dst) in enumerate(((ckf_ref, ck_scr), (cvf_ref, cv_scr))):
            def body(j, acc):
                x = src[pl.ds(j, nb, stride=CMP_BLK), :].astype(BF16)
                return acc + _dot(x, cw_ref[which, j])
            acc = lax.fori_loop(0, CMP_BLK, body, jnp.zeros((nb, LANES), F32))
            dst[...] = jnp.zeros_like(dst)
            dst[0:nb, :] = (acc + cb_ref[which]).astype(BF16)

    t0 = qi * tq
    nqf = nq_ref[...].astype(F32)
    gt = gt_ref[...]
    pos = t0 + _iota((tq, 1), 0)
    blk = _iota((1, LANES), 1)
    vis = ((blk * CMP_BLK + CMP_BLK - 1) <= pos) & (blk < nb)
    forced = ((blk == 0) | (blk == pos // CMP_BLK)) & (blk < nb)
    vis_bias = jnp.where(vis, 0.0, NEG)
    lane = _iota((tq, LANES), 1)
    kcol = _iota((1, tk), 1)
    blk_row = _iota((LANES, tk), 0)
    blk_of_key = _iota((LANES, tk), 1) // CMP_BLK

    n_kv = (t0 + tq + tk - 1) // tk

    q_all = jnp.concatenate(_group_queries(nqf, 0, tq) + _group_queries(nqf, 1, tq), axis=0)

    sc = _dot_nt(q_all, ck_scr[...]).reshape(H_NSA, tq, LANES) + vis_bias[None]
    pc = jnp.where(sc > 0.5 * NEG, jnp.exp(sc - jnp.max(sc, axis=-1, keepdims=True)), 0.0)
    pc = pc / jnp.maximum(jnp.sum(pc, axis=-1, keepdims=True), 1e-30)
    o_cmp = _dot(pc.reshape(H_NSA * tq, LANES).astype(BF16), cv_scr[...]).reshape(H_NSA, tq, LANES)

    sels = []
    for g in range(G_NSA):
        imp = pc[g * R_NSA] + pc[g * R_NSA + 1] + pc[g * R_NSA + 2] + pc[g * R_NSA + 3]
        score = jnp.where(forced, SEL_FORCE, jnp.where(vis, imp, -1.0))
        score = jnp.where(blk < nb, score, -2.0)
        sels.append(_select_blocks(score, nb, n_sel).astype(BF16))

    def mask_tile(j, _):
        expand = jnp.where(blk_row == blk_of_key + j * (tk // CMP_BLK), 1.0, 0.0).astype(BF16)
        causal = j * tk + kcol <= pos
        for g in range(G_NSA):
            chosen = _dot(sels[g], expand)
            bias_scr[j, g] = jnp.where((chosen > 0.5) & causal, 0.0, NEG)
        return 0

    lax.fori_loop(0, n_kv, mask_tile, 0)
    o_slc = _flash_run(q_all, sk_ref, sv_ref, bias_scr, n_kv, tq, tk)

    slab = pl.ds(pl.multiple_of(jnp.maximum(t0 - WINDOW, 0), tq), WINDOW + tq)
    wbias = wb_ref[jnp.minimum(qi, WINDOW // tq)]
    sw = _dot_nt(q_all, wk_ref[slab, :]).reshape(H_NSA, tq, WINDOW + tq) + wbias[None]
    pw = jnp.exp(sw - jnp.max(sw, axis=-1, keepdims=True))
    o_win = (_dot(pw.reshape(H_NSA * tq, WINDOW + tq).astype(BF16), wv_ref[slab, :]).reshape(H_NSA, tq, LANES)
             / jnp.maximum(jnp.sum(pw, axis=-1, keepdims=True), 1e-30))

    for c in range(H_NSA // 2):
        pair = []
        for h in (2 * c, 2 * c + 1):
            o_h = (gt[:, 3 * h:3 * h + 1] * o_cmp[h] + gt[:, 3 * h + 1:3 * h + 2] * o_slc[h]
                   + gt[:, 3 * h + 2:3 * h + 3] * o_win[h])
            if h % 2 != h // R_NSA:
                o_h = pltpu.roll(o_h, HD_NSA, axis=1)
            pair.append(o_h)
        o_ref[:, c * LANES:(c + 1) * LANES] = jnp.where(lane < HD_NSA, pair[0], pair[1]).astype(BF16)


def _nsa_call(nq, kv, kvb, gates, cw_bd, cb2, B, L):
    tq = NSA_TQ
    nQ = L // tq
    nb = L // CMP_BLK
    assert nb <= LANES and L % NSA_TK == 0 and NSA_TK % tq == 0 and WINDOW % tq == 0 and L >= WINDOW + tq
    win_bias = _window_bias(tq)
    rows = lambda w: pl.BlockSpec((tq, w), lambda b, q: (b * nQ + q, 0))
    seq = lambda c: pl.BlockSpec((L, LANES), lambda b, q: (b, c))
    return pl.pallas_call(
        functools.partial(_nsa_kernel, nb=nb),
        grid=(B, nQ),
        in_specs=[rows(D_NSA), seq(0), seq(1), seq(2), seq(3), seq(4), seq(5), rows(LANES),
                  _full((2, CMP_BLK, LANES, LANES)), _full((2, 1, LANES)), _full(win_bias.shape)],
        out_specs=rows(D_NSA),
        out_shape=jax.ShapeDtypeStruct((B * L, D_NSA), BF16),
        scratch_shapes=[pltpu.VMEM((LANES, LANES), BF16), pltpu.VMEM((LANES, LANES), BF16),
                        pltpu.VMEM((L // NSA_TK, G_NSA, tq, NSA_TK), F32)],
        compiler_params=_cparams(("parallel", "arbitrary")),
        name="nsa_prompt",
    )(nq, kv, kv, kvb, kvb, kvb, kvb, gates, cw_bd, cb2, win_bias)


def _normmm_kernel(x_ref, g_ref, w_ref, o_ref, ob_ref):
    y = _dot(_rms(x_ref[...], g_ref[...]).astype(BF16), w_ref[...])
    o_ref[...] = y
    ob_ref[...] = y.astype(BF16)


def _normmm_call(x2d, g, w, tm):
    n, d = x2d.shape
    m = w.shape[1]
    return pl.pallas_call(
        _normmm_kernel,
        grid=(n // tm,),
        in_specs=[pl.BlockSpec((tm, d), lambda i: (i, 0)), _full((1, d)), _full((d, m))],
        out_specs=[pl.BlockSpec((tm, m), lambda i: (i, 0))] * 2,
        out_shape=[jax.ShapeDtypeStruct((n, m), F32), jax.ShapeDtypeStruct((n, m), BF16)],
        compiler_params=_cparams(("parallel",)),
        name="norm_matmul",
    )(x2d, g.reshape(1, d), w)


def _mixout_kernel(yr_ref, on_ref, h_ref, wo_ref, g_ref, wq_ref, h1_ref, mq_ref):
    h1 = h_ref[...] + _dot(yr_ref[...], wo_ref[0:D_RET, :]) + _dot(on_ref[...], wo_ref[D_RET:, :])
    h1_ref[...] = h1
    mq_ref[...] = _dot(_rms(h1, g_ref[...]).astype(BF16), wq_ref[...]).astype(BF16)


def _mixout_call(yret, onsa, h, w_out, g_mem, w_mq, tm):
    n = h.shape[0]
    row = lambda w: pl.BlockSpec((tm, w), lambda i: (i, 0))
    return pl.pallas_call(
        _mixout_kernel,
        grid=(n // tm,),
        in_specs=[row(D_RET), row(D_NSA), row(D_MODEL), _full((D_RET + D_NSA, D_MODEL)), _full((1, D_MODEL)),
                  _full((D_MODEL, D_MEM))],
        out_specs=[row(D_MODEL), row(D_MEM)],
        out_shape=[jax.ShapeDtypeStruct((n, D_MODEL), F32), jax.ShapeDtypeStruct((n, D_MEM), BF16)],
        compiler_params=_cparams(("parallel",)),
        name="mixer_out",
    )(yret, onsa, h, w_out, g_mem.reshape(1, D_MODEL), w_mq)


def _memattn_kernel(q_ref, k_ref, v_ref, o_ref):
    q, k, v = q_ref[...], k_ref[...], v_ref[...]
    for h in range(H_MEM):
        sl = slice(h * HD_MEM, (h + 1) * HD_MEM)
        s = _dot_nt(q[:, sl], k[:, sl]) * (HD_MEM ** -0.5)
        p = jnp.exp(s - jnp.max(s, axis=-1, keepdims=True))
        p = p / jnp.sum(p, axis=-1, keepdims=True)
        o_ref[:, sl] = _dot(p.astype(BF16), v[:, sl]).astype(BF16)


def _memattn_call(mq, mkvb, B, L, n_mem, tm):
    nT = L // tm
    return pl.pallas_call(
        _memattn_kernel,
        grid=(B, nT),
        in_specs=[pl.BlockSpec((tm, D_MEM), lambda b, i: (b * nT + i, 0)),
                  pl.BlockSpec((n_mem, D_MEM), lambda b, i: (b, 0)),
                  pl.BlockSpec((n_mem, D_MEM), lambda b, i: (b, 1))],
        out_specs=pl.BlockSpec((tm, D_MEM), lambda b, i: (b * nT + i, 0)),
        out_shape=jax.ShapeDtypeStruct((B * L, D_MEM), BF16),
        compiler_params=_cparams(("parallel", "parallel")),
        name="mem_attention",
    )(mq, mkvb, mkvb)


def _memattn1_kernel(q_ref, k_ref, v_ref, o_ref):
    tb = q_ref.shape[0]
    for b in range(tb):
        q = q_ref[b]
        prod = k_ref[b] * q
        outs = []
        for h in range(H_MEM):
            sl = slice(h * HD_MEM, (h + 1) * HD_MEM)
            s = jnp.sum(prod[:, sl], axis=-1, keepdims=True) * (HD_MEM ** -0.5)
            p = jnp.exp(s - jnp.max(s, axis=0, keepdims=True))
            p = p / jnp.sum(p, axis=0, keepdims=True)
            outs.append(jnp.sum(p * v_ref[b][:, sl], axis=0, keepdims=True))
        o_ref[b] = jnp.concatenate(outs, axis=-1)


def _memattn1_call(mq, cache_k, cache_v):
    DB, n_mem = cache_k.shape[0], cache_k.shape[1]
    tb = 8
    blk = pl.BlockSpec((tb, n_mem, D_MEM), lambda i: (i, 0, 0))
    q3 = pl.BlockSpec((tb, 1, D_MEM), lambda i: (i, 0, 0))
    o = pl.pallas_call(
        _memattn1_kernel,
        grid=(DB // tb,),
        in_specs=[q3, blk, blk],
        out_specs=q3,
        out_shape=jax.ShapeDtypeStruct((DB, 1, D_MEM), F32),
        compiler_params=_cparams(("parallel",)),
        name="mem_attention_step",
    )(mq.astype(F32).reshape(DB, 1, D_MEM), cache_k.reshape(DB, n_mem, D_MEM), cache_v.reshape(DB, n_mem, D_MEM))
    return o.reshape(DB, D_MEM).astype(BF16)


def _prep_peer_keys(subkeys):
    half = PEER_DKEY // 2
    z = jnp.zeros_like(subkeys[:, 0])
    k0 = jnp.concatenate([subkeys[:, 0], z], axis=-1)
    k1 = jnp.concatenate([z, subkeys[:, 1]], axis=-1)
    return jnp.concatenate([k0, k1], axis=1).astype(BF16)


def _top_rows(x, k, payload=None):
    n = x.shape[-2]
    ri = _iota(x.shape, x.ndim - 2)
    vals, picks = [], []
    for _ in range(k):
        m = jnp.max(x, axis=-2, keepdims=True)
        i = jnp.min(jnp.where(x == m, ri, n), axis=-2, keepdims=True)
        hit = ri == i
        vals.append(m)
        picks.append(i if payload is None else jnp.max(jnp.where(hit, payload, -1), axis=-2, keepdims=True))
        x = jnp.where(hit, -jnp.inf, x)
    return jnp.concatenate(vals, axis=-2), jnp.concatenate(picks, axis=-2)


def _route_kernel(om_ref, h1_ref, wo_ref, g_ref, wq_ref, sk_ref, h2_ref, xn_ref, idx_ref, gw_ref,
                  idx_scr, gw_scr):
    tm = h1_ref.shape[0]
    h2 = h1_ref[...] + _dot(om_ref[...], wo_ref[...])
    h2_ref[...] = h2
    xn = _rms(h2, g_ref[...])
    xn_ref[...] = xn
    xb = xn.astype(BF16)

    k = PEER_TOPK
    n_b = [k // (a + 1) for a in range(k)]
    pad = -sum(n_b) % 8
    lanes = min(tm, LANES)

    def head(h, _):
        pq = _dot(xb, wq_ref[h]).astype(BF16)
        s_all = _dot_nt(sk_ref[h], pq)
        rows = pl.ds(pl.multiple_of(h * k, k), k)
        for c in range(tm // lanes):
            s = s_all[:, c * lanes:(c + 1) * lanes].reshape(2, PEER_KEYS, lanes)
            v12, i12 = _top_rows(s, k)
            cand = jnp.concatenate([v12[0][a:a + 1] + v12[1][0:n_b[a]] for a in range(k)]
                                   + [jnp.full((pad, lanes), -jnp.inf, F32)], axis=0)
            cidx = jnp.concatenate([i12[0][a:a + 1] * PEER_KEYS + i12[1][0:n_b[a]] for a in range(k)]
                                   + [jnp.full((pad, lanes), -1, jnp.int32)], axis=0)
            top, expert = _top_rows(cand, k, payload=cidx)
            e = jnp.exp(top - top[0:1, :])
            idx_scr[rows, c * lanes:(c + 1) * lanes] = expert * PEER_ROW
            gw_scr[rows, c * lanes:(c + 1) * lanes] = e / jnp.sum(e, axis=0, keepdims=True)
        return 0

    lax.fori_loop(0, PEER_HEADS, head, 0)
    idx_ref[...] = idx_scr[...].T
    gw_ref[...] = gw_scr[...].T


def _route_call(omem, h1, w_mo, g_ffn, wq_h, sk_pad, tm):
    n = h1.shape[0]
    nk = PEER_HEADS * PEER_TOPK
    row = lambda w: pl.BlockSpec((tm, w), lambda i: (i, 0))
    return pl.pallas_call(
        _route_kernel,
        grid=(n // tm,),
        in_specs=[row(D_MEM), row(D_MODEL), _full((D_MEM, D_MODEL)), _full((1, D_MODEL)),
                  _full((PEER_HEADS, D_MODEL, PEER_DKEY)), _full((PEER_HEADS, 2 * PEER_KEYS, PEER_DKEY))],
        out_specs=[row(D_MODEL), row(D_MODEL), row(nk), row(nk)],
        out_shape=[jax.ShapeDtypeStruct((n, D_MODEL), F32), jax.ShapeDtypeStruct((n, D_MODEL), F32),
                   jax.ShapeDtypeStruct((n, nk), jnp.int32), jax.ShapeDtypeStruct((n, nk), F32)],
        scratch_shapes=[pltpu.VMEM((nk, tm), jnp.int32), pltpu.VMEM((nk, tm), F32)],
        compiler_params=_cparams(("parallel",)),
        name="peer_route",
    )(omem, h1, w_mo, g_ffn.reshape(1, D_MODEL), wq_h, sk_pad)


def _gelu_tanh(x):
    return 0.5 * x * (1.0 + jnp.tanh(0.7978845608028654 * (x + 0.044715 * x * x * x)))


PEER_TT = 64


def _pack_table(t):
    e, d = t.shape
    b = lax.bitcast_convert_type(t.astype(BF16), jnp.uint16).astype(jnp.uint32)
    w = b[:, :d // 2] | (b[:, d // 2:] << 16)
    return w.reshape(e * PEER_ROW, LANES)


def _expert_row(tab_ref, off):
    w = tab_ref[pl.ds(pl.multiple_of(off, PEER_ROW), PEER_ROW), :]
    lo = pltpu.bitcast(w << 16, F32)
    hi = pltpu.bitcast(w & jnp.uint32(0xFFFF0000), F32)
    return lo, hi


def _peer_act_kernel(idx_ref, x_ref, gw_ref, tab_ref, c_ref, part_scr, act_scr):
    tt, nk = gw_ref.shape
    sub = _iota((8, LANES), 0)
    keep_pairs = (sub % 4) < 2
    keep_even = (sub % 2) == 0
    feed = (0, 4, 2, 6, 1, 5, 3, 7)

    def row_sums8(p):
        p = [p[i] for i in feed]
        v = [jnp.concatenate([p[2 * i], p[2 * i + 1]], axis=0) for i in range(4)]
        w = [x + pltpu.roll(x, 6, axis=0) for x in v]
        u = [jnp.where(keep_pairs, w[2 * i], pltpu.roll(w[2 * i + 1], 2, axis=0)) for i in range(2)]
        z = [x + pltpu.roll(x, 7, axis=0) for x in u]
        return jnp.where(keep_even, z[0], pltpu.roll(z[1], 1, axis=0))

    def lane_sums(t):
        act_scr[pl.ds(t, 1), :] = jnp.sum(part_scr[t].T, axis=0, keepdims=True)

    part_scr[0] = jnp.zeros((nk, LANES), F32)

    def token(t, _):
        lane_sums(jnp.maximum(t - 1, 0))
        xt = x_ref[t]
        xlo, xhi = xt[0:PEER_ROW], xt[PEER_ROW:]
        for j0 in range(0, nk, 8):
            prods = []
            for j in range(j0, j0 + 8):
                lo, hi = _expert_row(tab_ref, idx_ref[t * nk + j])
                prods.append(lo * xlo + hi * xhi)
            part_scr[t, j0:j0 + 8, :] = row_sums8(prods)
        return 0

    lax.fori_loop(0, tt, token, 0)
    lane_sums(tt - 1)
    c_ref[...] = gw_ref[...] * _gelu_tanh(act_scr[...])


def _peer_out_kernel(idx_ref, c_ref, h2_ref, gf_ref, tab_ref, y_ref, splat_scr):
    tt, nk = c_ref.shape
    n_acc = 8

    def splat(buf, t):
        splat_scr[buf] = jnp.broadcast_to(c_ref[pl.ds(t, 1), :], (nk, nk)).T

    def accumulate(buf, t):
        acc_lo = [jnp.zeros((PEER_ROW, LANES), F32)] * n_acc
        acc_hi = [jnp.zeros((PEER_ROW, LANES), F32)] * n_acc
        for j in range(nk):
            lo, hi = _expert_row(tab_ref, idx_ref[t * nk + j])
            c = splat_scr[buf, j:j + 1, :]
            acc_lo[j % n_acc] = acc_lo[j % n_acc] + c * lo
            acc_hi[j % n_acc] = acc_hi[j % n_acc] + c * hi
        tree = lambda v: v[0] if len(v) == 1 else tree([a + b for a, b in zip(v[0::2], v[1::2])])
        y_ref[t] = h2_ref[t] + jnp.concatenate([tree(acc_lo), tree(acc_hi)], axis=0)

    splat(0, 0)

    def token_pair(i, _):
        t = 2 * i
        splat(1, t + 1)
        accumulate(0, t)
        splat(0, jnp.minimum(t + 2, tt - 1))
        accumulate(1, t + 1)
        return 0

    lax.fori_loop(0, tt // 2, token_pair, 0)
    h3 = y_ref[...]
    ms = jnp.sum(jnp.sum(h3 * h3, axis=2, keepdims=True), axis=1, keepdims=True) * (1.0 / D_MODEL)
    y_ref[...] = h3 * lax.rsqrt(ms + EPS) * gf_ref[...]


def _peer_call(idx, xn, gw, h2, g_final, u_tab, v_tab, tt):
    n, nk = idx.shape
    assert n % tt == 0 and nk % 8 == 0 and tt % 2 == 0
    smem = lambda: pl.BlockSpec((tt * nk,), lambda i: (i,), memory_space=pltpu.SMEM)
    idx = idx.reshape(n * nk)
    tile = lambda: pl.BlockSpec((tt, 8, LANES), lambda i: (i, 0, 0))
    table = lambda t: pl.BlockSpec(t.shape, lambda i: (0, 0), pipeline_mode=pl.Buffered(1))
    as_tiles = lambda a: a.reshape(n, 8, LANES)
    c = pl.pallas_call(
        _peer_act_kernel,
        grid=(n // tt,),
        in_specs=[smem(), tile(), pl.BlockSpec((tt, nk), lambda i: (i, 0)), table(u_tab)],
        out_specs=pl.BlockSpec((tt, nk), lambda i: (i, 0)),
        out_shape=jax.ShapeDtypeStruct((n, nk), F32),
        scratch_shapes=[pltpu.VMEM((tt, nk, LANES), F32), pltpu.VMEM((tt, nk), F32)],
        compiler_params=_cparams(("arbitrary",)),
        name="peer_act",
    )(idx, as_tiles(xn), gw, u_tab)
    y = pl.pallas_call(
        _peer_out_kernel,
        grid=(n // tt,),
        in_specs=[smem(), pl.BlockSpec((tt, nk), lambda i: (i, 0)), tile(), _full((8, LANES)), table(v_tab)],
        out_specs=tile(),
        out_shape=jax.ShapeDtypeStruct((n, 8, LANES), F32),
        scratch_shapes=[pltpu.VMEM((2, nk, nk), F32)],
        compiler_params=_cparams(("arbitrary",)),
        name="peer_out",
    )(idx, c, as_tiles(h2), g_final.reshape(8, LANES), v_tab)
    return y.reshape(n, D_MODEL)


CMP_PAGES = 128


def _cmp_pages_kernel(x_hbm, w_ref, b_ref, o_ref, buf, sem):
    i = pl.program_id(0)
    n_pages = o_ref.shape[0]

    def start(step, slot):
        def body(d, _):
            for g in range(G_NSA):
                pltpu.make_async_copy(x_hbm.at[pl.ds(step * n_pages, n_pages), g, d, :], buf.at[slot, d, g],
                                      sem.at[slot]).start()
            return 0
        lax.fori_loop(0, HD_NSA, body, 0)

    @pl.when(i == 0)
    def _():
        start(0, 0)

    @pl.when(i + 1 < pl.num_programs(0))
    def _():
        start(i + 1, (i + 1) % 2)

    slot = i % 2
    pltpu.make_async_copy(buf.at[slot], buf.at[slot], sem.at[slot]).wait()
    n_chain = 4

    def body(k, accs):
        out = []
        for g in range(G_NSA):
            for c in range(n_chain):
                d = k * n_chain + c
                out.append(accs[g * n_chain + c] + _dot(buf[slot, d, g].astype(BF16), w_ref[d]))
        return tuple(out)

    zero = jnp.zeros((n_pages, LANES), F32)
    accs = lax.fori_loop(0, HD_NSA // n_chain, body, (zero,) * (G_NSA * n_chain))
    for g in range(G_NSA):
        acc = accs[g * n_chain]
        for c in range(1, n_chain):
            acc = acc + accs[g * n_chain + c]
        o_ref[:, g * LANES:(g + 1) * LANES] = acc + b_ref[...]


def _cmp_pages_call(pool_t, w_bd, b):
    n_phys = pool_t.shape[0]
    assert n_phys % CMP_PAGES == 0
    out = pl.pallas_call(
        _cmp_pages_kernel,
        grid=(n_phys // CMP_PAGES,),
        in_specs=[pl.BlockSpec(memory_space=pl.ANY), _full((HD_NSA, PAGE_SIZE, LANES)), _full((1, LANES))],
        out_specs=pl.BlockSpec((CMP_PAGES, G_NSA * LANES), lambda i: (i, 0)),
        out_shape=jax.ShapeDtypeStruct((n_phys, G_NSA * LANES), F32),
        scratch_shapes=[pltpu.VMEM((2, HD_NSA, G_NSA, CMP_PAGES, PAGE_SIZE), F32), pltpu.SemaphoreType.DMA((2,))],
        compiler_params=_cparams(("arbitrary",)),
        name="compress_pages",
    )(pool_t, w_bd, jnp.tile(b, 2).reshape(1, LANES))
    return out.reshape(n_phys * G_NSA, LANES)


def _nsa1_cmp_kernel(pt_ref, nq_ref, ckn_ref, cvn_ref, tk_ref, tv_ref, cw_ref, cb_ref, ocmp_ref, idx_ref,
                     kg_scr, vg_scr, *, n_pages, q_pos):
    tb = nq_ref.shape[0]
    nb_past = n_pages * (PAGE_SIZE // CMP_BLK)
    base = pl.program_id(0) * tb
    lane = _iota((1, LANES), 1)
    blk_n = 2 * (lane % HD_NSA) + lane // HD_NSA
    forced = (blk_n == 0) | (blk_n == q_pos // CMP_BLK)
    vis = (blk_n * CMP_BLK + CMP_BLK - 1) <= q_pos
    new_vis = (nb_past * CMP_BLK + CMP_BLK - 1) <= q_pos
    new_forced = nb_past == q_pos // CMP_BLK
    n_row = jnp.broadcast_to(blk_n, (LANES, LANES))
    n_col = 2 * (_iota((LANES, LANES), 0) % HD_NSA) + _iota((LANES, LANES), 0) // HD_NSA
    lane8 = _iota((1, LANES), 1) // HD_NSA

    def sample(b, _):
        qrow = nq_ref[pl.ds(b, 1), :].astype(F32)
        new_k = _dot(ckn_ref[pl.ds(b, 1), :].astype(BF16), cw_ref[0]) + cb_ref[0]
        new_v = _dot(cvn_ref[pl.ds(b, 1), :].astype(BF16), cw_ref[1]) + cb_ref[1]
        o_row, idx_row = [], jnp.full((1, LANES), -1, jnp.int32)
        for g in range(G_NSA):
            def gather(i, _):
                r = pt_ref[base + b, i] * G_NSA + g
                kg_scr[pl.ds(i, 1), :] = tk_ref[pl.ds(r, 1), :]
                vg_scr[pl.ds(i, 1), :] = tv_ref[pl.ds(r, 1), :]
                return 0
            lax.fori_loop(0, n_pages, gather, 0)
            rows = []
            for c in range(2):
                for r in range(R_NSA):
                    h = g * R_NSA + r
                    x = qrow[:, (h // 2) * LANES:(h // 2 + 1) * LANES]
                    if h % 2 != c:
                        x = pltpu.roll(x, HD_NSA, axis=1)
                    rows.append(jnp.where(lane8 == c, x, 0.0))
            qpad = jnp.concatenate(rows, axis=0)
            s = _dot_nt(qpad.astype(BF16), kg_scr[...].astype(BF16)) * (HD_NSA ** -0.5)
            s3 = s.reshape(2, R_NSA, n_pages)
            qg = qpad[R_NSA * g:R_NSA * (g + 1), :]
            nk_g = jnp.where(lane8 == g, new_k.astype(BF16).astype(F32), 0.0)
            s_new = jnp.sum(qg.astype(BF16).astype(F32) * nk_g, axis=-1, keepdims=True) * (HD_NSA ** -0.5)
            s_new = jnp.where(new_vis, s_new, NEG)[None]
            m = jnp.maximum(jnp.max(jnp.max(s3, axis=2, keepdims=True), axis=0, keepdims=True), s_new)
            p = jnp.exp(s3 - m)
            p_new = jnp.where(s_new > 0.5 * NEG, jnp.exp(s_new - m), 0.0)
            l = jnp.sum(jnp.sum(p, axis=2, keepdims=True), axis=0, keepdims=True) + p_new
            inv = 1.0 / jnp.maximum(l, 1e-30)
            p = p * inv
            p_new = p_new * inv
            res = _dot(p.reshape(2 * R_NSA, n_pages).astype(BF16), vg_scr[...].astype(BF16))
            o4 = res[0:R_NSA] + pltpu.roll(res[R_NSA:], HD_NSA, axis=1)
            nv_g = new_v.astype(BF16).astype(F32)
            if g == 1:
                nv_g = pltpu.roll(nv_g, HD_NSA, axis=1)
            o4 = o4 + p_new[0].astype(BF16).astype(F32) * nv_g
            o_row += [o4[r:r + 1, 0:HD_NSA] for r in range(R_NSA)]
            imp2 = jnp.sum(p, axis=1)
            imp = jnp.concatenate([imp2[0:1], imp2[1:2]], axis=-1)
            score = jnp.where(forced, SEL_FORCE, jnp.where(vis, imp, -1.0))
            imp_new = jnp.sum(p_new)
            sc_new = SEL_FORCE if new_forced else jnp.where(new_vis, imp_new, -1.0)
            a = jnp.broadcast_to(score, (LANES, LANES))
            bt = a.T
            ahead = (bt > a) | ((bt == a) & (n_col < n_row))
            rank = jnp.sum(jnp.where(ahead, 1.0, 0.0), axis=0, keepdims=True) + jnp.where(sc_new > score, 1.0, 0.0)
            rank_new = jnp.sum(jnp.where(score >= sc_new, 1.0, 0.0))
            for r in range(N_SEL):
                hit = (rank == r) & (score >= 0.0)
                val = jnp.sum(jnp.where(hit, blk_n + 1, 0)) - 1
                val = jnp.where((rank_new == r) & (sc_new >= 0.0), nb_past, val)
                idx_row = jnp.where(lane == g * N_SEL + r, val, idx_row)
        ocmp_ref[pl.ds(b, 1), :] = jnp.concatenate(o_row, axis=-1)
        idx_ref[pl.ds(b, 1), :] = idx_row
        return 0

    lax.fori_loop(0, tb, sample, 0)


def _nsa1_cmp_call(page_table, nq, ck_new, cv_new, tbl_k, tbl_v, cw_bd, cb2, q_pos):
    DB, n_pages = page_table.shape
    assert n_pages * (PAGE_SIZE // CMP_BLK) == LANES
    tb = 8
    row = lambda w: pl.BlockSpec((tb, w), lambda i, pt: (i, 0))
    whole = lambda a: pl.BlockSpec(a.shape, lambda i, pt: (0,) * a.ndim, pipeline_mode=pl.Buffered(1))
    return pl.pallas_call(
        functools.partial(_nsa1_cmp_kernel, n_pages=n_pages, q_pos=q_pos),
        grid_spec=pltpu.PrefetchScalarGridSpec(
            num_scalar_prefetch=1,
            grid=(DB // tb,),
            in_specs=[row(D_NSA), row(LANES), row(LANES), whole(tbl_k), whole(tbl_v),
                      pl.BlockSpec((2, LANES, LANES), lambda i, pt: (0, 0, 0)),
                      pl.BlockSpec((2, 1, LANES), lambda i, pt: (0, 0, 0))],
            out_specs=[row(D_NSA), row(LANES)],
            scratch_shapes=[pltpu.VMEM((n_pages, LANES), F32), pltpu.VMEM((n_pages, LANES), F32)]),
        out_shape=[jax.ShapeDtypeStruct((DB, D_NSA), F32), jax.ShapeDtypeStruct((DB, LANES), jnp.int32)],
        compiler_params=_cparams(("arbitrary",)),
        name="nsa_step_compressed",
    )(page_table, nq.astype(F32), ck_new, cv_new, tbl_k, tbl_v, cw_bd[:, 0], cb2)


def _nsa1_attn_kernel(idx_ref, pt_ref, nq_ref, kvn_ref, gt_ref, ocmp_ref, wk_ref, wv_ref, sk_hbm, sv_hbm,
                      o_ref, kbuf, vbuf, sem, *, n_pages, q_pos, past_len):
    b = pl.program_id(0)
    bpp = PAGE_SIZE // CMP_BLK
    nb_past = n_pages * bpp
    wb = wk_ref.shape[3]

    def block_copies(bb, slot, g, r):
        n = idx_ref[bb, g * N_SEL + r]
        past = (n >= 0) & (n < nb_past)
        page = pt_ref[bb, jnp.clip(n, 0, nb_past - 1) // bpp]
        ck = pltpu.make_async_copy(sk_hbm.at[page, g], kbuf.at[slot, g, r], sem.at[slot, 0])
        cv = pltpu.make_async_copy(sv_hbm.at[page, g], vbuf.at[slot, g, r], sem.at[slot, 1])
        return past, ck, cv

    def issue(bb, slot):
        for g in range(G_NSA):
            for r in range(N_SEL):
                past, ck, cv = block_copies(bb, slot, g, r)

                @pl.when(past)
                def _():
                    ck.start()
                    cv.start()

                @pl.when(jnp.logical_not(past))
                def _():
                    kbuf[slot, g, r] = jnp.zeros((HD_NSA, PAGE_SIZE), F32)
                    vbuf[slot, g, r] = jnp.zeros((HD_NSA, PAGE_SIZE), F32)

    @pl.when(b == 0)
    def _():
        issue(0, 0)

    @pl.when(b + 1 < pl.num_programs(0))
    def _():
        issue(b + 1, (b + 1) % 2)

    slot = b % 2
    for g in range(G_NSA):
        for r in range(N_SEL):
            past, ck, cv = block_copies(b, slot, g, r)

            @pl.when(past)
            def _():
                ck.wait()
                cv.wait()

    qrow = nq_ref[0].astype(F32)
    kvn = kvn_ref[0]
    gt = gt_ref[0]
    ocmp = ocmp_ref[0]
    scale = HD_NSA ** -0.5
    blk_in_page = _iota((1, PAGE_SIZE), 1) // CMP_BLK
    wpos = past_len - wb + _iota((1, wb), 1)
    wdist = q_pos - wpos
    w_ok = (wdist >= 0) & (wdist < WINDOW) & (wpos >= 0)
    bfr = lambda t: t.astype(BF16).astype(F32)
    pieces = []
    for g in range(G_NSA):
        q4 = jnp.concatenate([qrow[:, (g * R_NSA + r) * HD_NSA:(g * R_NSA + r + 1) * HD_NSA]
                              for r in range(R_NSA)], axis=0)
        q4b = q4.astype(BF16)
        new = lambda i: kvn[:, i * D_KV + g * HD_NSA:i * D_KV + (g + 1) * HD_NSA]
        scores, keeps = [], []
        has_new = jnp.int32(0)
        for r in range(N_SEL):
            n = idx_ref[b, g * N_SEL + r]
            past = ((n >= 0) & (n < nb_past)).astype(jnp.int32)
            keep = (blk_in_page == n % bpp) & (past > 0)
            has_new = has_new | (n == nb_past).astype(jnp.int32)
            keeps.append(keep)
            scores.append(jnp.where(keep, _dot(q4b, kbuf[slot, g, r].astype(BF16)) * scale, NEG))
        s_new = jnp.sum(bfr(q4) * bfr(new(2)), axis=-1, keepdims=True) * scale
        s_new = jnp.where(has_new > 0, s_new, NEG)
        m = s_new
        for s in scores:
            m = jnp.maximum(m, jnp.max(s, axis=-1, keepdims=True))
        p_new = jnp.where(s_new > 0.5 * NEG, jnp.exp(s_new - m), 0.0)
        l = p_new
        acc = bfr(p_new) * bfr(new(3))
        for r in range(N_SEL):
            p = jnp.where(scores[r] > 0.5 * NEG, jnp.exp(scores[r] - m), 0.0)
            l = l + jnp.sum(p, axis=-1, keepdims=True)
            v_t = jnp.where(keeps[r], vbuf[slot, g, r], 0.0).astype(BF16)
            acc = acc + _dot_nt(p.astype(BF16), v_t)
        o_slc = acc / jnp.maximum(l, 1e-30)
        wk = wk_ref[0, g].astype(BF16)
        wv = wv_ref[0, g].astype(BF16)
        s = jnp.where(w_ok, _dot(q4b, wk) * scale, NEG)
        s_new = jnp.sum(bfr(q4) * bfr(new(4)), axis=-1, keepdims=True) * scale
        m = jnp.maximum(jnp.max(s, axis=-1, keepdims=True), s_new)
        p = jnp.where(s > 0.5 * NEG, jnp.exp(s - m), 0.0)
        p_new = jnp.exp(s_new - m)
        l = jnp.sum(p, axis=-1, keepdims=True) + p_new
        o_win = (_dot_nt(p.astype(BF16), wv) + bfr(p_new) * bfr(new(5))) / jnp.maximum(l, 1e-30)
        for r in range(R_NSA):
            h = g * R_NSA + r
            pieces.append(gt[:, 3 * h:3 * h + 1] * ocmp[:, h * HD_NSA:(h + 1) * HD_NSA]
                          + gt[:, 3 * h + 1:3 * h + 2] * o_slc[r:r + 1]
                          + gt[:, 3 * h + 2:3 * h + 3] * o_win[r:r + 1])
    o_ref[0] = jnp.concatenate(pieces, axis=-1)


def _nsa1_attn_call(idx, page_table, nq, kv_new, gates, ocmp, win_k, win_v, slc_k, slc_v, q_pos, past_len):
    DB, n_pages = page_table.shape
    wb = win_k.shape[3]
    r3 = lambda a: a.reshape(DB, 1, a.shape[-1])
    row = lambda w: pl.BlockSpec((1, 1, w), lambda b, *_: (b, 0, 0))
    win = pl.BlockSpec((1, G_NSA, HD_NSA, wb), lambda b, *_: (b, 0, 0, 0))
    o = pl.pallas_call(
        functools.partial(_nsa1_attn_kernel, n_pages=n_pages, q_pos=q_pos, past_len=past_len),
        grid_spec=pltpu.PrefetchScalarGridSpec(
            num_scalar_prefetch=2,
            grid=(DB,),
            in_specs=[row(D_NSA), row(6 * D_KV), row(LANES), row(D_NSA), win, win,
                      pl.BlockSpec(memory_space=pl.ANY), pl.BlockSpec(memory_space=pl.ANY)],
            out_specs=row(D_NSA),
            scratch_shapes=[pltpu.VMEM((2, G_NSA, N_SEL, HD_NSA, PAGE_SIZE), F32),
                            pltpu.VMEM((2, G_NSA, N_SEL, HD_NSA, PAGE_SIZE), F32),
                            pltpu.SemaphoreType.DMA((2, 2))]),
        out_shape=jax.ShapeDtypeStruct((DB, 1, D_NSA), F32),
        compiler_params=_cparams(("arbitrary",)),
        name="nsa_step_attend",
    )(idx, page_table, r3(nq), r3(kv_new), r3(gates), r3(ocmp), win_k, win_v, slc_k, slc_v)
    return o.reshape(DB, D_NSA).astype(BF16)


def _token_tail(yret, onsa, h, mem_attend, lw, tm):
    w_out, g_mem, w_mq, w_mo, g_ffn, wq_h, sk_pad, u, v, g_final = lw
    h1, mq = _mixout_call(yret, onsa, h, w_out, g_mem, w_mq, tm)
    omem = mem_attend(mq)
    h2, xn, idx, gw = _route_call(omem, h1, w_mo, g_ffn, wq_h, sk_pad, tm)
    return _peer_call(idx, xn, gw, h2, g_final, u, v, min(PEER_TT, h.shape[0]))


def _all_tables(pos):
    return (_rot_tables(pos, H_RET, HD_RET, HD_RET, RET_THETA)
            + _rot_tables(pos, H_NSA, HD_NSA, ROPE_DIMS, ROPE_THETA)
            + _rot_tables(pos, G_NSA, HD_NSA, ROPE_DIMS, ROPE_THETA))


def kernel(x_prompt, x_sample, mem_prompt, state_ret, cache_cmp_k, cache_cmp_v, cache_slc_k, cache_slc_v,
           cache_win_k, cache_win_v, cache_mem_k, cache_mem_v, page_table, norm_mix_g, w_in, ret_gn_g,
           cmp_w, cmp_b, w_out, norm_mem_g, mem_norm_g, w_mq, w_mk, w_mv, w_mo, norm_ffn_g,
           peer_wq, peer_subkeys, peer_u, peer_v, norm_final_g):
    B, L, D = x_prompt.shape
    DB, LS, _ = x_sample.shape
    n_mem = mem_prompt.shape[1]
    n_pages = page_table.shape[1]
    past_len = n_pages * PAGE_SIZE
    assert w_in.shape[0] == 1 and LS == 1 and D == D_MODEL
    l = 0
    tm = 256

    w_all = _prep_w_in(w_in[l])
    cw_bd = _prep_cmp_w(cmp_w[l])
    cb2 = jnp.tile(cmp_b[l], (1, G_NSA)).reshape(2, 1, LANES)
    w_kv = jnp.concatenate([w_mk[l], w_mv[l]], axis=1).astype(BF16)
    wq_h = peer_wq[l].reshape(D, PEER_HEADS, PEER_DKEY).transpose(1, 0, 2).astype(BF16)
    lw = (w_out[l].astype(BF16), norm_mem_g[l], w_mq[l].astype(BF16), w_mo[l].astype(BF16), norm_ffn_g[l],
          wq_h, _prep_peer_keys(peer_subkeys[l]), _pack_table(peer_u[l]), _pack_table(peer_v[l]), norm_final_g)

    xs = x_sample.reshape(DB, D)
    rq, rk, rv, rg, nq, kvs, _, gt = _proj_call(xs, norm_mix_g[l], w_all,
                                                _all_tables(jnp.full((DB,), past_len, jnp.int32)), DB)
    yret, s_state = _ret1_call(rq, rk, rv, rg, ret_gn_g[l], state_ret[l])
    fm = lambda cache: jnp.swapaxes(cache[l], -1, -2)
    cw_pg = _prep_cmp_w(cmp_w[l].transpose(0, 2, 1, 3))
    tbl_k = _cmp_pages_call(fm(cache_cmp_k), cw_pg[0], cmp_b[l, 0])
    tbl_v = _cmp_pages_call(fm(cache_cmp_v), cw_pg[1], cmp_b[l, 1])
    ocmp, sel_idx = _nsa1_cmp_call(page_table, nq, kvs[:, 0:D_KV], kvs[:, D_KV:2 * D_KV], tbl_k, tbl_v, cw_bd, cb2,
                                   past_len)
    onsa = _nsa1_attn_call(sel_idx, page_table, nq, kvs, gt, ocmp, fm(cache_win_k), fm(cache_win_v),
                           fm(cache_slc_k), fm(cache_slc_v), past_len, past_len)
    y_s = _token_tail(yret, onsa, xs, lambda mq: _memattn1_call(mq, cache_mem_k[l], cache_mem_v[l]), lw, DB)

    new = lambda i: kvs[:, i * D_KV:(i + 1) * D_KV].reshape(DB, G_NSA, 1, HD_NSA)
    wb = cache_win_k.shape[3]
    keep_s = min(WINDOW, wb + 1)
    win = lambda cache, i: jnp.concatenate([cache[l], new(i)], axis=2)[:, :, wb + 1 - keep_s:]

    xp = x_prompt.reshape(B * L, D)
    rq, rk, rv, rg, nq, kv, kvb, gt = _proj_call(xp, norm_mix_g[l], w_all,
                                                 _all_tables(jnp.arange(L, dtype=jnp.int32)), tm)
    yret, p_state = _ret_call(rq, rk, rv, rg, ret_gn_g[l], B, L)
    onsa = _nsa_call(nq, kv, kvb, gt, cw_bd, cb2, B, L)
    mkv, mkvb = _normmm_call(mem_prompt.reshape(B * n_mem, D), mem_norm_g[l], w_kv, tm)
    y_p = _token_tail(yret, onsa, xp, lambda mq: _memattn_call(mq, mkvb, B, L, n_mem, tm), lw, tm)

    kv6 = kv.reshape(B, L, 6, G_NSA, HD_NSA)
    pages = lambda i: kv6[:, :, i].reshape(B, L // PAGE_SIZE, PAGE_SIZE, G_NSA, HD_NSA).transpose(0, 1, 3, 2, 4)
    keep = min(WINDOW, L)
    tail = lambda i: kv6[:, L - keep:, i].transpose(0, 2, 1, 3)
    mem4 = lambda t: t.reshape(B, n_mem, H_MEM, HD_MEM)
    st = lambda t: t[None]
    return (y_p.reshape(B, L, D), y_s.reshape(DB, 1, D), st(p_state),
            st(pages(0)), st(pages(1)), st(pages(2)), st(pages(3)), st(tail(4)), st(tail(5)),
            st(mem4(mkv[:, :D_MEM])), st(mem4(mkv[:, D_MEM:])), st(s_state),
            st(new(0)), st(new(1)), st(new(2)), st(new(3)), st(win(cache_win_k, 4)), st(win(cache_win_v, 5)))
```

```python
import functools

import numpy as np
import jax
import jax.numpy as jnp
from jax import lax
from jax.experimental import pallas as pl
from jax.experimental.pallas import tpu as pltpu

F32 = jnp.float32
BF16 = jnp.bfloat16

D_MODEL = 1024
PAGE_SIZE = 128
H_RET = 8
HD_RET = 64
RET_CHUNK = 128
RET_THETA = 10000.0
H_NSA = 8
G_NSA = 2
R_NSA = H_NSA // G_NSA
HD_NSA = 64
CMP_BLK = 64
N_SEL = 16
WINDOW = 512
SEL_FORCE = 1.0e4
ROPE_THETA = 500000.0
ROPE_DIMS = HD_NSA // 4
H_MEM = 4
HD_MEM = 128
PEER_KEYS = 128
PEER_HEADS = 8
PEER_DKEY = 128
PEER_TOPK = 16
PEER_ROW = 4
EPS = 1e-6

D_RET = H_RET * HD_RET
D_NSA = H_NSA * HD_NSA
D_KV = G_NSA * HD_NSA
D_MEM = H_MEM * HD_MEM
NEG = -1.0e30

LANES = 128
VMEM_LIMIT = 56 * 1024 * 1024

C_RQ, C_RQR, C_RK, C_RKR, C_RV, C_RG, C_NQ, C_NQR = (i * 512 for i in range(8))
C_KV = 4096
C_KVR = C_KV + 6 * D_KV
C_NG = C_KVR + 3 * D_KV
N_PROJ = C_NG + LANES


def _cparams(sem):
    return pltpu.CompilerParams(dimension_semantics=sem, vmem_limit_bytes=VMEM_LIMIT)


def _rms(x, g):
    return x * lax.rsqrt(jnp.mean(x * x, axis=-1, keepdims=True) + EPS) * g


def _dot(a, b):
    return jnp.dot(a, b, preferred_element_type=F32)


def _dot_nt(a, b):
    return lax.dot_general(a, b, (((1,), (1,)), ((), ())), preferred_element_type=F32)


def _full(shape):
    n = len(shape)
    return pl.BlockSpec(shape, lambda *_: (0,) * n)


def _partner_cols(n_heads, hd, n_rot):
    half = n_rot // 2
    j = np.arange(hd)
    p = np.where(j < half, j + half, np.where(j < n_rot, j - half, j))
    return (np.arange(n_heads)[:, None] * hd + p[None, :]).reshape(-1)


def _prep_w_in(w_in):
    o = 0
    seg = {}
    for name, size in (("rq", D_RET), ("rk", D_RET), ("rv", D_RET), ("rg", D_RET), ("nq", D_NSA),
                       ("ck", D_KV), ("cv", D_KV), ("sk", D_KV), ("sv", D_KV), ("wk", D_KV), ("wv", D_KV),
                       ("ng", 3 * H_NSA)):
        seg[name] = (o, size)
        o += size
    cols = lambda n: np.arange(seg[n][0], seg[n][0] + seg[n][1])
    pr = _partner_cols(H_RET, HD_RET, HD_RET)
    pn = _partner_cols(H_NSA, HD_NSA, ROPE_DIMS)
    pk = _partner_cols(G_NSA, HD_NSA, ROPE_DIMS)
    order = np.concatenate([
        cols("rq"), cols("rq")[pr], cols("rk"), cols("rk")[pr], cols("rv"), cols("rg"),
        cols("nq"), cols("nq")[pn],
        cols("ck"), cols("cv"), cols("sk"), cols("sv"), cols("wk"), cols("wv"),
        cols("ck")[pk], cols("sk")[pk], cols("wk")[pk], cols("ng")])
    w = jnp.take(w_in, jnp.asarray(order, jnp.int32), axis=1)
    w = jnp.pad(w, ((0, 0), (0, N_PROJ - w.shape[1])))
    return w.astype(BF16)


def _rot_tables(pos, n_heads, hd, n_rot, theta):
    half = n_rot // 2
    inv = theta ** (-jnp.arange(half, dtype=F32) / half)
    ang = pos.astype(F32)[:, None] * inv[None, :]
    cos, sin = jnp.cos(ang), jnp.sin(ang)
    P = pos.shape[0]
    c = jnp.concatenate([cos, cos, jnp.ones((P, hd - n_rot), F32)], axis=1)
    s = jnp.concatenate([-sin, sin, jnp.zeros((P, hd - n_rot), F32)], axis=1)
    return jnp.tile(c, (1, n_heads)), jnp.tile(s, (1, n_heads))


def _proj_kernel(x_ref, g_ref, w_ref, cr_ref, sr_ref, cn_ref, sn_ref, ck_ref, sk_ref,
                 rq_ref, rk_ref, rv_ref, rg_ref, nq_ref, kv_ref, kvb_ref, gt_ref):
    hn = _rms(x_ref[...], g_ref[...]).astype(BF16)
    seg = lambda c0, n: _dot(hn, w_ref[:, c0:c0 + n])
    cr, sr = cr_ref[...], sr_ref[...]
    rq_ref[...] = (seg(C_RQ, 512) * cr + seg(C_RQR, 512) * sr).astype(BF16)
    rk_ref[...] = ((seg(C_RK, 512) * cr + seg(C_RKR, 512) * sr) * (HD_RET ** -0.5)).astype(BF16)
    rv_ref[...] = seg(C_RV, 512).astype(BF16)
    rg_ref[...] = seg(C_RG, 512)
    nq_ref[...] = (seg(C_NQ, 512) * cn_ref[...] + seg(C_NQR, 512) * sn_ref[...]).astype(BF16)
    ck, sk = ck_ref[...], sk_ref[...]
    for i in range(6):
        a = seg(C_KV + i * D_KV, D_KV)
        if i % 2 == 0:
            a = a * ck + seg(C_KVR + (i // 2) * D_KV, D_KV) * sk
        kv_ref[:, i * D_KV:(i + 1) * D_KV] = a
        kvb_ref[:, i * D_KV:(i + 1) * D_KV] = a.astype(BF16)
    z = seg(C_NG, LANES)
    gt_ref[...] = 1.0 / (1.0 + jnp.exp(-z))


def _proj_call(x2d, g, w_all, tabs, tm):
    n = x2d.shape[0]
    period = tabs[0].shape[0] // tm
    row = lambda w: pl.BlockSpec((tm, w), lambda i: (i, 0))
    tab = lambda w: pl.BlockSpec((tm, w), lambda i: (i % period, 0))
    outs = [(512, BF16), (512, BF16), (512, BF16), (512, F32), (512, BF16), (6 * D_KV, F32), (6 * D_KV, BF16),
            (LANES, F32)]
    return pl.pallas_call(
        _proj_kernel,
        grid=(n // tm,),
        in_specs=[row(D_MODEL), _full((1, D_MODEL)), _full((D_MODEL, N_PROJ)),
                  tab(512), tab(512), tab(512), tab(512), tab(D_KV), tab(D_KV)],
        out_specs=[row(w) for w, _ in outs],
        out_shape=[jax.ShapeDtypeStruct((n, w), dt) for w, dt in outs],
        compiler_params=_cparams(("parallel",)),
        name="proj",
    )(x2d, g.reshape(1, D_MODEL), w_all, *tabs)


def _ret_consts(C):
    lg = jnp.log(1.0 - 2.0 ** (-5.0 - jnp.arange(H_RET, dtype=F32)))
    idx = jnp.arange(C, dtype=F32)
    diff = idx[:, None] - idx[None, :]
    dmat = jnp.where(diff >= 0, jnp.exp(lg[:, None, None] * jnp.maximum(diff, 0.0)), 0.0)
    xi = jnp.exp(lg[None, :] * (idx[:, None] + 1.0))
    zeta = jnp.exp(lg[:, None] * (C - 1.0 - idx[None, :]))
    g_c = jnp.exp(lg * C)
    return dmat, xi, zeta, g_c


def _ret_kernel(gc_ref, q_ref, k_ref, v_ref, rg_ref, gn_ref, dmat_ref, xi_ref, zeta_ref,
                y_ref, st_ref, s_scr):
    c = pl.program_id(1)

    @pl.when(c == 0)
    def _():
        s_scr[...] = jnp.zeros_like(s_scr)

    q, k, v = q_ref[...], k_ref[...], v_ref[...]
    k_t = k.astype(F32).T
    outs = []
    for h in range(H_RET):
        sl = slice(h * HD_RET, (h + 1) * HD_RET)
        qh, kh, vh = q[:, sl], k[:, sl], v[:, sl]
        att = _dot_nt(qh, kh) * dmat_ref[h]
        inner = _dot(att.astype(BF16), vh)
        s_old = s_scr[h]
        cross = _dot(qh, s_old.astype(BF16)) * xi_ref[:, h:h + 1]
        o = inner + cross
        kz = (k_t[sl, :] * zeta_ref[h:h + 1, :]).astype(BF16)
        s_scr[h] = s_old * gc_ref[h] + _dot(kz, vh)
        mu = jnp.mean(o, axis=-1, keepdims=True)
        d = o - mu
        var = jnp.mean(d * d, axis=-1, keepdims=True)
        outs.append(d * lax.rsqrt(var + EPS))
    on = jnp.concatenate(outs, axis=-1)
    rg = rg_ref[...]
    silu = rg * (1.0 / (1.0 + jnp.exp(-rg)))
    y_ref[...] = (silu * (on * gn_ref[...])).astype(BF16)

    @pl.when(c == pl.num_programs(1) - 1)
    def _():
        st_ref[0] = s_scr[...]


def _ret_call(rq, rk, rv, rg, gn, B, L):
    C = RET_CHUNK
    nC = L // C
    dmat, xi, zeta, g_c = _ret_consts(C)
    blk = lambda: pl.BlockSpec((C, D_RET), lambda b, c: (b * nC + c, 0))
    return pl.pallas_call(
        _ret_kernel,
        grid=(B, nC),
        in_specs=[pl.BlockSpec(memory_space=pltpu.SMEM), blk(), blk(), blk(), blk(), _full((1, D_RET)),
                  _full((H_RET, C, C)), _full((C, H_RET)), _full((H_RET, C))],
        out_specs=[blk(), pl.BlockSpec((1, H_RET, HD_RET, HD_RET), lambda b, c: (b, 0, 0, 0))],
        out_shape=[jax.ShapeDtypeStruct((B * L, D_RET), BF16),
                   jax.ShapeDtypeStruct((B, H_RET, HD_RET, HD_RET), F32)],
        scratch_shapes=[pltpu.VMEM((H_RET, HD_RET, HD_RET), F32)],
        compiler_params=_cparams(("parallel", "arbitrary")),
        name="retention",
    )(g_c, rq, rk, rv, rg, gn.reshape(1, D_RET), dmat, xi, zeta)


def _ret1_kernel(q_ref, k_ref, v_ref, s_ref, gam_ref, rg_ref, gn_ref, y_ref, so_ref):
    q, k, v, s, gam = q_ref[...], k_ref[...], v_ref[...], s_ref[...], gam_ref[...]
    qk = jnp.sum(q * k, axis=1, keepdims=True)
    cross = jnp.sum(q * s, axis=1, keepdims=True) * gam
    o = qk * v + cross
    so_ref[...] = s * gam + k * v
    mu = jnp.mean(o, axis=-1, keepdims=True)
    d = o - mu
    var = jnp.mean(d * d, axis=-1, keepdims=True)
    rg = rg_ref[...]
    silu = rg * (1.0 / (1.0 + jnp.exp(-rg)))
    y_ref[...] = silu * (d * lax.rsqrt(var + EPS) * gn_ref[...])


def _ret1_call(rq, rk, rv, rg, gn, state):
    DB = rq.shape[0]
    n = DB * H_RET
    _, _, _, g_c = _ret_consts(1)
    col = lambda t: t.astype(F32).reshape(n, HD_RET, 1)
    rowv = lambda t: t.astype(F32).reshape(n, 1, HD_RET)
    gam = jnp.tile(g_c, DB).reshape(n, 1, 1)
    gn3 = jnp.tile(gn.reshape(H_RET, 1, HD_RET), (DB, 1, 1))
    tb = 128
    b3 = lambda a, b: pl.BlockSpec((tb, a, b), lambda i: (i, 0, 0))
    y, s_new = pl.pallas_call(
        _ret1_kernel,
        grid=(n // tb,),
        in_specs=[b3(HD_RET, 1), b3(HD_RET, 1), b3(1, HD_RET), b3(HD_RET, HD_RET), b3(1, 1), b3(1, HD_RET),
                  b3(1, HD_RET)],
        out_specs=[b3(1, HD_RET), b3(HD_RET, HD_RET)],
        out_shape=[jax.ShapeDtypeStruct((n, 1, HD_RET), F32),
                   jax.ShapeDtypeStruct((n, HD_RET, HD_RET), F32)],
        compiler_params=_cparams(("parallel",)),
        name="retention_step",
    )(col(rq), col(rk), rowv(rv), state.astype(F32).reshape(n, HD_RET, HD_RET), gam, rowv(rg), gn3)
    return y.reshape(DB, D_RET).astype(BF16), s_new.reshape(DB, H_RET, HD_RET, HD_RET)


NSA_TQ = 128
NSA_TK = 512


def _prep_cmp_w(cmp_w):
    z = jnp.zeros_like(cmp_w)
    top = jnp.concatenate([cmp_w, z], axis=-1)
    bot = jnp.concatenate([z, cmp_w], axis=-1)
    return jnp.concatenate([top, bot], axis=-2).astype(BF16)


def _iota(shape, dim):
    return lax.broadcasted_iota(jnp.int32, shape, dim)


def _group_queries(nqf, g, tq):
    lane_g = _iota((tq, LANES), 1) // HD_NSA
    parts = []
    for r in range(R_NSA):
        h = g * R_NSA + r
        x = nqf[:, (h // 2) * LANES:(h // 2 + 1) * LANES]
        if h % 2 != g:
            x = pltpu.roll(x, HD_NSA, axis=1)
        parts.append((jnp.where(lane_g == g, x, 0.0) * (HD_NSA ** -0.5)).astype(BF16))
    return parts


def _select_blocks(score, nb, n_sel):
    s_t = score.T[:nb, :]
    n_i = _iota(s_t.shape, 0)
    rank = jnp.zeros(s_t.shape, F32)
    for m in range(nb):
        row = s_t[m:m + 1, :]
        ahead = (row > s_t) | ((row == s_t) & (n_i > m))
        rank = rank + jnp.where(ahead, 1.0, 0.0)
    sel_t = jnp.where((rank < n_sel) & (s_t >= 0.0), 1.0, 0.0)
    sel_t = jnp.concatenate([sel_t, jnp.zeros((LANES - nb, s_t.shape[1]), F32)], axis=0)
    return sel_t.T


def _flash_run(q_all, k_ref, v_ref, bias_ref, n_steps, tq, tk):
    def body(j, carry):
        m, l, acc = carry
        rows = pl.ds(pl.multiple_of(j * tk, tk), tk)
        s = _dot_nt(q_all, k_ref[rows, :]).reshape(G_NSA, R_NSA, tq, tk) + bias_ref[j][:, None]
        s = s.reshape(H_NSA, tq, tk)
        m_new = jnp.maximum(m, jnp.max(s, axis=-1, keepdims=True))
        alpha = jnp.exp(m - m_new)
        p = jnp.exp(s - m_new)
        l = alpha * l + jnp.sum(p, axis=-1, keepdims=True)
        pv = _dot(p.reshape(H_NSA * tq, tk).astype(BF16), v_ref[rows, :]).reshape(H_NSA, tq, LANES)
        return m_new, l, alpha * acc + pv

    init = (jnp.full((H_NSA, tq, 1), NEG, F32), jnp.zeros((H_NSA, tq, 1), F32),
            jnp.zeros((H_NSA, tq, LANES), F32))
    _, l, acc = lax.fori_loop(0, n_steps, body, init)
    return acc / jnp.maximum(l, 1e-30)


def _window_bias(tq):
    n = WINDOW // tq + 1
    dist = (jnp.arange(tq)[None, :, None] - jnp.arange(WINDOW + tq)[None, None, :]
            + jnp.arange(n)[:, None, None] * tq)
    return jnp.where((dist >= 0) & (dist < WINDOW), 0.0, NEG).astype(F32)


def _nsa_kernel(nq_ref, ckf_ref, cvf_ref, sk_ref, sv_ref, wk_ref, wv_ref, gt_ref, cw_ref, cb_ref, wb_ref,
                o_ref, ck_scr, cv_scr, bias_scr, *, nb):
    qi = pl.program_id(1)
    tq, tk = NSA_TQ, NSA_TK
    n_sel = min(N_SEL, nb)

    @pl.when(qi == 0)
    def _compress():
        for which, (src, dst) in enumerate(((ckf_ref, ck_scr), (cvf_ref, cv_scr))):
            def body(j, acc):
                x = src[pl.ds(j, nb, stride=CMP_BLK), :].astype(BF16)
                return acc + _dot(x, cw_ref[which, j])
            acc = lax.fori_loop(0, CMP_BLK, body, jnp.zeros((nb, LANES), F32))
            dst[...] = jnp.zeros_like(dst)
            dst[0:nb, :] = (acc + cb_ref[which]).astype(BF16)

    t0 = qi * tq
    nqf = nq_ref[...].astype(F32)
    gt = gt_ref[...]
    pos = t0 + _iota((tq, 1), 0)
    blk = _iota((1, LANES), 1)
    vis = ((blk * CMP_BLK + CMP_BLK - 1) <= pos) & (blk < nb)
    forced = ((blk == 0) | (blk == pos // CMP_BLK)) & (blk < nb)
    vis_bias = jnp.where(vis, 0.0, NEG)
    lane = _iota((tq, LANES), 1)
    kcol = _iota((1, tk), 1)
    blk_row = _iota((LANES, tk), 0)
    blk_of_key = _iota((LANES, tk), 1) // CMP_BLK

    n_kv = (t0 + tq + tk - 1) // tk

    q_all = jnp.concatenate(_group_queries(nqf, 0, tq) + _group_queries(nqf, 1, tq), axis=0)

    sc = _dot_nt(q_all, ck_scr[...]).reshape(H_NSA, tq, LANES) + vis_bias[None]
    pc = jnp.where(sc > 0.5 * NEG, jnp.exp(sc - jnp.max(sc, axis=-1, keepdims=True)), 0.0)
    pc = pc / jnp.maximum(jnp.sum(pc, axis=-1, keepdims=True), 1e-30)
    o_cmp = _dot(pc.reshape(H_NSA * tq, LANES).astype(BF16), cv_scr[...]).reshape(H_NSA, tq, LANES)

    sels = []
    for g in range(G_NSA):
        imp = pc[g * R_NSA] + pc[g * R_NSA + 1] + pc[g * R_NSA + 2] + pc[g * R_NSA + 3]
        score = jnp.where(forced, SEL_FORCE, jnp.where(vis, imp, -1.0))
        score = jnp.where(blk < nb, score, -2.0)
        sels.append(_select_blocks(score, nb, n_sel).astype(BF16))

    def mask_tile(j, _):
        expand = jnp.where(blk_row == blk_of_key + j * (tk // CMP_BLK), 1.0, 0.0).astype(BF16)
        causal = j * tk + kcol <= pos
        for g in range(G_NSA):
            chosen = _dot(sels[g], expand)
            bias_scr[j, g] = jnp.where((chosen > 0.5) & causal, 0.0, NEG)
        return 0

    lax.fori_loop(0, n_kv, mask_tile, 0)
    o_slc = _flash_run(q_all, sk_ref, sv_ref, bias_scr, n_kv, tq, tk)

    slab = pl.ds(pl.multiple_of(jnp.maximum(t0 - WINDOW, 0), tq), WINDOW + tq)
    wbias = wb_ref[jnp.minimum(qi, WINDOW // tq)]
    sw = _dot_nt(q_all, wk_ref[slab, :]).reshape(H_NSA, tq, WINDOW + tq) + wbias[None]
    pw = jnp.exp(sw - jnp.max(sw, axis=-1, keepdims=True))
    o_win = (_dot(pw.reshape(H_NSA * tq, WINDOW + tq).astype(BF16), wv_ref[slab, :]).reshape(H_NSA, tq, LANES)
             / jnp.maximum(jnp.sum(pw, axis=-1, keepdims=True), 1e-30))

    for c in range(H_NSA // 2):
        pair = []
        for h in (2 * c, 2 * c + 1):
            o_h = (gt[:, 3 * h:3 * h + 1] * o_cmp[h] + gt[:, 3 * h + 1:3 * h + 2] * o_slc[h]
                   + gt[:, 3 * h + 2:3 * h + 3] * o_win[h])
            if h % 2 != h // R_NSA:
                o_h = pltpu.roll(o_h, HD_NSA, axis=1)
            pair.append(o_h)
        o_ref[:, c * LANES:(c + 1) * LANES] = jnp.where(lane < HD_NSA, pair[0], pair[1]).astype(BF16)


def _nsa_call(nq, kv, kvb, gates, cw_bd, cb2, B, L):
    tq = NSA_TQ
    nQ = L // tq
    nb = L // CMP_BLK
    assert nb <= LANES and L % NSA_TK == 0 and NSA_TK % tq == 0 and WINDOW % tq == 0 and L >= WINDOW + tq
    win_bias = _window_bias(tq)
    rows = lambda w: pl.BlockSpec((tq, w), lambda b, q: (b * nQ + q, 0))
    seq = lambda c: pl.BlockSpec((L, LANES), lambda b, q: (b, c))
    return pl.pallas_call(
        functools.partial(_nsa_kernel, nb=nb),
        grid=(B, nQ),
        in_specs=[rows(D_NSA), seq(0), seq(1), seq(2), seq(3), seq(4), seq(5), rows(LANES),
                  _full((2, CMP_BLK, LANES, LANES)), _full((2, 1, LANES)), _full(win_bias.shape)],
        out_specs=rows(D_NSA),
        out_shape=jax.ShapeDtypeStruct((B * L, D_NSA), BF16),
        scratch_shapes=[pltpu.VMEM((LANES, LANES), BF16), pltpu.VMEM((LANES, LANES), BF16),
                        pltpu.VMEM((L // NSA_TK, G_NSA, tq, NSA_TK), F32)],
        compiler_params=_cparams(("parallel", "arbitrary")),
        name="nsa_prompt",
    )(nq, kv, kv, kvb, kvb, kvb, kvb, gates, cw_bd, cb2, win_bias)


def _normmm_kernel(x_ref, g_ref, w_ref, o_ref, ob_ref):
    y = _dot(_rms(x_ref[...], g_ref[...]).astype(BF16), w_ref[...])
    o_ref[...] = y
    ob_ref[...] = y.astype(BF16)


def _normmm_call(x2d, g, w, tm):
    n, d = x2d.shape
    m = w.shape[1]
    return pl.pallas_call(
        _normmm_kernel,
        grid=(n // tm,),
        in_specs=[pl.BlockSpec((tm, d), lambda i: (i, 0)), _full((1, d)), _full((d, m))],
        out_specs=[pl.BlockSpec((tm, m), lambda i: (i, 0))] * 2,
        out_shape=[jax.ShapeDtypeStruct((n, m), F32), jax.ShapeDtypeStruct((n, m), BF16)],
        compiler_params=_cparams(("parallel",)),
        name="norm_matmul",
    )(x2d, g.reshape(1, d), w)


def _mixout_kernel(yr_ref, on_ref, h_ref, wo_ref, g_ref, wq_ref, h1_ref, mq_ref):
    h1 = h_ref[...] + _dot(yr_ref[...], wo_ref[0:D_RET, :]) + _dot(on_ref[...], wo_ref[D_RET:, :])
    h1_ref[...] = h1
    mq_ref[...] = _dot(_rms(h1, g_ref[...]).astype(BF16), wq_ref[...]).astype(BF16)


def _mixout_call(yret, onsa, h, w_out, g_mem, w_mq, tm):
    n = h.shape[0]
    row = lambda w: pl.BlockSpec((tm, w), lambda i: (i, 0))
    return pl.pallas_call(
        _mixout_kernel,
        grid=(n // tm,),
        in_specs=[row(D_RET), row(D_NSA), row(D_MODEL), _full((D_RET + D_NSA, D_MODEL)), _full((1, D_MODEL)),
                  _full((D_MODEL, D_MEM))],
        out_specs=[row(D_MODEL), row(D_MEM)],
        out_shape=[jax.ShapeDtypeStruct((n, D_MODEL), F32), jax.ShapeDtypeStruct((n, D_MEM), BF16)],
        compiler_params=_cparams(("parallel",)),
        name="mixer_out",
    )(yret, onsa, h, w_out, g_mem.reshape(1, D_MODEL), w_mq)


def _memattn_kernel(q_ref, k_ref, v_ref, o_ref):
    q, k, v = q_ref[...], k_ref[...], v_ref[...]
    for h in range(H_MEM):
        sl = slice(h * HD_MEM, (h + 1) * HD_MEM)
        s = _dot_nt(q[:, sl], k[:, sl]) * (HD_MEM ** -0.5)
        p = jnp.exp(s - jnp.max(s, axis=-1, keepdims=True))
        p = p / jnp.sum(p, axis=-1, keepdims=True)
        o_ref[:, sl] = _dot(p.astype(BF16), v[:, sl]).astype(BF16)


def _memattn_call(mq, mkvb, B, L, n_mem, tm):
    nT = L // tm
    return pl.pallas_call(
        _memattn_kernel,
        grid=(B, nT),
        in_specs=[pl.BlockSpec((tm, D_MEM), lambda b, i: (b * nT + i, 0)),
                  pl.BlockSpec((n_mem, D_MEM), lambda b, i: (b, 0)),
                  pl.BlockSpec((n_mem, D_MEM), lambda b, i: (b, 1))],
        out_specs=pl.BlockSpec((tm, D_MEM), lambda b, i: (b * nT + i, 0)),
        out_shape=jax.ShapeDtypeStruct((B * L, D_MEM), BF16),
        compiler_params=_cparams(("parallel", "parallel")),
        name="mem_attention",
    )(mq, mkvb, mkvb)


def _memattn1_kernel(q_ref, k_ref, v_ref, o_ref):
    tb = q_ref.shape[0]
    for b in range(tb):
        q = q_ref[b]
        prod = k_ref[b] * q
        outs = []
        for h in range(H_MEM):
            sl = slice(h * HD_MEM, (h + 1) * HD_MEM)
            s = jnp.sum(prod[:, sl], axis=-1, keepdims=True) * (HD_MEM ** -0.5)
            p = jnp.exp(s - jnp.max(s, axis=0, keepdims=True))
            p = p / jnp.sum(p, axis=0, keepdims=True)
            outs.append(jnp.sum(p * v_ref[b][:, sl], axis=0, keepdims=True))
        o_ref[b] = jnp.concatenate(outs, axis=-1)


def _memattn1_call(mq, cache_k, cache_v):
    DB, n_mem = cache_k.shape[0], cache_k.shape[1]
    tb = 8
    blk = pl.BlockSpec((tb, n_mem, D_MEM), lambda i: (i, 0, 0))
    q3 = pl.BlockSpec((tb, 1, D_MEM), lambda i: (i, 0, 0))
    o = pl.pallas_call(
        _memattn1_kernel,
        grid=(DB // tb,),
        in_specs=[q3, blk, blk],
        out_specs=q3,
        out_shape=jax.ShapeDtypeStruct((DB, 1, D_MEM), F32),
        compiler_params=_cparams(("parallel",)),
        name="mem_attention_step",
    )(mq.astype(F32).reshape(DB, 1, D_MEM), cache_k.reshape(DB, n_mem, D_MEM), cache_v.reshape(DB, n_mem, D_MEM))
    return o.reshape(DB, D_MEM).astype(BF16)


def _prep_peer_keys(subkeys):
    half = PEER_DKEY // 2
    z = jnp.zeros_like(subkeys[:, 0])
    k0 = jnp.concatenate([subkeys[:, 0], z], axis=-1)
    k1 = jnp.concatenate([z, subkeys[:, 1]], axis=-1)
    return jnp.concatenate([k0, k1], axis=1).astype(BF16)


def _top_rows(x, k, payload=None):
    n = x.shape[-2]
    ri = _iota(x.shape, x.ndim - 2)
    vals, picks = [], []
    for _ in range(k):
        m = jnp.max(x, axis=-2, keepdims=True)
        i = jnp.min(jnp.where(x == m, ri, n), axis=-2, keepdims=True)
        hit = ri == i
        vals.append(m)
        picks.append(i if payload is None else jnp.max(jnp.where(hit, payload, -1), axis=-2, keepdims=True))
        x = jnp.where(hit, -jnp.inf, x)
    return jnp.concatenate(vals, axis=-2), jnp.concatenate(picks, axis=-2)


def _route_kernel(om_ref, h1_ref, wo_ref, g_ref, wq_ref, sk_ref, h2_ref, xn_ref, idx_ref, gw_ref,
                  idx_scr, gw_scr):
    tm = h1_ref.shape[0]
    h2 = h1_ref[...] + _dot(om_ref[...], wo_ref[...])
    h2_ref[...] = h2
    xn = _rms(h2, g_ref[...])
    xn_ref[...] = xn
    xb = xn.astype(BF16)

    k = PEER_TOPK
    n_b = [k // (a + 1) for a in range(k)]
    pad = -sum(n_b) % 8
    lanes = min(tm, LANES)

    def head(h, _):
        pq = _dot(xb, wq_ref[h]).astype(BF16)
        s_all = _dot_nt(sk_ref[h], pq)
        rows = pl.ds(pl.multiple_of(h * k, k), k)
        for c in range(tm // lanes):
            s = s_all[:, c * lanes:(c + 1) * lanes].reshape(2, PEER_KEYS, lanes)
            v12, i12 = _top_rows(s, k)
            cand = jnp.concatenate([v12[0][a:a + 1] + v12[1][0:n_b[a]] for a in range(k)]
                                   + [jnp.full((pad, lanes), -jnp.inf, F32)], axis=0)
            cidx = jnp.concatenate([i12[0][a:a + 1] * PEER_KEYS + i12[1][0:n_b[a]] for a in range(k)]
                                   + [jnp.full((pad, lanes), -1, jnp.int32)], axis=0)
            top, expert = _top_rows(cand, k, payload=cidx)
            e = jnp.exp(top - top[0:1, :])
            idx_scr[rows, c * lanes:(c + 1) * lanes] = expert * PEER_ROW
            gw_scr[rows, c * lanes:(c + 1) * lanes] = e / jnp.sum(e, axis=0, keepdims=True)
        return 0

    lax.fori_loop(0, PEER_HEADS, head, 0)
    idx_ref[...] = idx_scr[...].T
    gw_ref[...] = gw_scr[...].T


def _route_call(omem, h1, w_mo, g_ffn, wq_h, sk_pad, tm):
    n = h1.shape[0]
    nk = PEER_HEADS * PEER_TOPK
    row = lambda w: pl.BlockSpec((tm, w), lambda i: (i, 0))
    return pl.pallas_call(
        _route_kernel,
        grid=(n // tm,),
        in_specs=[row(D_MEM), row(D_MODEL), _full((D_MEM, D_MODEL)), _full((1, D_MODEL)),
                  _full((PEER_HEADS, D_MODEL, PEER_DKEY)), _full((PEER_HEADS, 2 * PEER_KEYS, PEER_DKEY))],
        out_specs=[row(D_MODEL), row(D_MODEL), row(nk), row(nk)],
        out_shape=[jax.ShapeDtypeStruct((n, D_MODEL), F32), jax.ShapeDtypeStruct((n, D_MODEL), F32),
                   jax.ShapeDtypeStruct((n, nk), jnp.int32), jax.ShapeDtypeStruct((n, nk), F32)],
        scratch_shapes=[pltpu.VMEM((nk, tm), jnp.int32), pltpu.VMEM((nk, tm), F32)],
        compiler_params=_cparams(("parallel",)),
        name="peer_route",
    )(omem, h1, w_mo, g_ffn.reshape(1, D_MODEL), wq_h, sk_pad)


def _gelu_tanh(x):
    return 0.5 * x * (1.0 + jnp.tanh(0.7978845608028654 * (x + 0.044715 * x * x * x)))


PEER_TT = 64


def _pack_table(t):
    e, d = t.shape
    b = lax.bitcast_convert_type(t.astype(BF16), jnp.uint16).astype(jnp.uint32)
    w = b[:, :d // 2] | (b[:, d // 2:] << 16)
    return w.reshape(e * PEER_ROW, LANES)


def _expert_row(tab_ref, off):
    w = tab_ref[pl.ds(pl.multiple_of(off, PEER_ROW), PEER_ROW), :]
    lo = pltpu.bitcast(w << 16, F32)
    hi = pltpu.bitcast(w & jnp.uint32(0xFFFF0000), F32)
    return lo, hi


def _peer_act_kernel(idx_ref, x_ref, gw_ref, tab_ref, c_ref, part_scr, act_scr):
    tt, nk = gw_ref.shape
    sub = _iota((8, LANES), 0)
    keep_pairs = (sub % 4) < 2
    keep_even = (sub % 2) == 0
    feed = (0, 4, 2, 6, 1, 5, 3, 7)

    def row_sums8(p):
        p = [p[i] for i in feed]
        v = [jnp.concatenate([p[2 * i], p[2 * i + 1]], axis=0) for i in range(4)]
        w = [x + pltpu.roll(x, 6, axis=0) for x in v]
        u = [jnp.where(keep_pairs, w[2 * i], pltpu.roll(w[2 * i + 1], 2, axis=0)) for i in range(2)]
        z = [x + pltpu.roll(x, 7, axis=0) for x in u]
        return jnp.where(keep_even, z[0], pltpu.roll(z[1], 1, axis=0))

    def lane_sums(t):
        act_scr[pl.ds(t, 1), :] = jnp.sum(part_scr[t].T, axis=0, keepdims=True)

    part_scr[0] = jnp.zeros((nk, LANES), F32)

    def token(t, _):
        lane_sums(jnp.maximum(t - 1, 0))
        xb = pltpu.bitcast(x_ref[t].astype(BF16).astype(F32), jnp.uint32)
        x_pk = pltpu.bitcast((xb[0:PEER_ROW] >> 16) | xb[PEER_ROW:], BF16)
        hi_mask = jnp.uint32(0xFFFF0000)
        for j0 in range(0, nk, 8):
            prods = []
            for j in range(j0, j0 + 8):
                w = tab_ref[pl.ds(pl.multiple_of(idx_ref[t * nk + j], PEER_ROW), PEER_ROW), :]
                pp = pltpu.bitcast(pltpu.bitcast(w, BF16) * x_pk, jnp.uint32)
                prods.append(pltpu.bitcast(pp << 16, F32) + pltpu.bitcast(pp & hi_mask, F32))
            part_scr[t, j0:j0 + 8, :] = row_sums8(prods)
        return 0

    lax.fori_loop(0, tt, token, 0)
    lane_sums(tt - 1)
    c_ref[...] = gw_ref[...] * _gelu_tanh(act_scr[...])


def _peer_out_kernel(idx_ref, c_ref, h2_ref, gf_ref, tab_ref, y_ref, splat_scr):
    tt, nk = c_ref.shape
    n_acc = 8

    def splat(buf, t):
        splat_scr[buf] = jnp.broadcast_to(c_ref[pl.ds(t, 1), :], (nk, nk)).T

    def accumulate(buf, t):
        acc_lo = [jnp.zeros((PEER_ROW, LANES), F32)] * n_acc
        acc_hi = [jnp.zeros((PEER_ROW, LANES), F32)] * n_acc
        for j in range(nk):
            lo, hi = _expert_row(tab_ref, idx_ref[t * nk + j])
            c = splat_scr[buf, j:j + 1, :]
            acc_lo[j % n_acc] = acc_lo[j % n_acc] + c * lo
            acc_hi[j % n_acc] = acc_hi[j % n_acc] + c * hi
        tree = lambda v: v[0] if len(v) == 1 else tree([a + b for a, b in zip(v[0::2], v[1::2])])
        y_ref[t] = h2_ref[t] + jnp.concatenate([tree(acc_lo), tree(acc_hi)], axis=0)

    splat(0, 0)

    def token_pair(i, _):
        t = 2 * i
        splat(1, t + 1)
        accumulate(0, t)
        splat(0, jnp.minimum(t + 2, tt - 1))
        accumulate(1, t + 1)
        return 0

    lax.fori_loop(0, tt // 2, token_pair, 0)
    h3 = y_ref[...]
    ms = jnp.sum(jnp.sum(h3 * h3, axis=2, keepdims=True), axis=1, keepdims=True) * (1.0 / D_MODEL)
    y_ref[...] = h3 * lax.rsqrt(ms + EPS) * gf_ref[...]


def _peer_call(idx, xn, gw, h2, g_final, u_tab, v_tab, tt):
    n, nk = idx.shape
    assert n % tt == 0 and nk % 8 == 0 and tt % 2 == 0
    smem = lambda: pl.BlockSpec((tt * nk,), lambda i: (i,), memory_space=pltpu.SMEM)
    idx = idx.reshape(n * nk)
    tile = lambda: pl.BlockSpec((tt, 8, LANES), lambda i: (i, 0, 0))
    table = lambda t: pl.BlockSpec(t.shape, lambda i: (0, 0), pipeline_mode=pl.Buffered(1))
    as_tiles = lambda a: a.reshape(n, 8, LANES)
    c = pl.pallas_call(
        _peer_act_kernel,
        grid=(n // tt,),
        in_specs=[smem(), tile(), pl.BlockSpec((tt, nk), lambda i: (i, 0)), table(u_tab)],
        out_specs=pl.BlockSpec((tt, nk), lambda i: (i, 0)),
        out_shape=jax.ShapeDtypeStruct((n, nk), F32),
        scratch_shapes=[pltpu.VMEM((tt, nk, LANES), F32), pltpu.VMEM((tt, nk), F32)],
        compiler_params=_cparams(("arbitrary",)),
        name="peer_act",
    )(idx, as_tiles(xn), gw, u_tab)
    y = pl.pallas_call(
        _peer_out_kernel,
        grid=(n // tt,),
        in_specs=[smem(), pl.BlockSpec((tt, nk), lambda i: (i, 0)), tile(), _full((8, LANES)), table(v_tab)],
        out_specs=tile(),
        out_shape=jax.ShapeDtypeStruct((n, 8, LANES), F32),
        scratch_shapes=[pltpu.VMEM((2, nk, nk), F32)],
        compiler_params=_cparams(("arbitrary",)),
        name="peer_out",
    )(idx, c, as_tiles(h2), g_final.reshape(8, LANES), v_tab)
    return y.reshape(n, D_MODEL)


CMP_PAGES = 128


def _cmp_pages_kernel(x_hbm, w_ref, b_ref, o_ref, buf, sem):
    i = pl.program_id(0)
    n_pages = o_ref.shape[0]

    def start(step, slot):
        def body(d, _):
            for g in range(G_NSA):
                pltpu.make_async_copy(x_hbm.at[pl.ds(step * n_pages, n_pages), g, d, :], buf.at[slot, d, g],
                                      sem.at[slot]).start()
            return 0
        lax.fori_loop(0, HD_NSA, body, 0)

    @pl.when(i == 0)
    def _():
        start(0, 0)

    @pl.when(i + 1 < pl.num_programs(0))
    def _():
        start(i + 1, (i + 1) % 2)

    slot = i % 2
    pltpu.make_async_copy(buf.at[slot], buf.at[slot], sem.at[slot]).wait()
    n_chain = 4

    def body(k, accs):
        out = []
        for g in range(G_NSA):
            for c in range(n_chain):
                d = k * n_chain + c
                out.append(accs[g * n_chain + c] + _dot(buf[slot, d, g].astype(BF16), w_ref[d]))
        return tuple(out)

    zero = jnp.zeros((n_pages, LANES), F32)
    accs = lax.fori_loop(0, HD_NSA // n_chain, body, (zero,) * (G_NSA * n_chain))
    for g in range(G_NSA):
        acc = accs[g * n_chain]
        for c in range(1, n_chain):
            acc = acc + accs[g * n_chain + c]
        o_ref[:, g * LANES:(g + 1) * LANES] = acc + b_ref[...]


def _cmp_pages_call(pool_t, w_bd, b):
    n_phys = pool_t.shape[0]
    assert n_phys % CMP_PAGES == 0
    out = pl.pallas_call(
        _cmp_pages_kernel,
        grid=(n_phys // CMP_PAGES,),
        in_specs=[pl.BlockSpec(memory_space=pl.ANY), _full((HD_NSA, PAGE_SIZE, LANES)), _full((1, LANES))],
        out_specs=pl.BlockSpec((CMP_PAGES, G_NSA * LANES), lambda i: (i, 0)),
        out_shape=jax.ShapeDtypeStruct((n_phys, G_NSA * LANES), F32),
        scratch_shapes=[pltpu.VMEM((2, HD_NSA, G_NSA, CMP_PAGES, PAGE_SIZE), F32), pltpu.SemaphoreType.DMA((2,))],
        compiler_params=_cparams(("arbitrary",)),
        name="compress_pages",
    )(pool_t, w_bd, jnp.tile(b, 2).reshape(1, LANES))
    return out.reshape(n_phys * G_NSA, LANES)


def _nsa1_cmp_kernel(pt_ref, nq_ref, ckn_ref, cvn_ref, tk_ref, tv_ref, cw_ref, cb_ref, ocmp_ref, idx_ref,
                     kg_scr, vg_scr, *, n_pages, q_pos):
    tb = nq_ref.shape[0]
    nb_past = n_pages * (PAGE_SIZE // CMP_BLK)
    base = pl.program_id(0) * tb
    lane = _iota((1, LANES), 1)
    blk_n = 2 * (lane % HD_NSA) + lane // HD_NSA
    forced = (blk_n == 0) | (blk_n == q_pos // CMP_BLK)
    vis = (blk_n * CMP_BLK + CMP_BLK - 1) <= q_pos
    new_vis = (nb_past * CMP_BLK + CMP_BLK - 1) <= q_pos
    new_forced = nb_past == q_pos // CMP_BLK
    n_row = jnp.broadcast_to(blk_n, (LANES, LANES))
    n_col = 2 * (_iota((LANES, LANES), 0) % HD_NSA) + _iota((LANES, LANES), 0) // HD_NSA
    lane8 = _iota((1, LANES), 1) // HD_NSA

    def sample(b, _):
        qrow = nq_ref[pl.ds(b, 1), :].astype(F32)
        new_k = _dot(ckn_ref[pl.ds(b, 1), :].astype(BF16), cw_ref[0]) + cb_ref[0]
        new_v = _dot(cvn_ref[pl.ds(b, 1), :].astype(BF16), cw_ref[1]) + cb_ref[1]
        o_row, idx_row = [], jnp.full((1, LANES), -1, jnp.int32)
        for g in range(G_NSA):
            def gather(i, _):
                r = pt_ref[base + b, i] * G_NSA + g
                kg_scr[pl.ds(i, 1), :] = tk_ref[pl.ds(r, 1), :]
                vg_scr[pl.ds(i, 1), :] = tv_ref[pl.ds(r, 1), :]
                return 0
            lax.fori_loop(0, n_pages, gather, 0)
            rows = []
            for c in range(2):
                for r in range(R_NSA):
                    h = g * R_NSA + r
                    x = qrow[:, (h // 2) * LANES:(h // 2 + 1) * LANES]
                    if h % 2 != c:
                        x = pltpu.roll(x, HD_NSA, axis=1)
                    rows.append(jnp.where(lane8 == c, x, 0.0))
            qpad = jnp.concatenate(rows, axis=0)
            s = _dot_nt(qpad.astype(BF16), kg_scr[...].astype(BF16)) * (HD_NSA ** -0.5)
            s3 = s.reshape(2, R_NSA, n_pages)
            qg = qpad[R_NSA * g:R_NSA * (g + 1), :]
            nk_g = jnp.where(lane8 == g, new_k.astype(BF16).astype(F32), 0.0)
            s_new = jnp.sum(qg.astype(BF16).astype(F32) * nk_g, axis=-1, keepdims=True) * (HD_NSA ** -0.5)
            s_new = jnp.where(new_vis, s_new, NEG)[None]
            m = jnp.maximum(jnp.max(jnp.max(s3, axis=2, keepdims=True), axis=0, keepdims=True), s_new)
            p = jnp.exp(s3 - m)
            p_new = jnp.where(s_new > 0.5 * NEG, jnp.exp(s_new - m), 0.0)
            l = jnp.sum(jnp.sum(p, axis=2, keepdims=True), axis=0, keepdims=True) + p_new
            inv = 1.0 / jnp.maximum(l, 1e-30)
            p = p * inv
            p_new = p_new * inv
            res = _dot(p.reshape(2 * R_NSA, n_pages).astype(BF16), vg_scr[...].astype(BF16))
            o4 = res[0:R_NSA] + pltpu.roll(res[R_NSA:], HD_NSA, axis=1)
            nv_g = new_v.astype(BF16).astype(F32)
            if g == 1:
                nv_g = pltpu.roll(nv_g, HD_NSA, axis=1)
            o4 = o4 + p_new[0].astype(BF16).astype(F32) * nv_g
            o_row += [o4[r:r + 1, 0:HD_NSA] for r in range(R_NSA)]
            imp2 = jnp.sum(p, axis=1)
            imp = jnp.concatenate([imp2[0:1], imp2[1:2]], axis=-1)
            score = jnp.where(forced, SEL_FORCE, jnp.where(vis, imp, -1.0))
            imp_new = jnp.sum(p_new)
            sc_new = SEL_FORCE if new_forced else jnp.where(new_vis, imp_new, -1.0)
            a = jnp.broadcast_to(score, (LANES, LANES))
            bt = a.T
            ahead = (bt > a) | ((bt == a) & (n_col < n_row))
            rank = jnp.sum(jnp.where(ahead, 1.0, 0.0), axis=0, keepdims=True) + jnp.where(sc_new > score, 1.0, 0.0)
            rank_new = jnp.sum(jnp.where(score >= sc_new, 1.0, 0.0))
            for r in range(N_SEL):
                hit = (rank == r) & (score >= 0.0)
                val = jnp.sum(jnp.where(hit, blk_n + 1, 0)) - 1
                val = jnp.where((rank_new == r) & (sc_new >= 0.0), nb_past, val)
                idx_row = jnp.where(lane == g * N_SEL + r, val, idx_row)
        ocmp_ref[pl.ds(b, 1), :] = jnp.concatenate(o_row, axis=-1)
        idx_ref[pl.ds(b, 1), :] = idx_row
        return 0

    lax.fori_loop(0, tb, sample, 0)


def _nsa1_cmp_call(page_table, nq, ck_new, cv_new, tbl_k, tbl_v, cw_bd, cb2, q_pos):
    DB, n_pages = page_table.shape
    assert n_pages * (PAGE_SIZE // CMP_BLK) == LANES
    tb = 8
    row = lambda w: pl.BlockSpec((tb, w), lambda i, pt: (i, 0))
    whole = lambda a: pl.BlockSpec(a.shape, lambda i, pt: (0,) * a.ndim, pipeline_mode=pl.Buffered(1))
    return pl.pallas_call(
        functools.partial(_nsa1_cmp_kernel, n_pages=n_pages, q_pos=q_pos),
        grid_spec=pltpu.PrefetchScalarGridSpec(
            num_scalar_prefetch=1,
            grid=(DB // tb,),
            in_specs=[row(D_NSA), row(LANES), row(LANES), whole(tbl_k), whole(tbl_v),
                      pl.BlockSpec((2, LANES, LANES), lambda i, pt: (0, 0, 0)),
                      pl.BlockSpec((2, 1, LANES), lambda i, pt: (0, 0, 0))],
            out_specs=[row(D_NSA), row(LANES)],
            scratch_shapes=[pltpu.VMEM((n_pages, LANES), F32), pltpu.VMEM((n_pages, LANES), F32)]),
        out_shape=[jax.ShapeDtypeStruct((DB, D_NSA), F32), jax.ShapeDtypeStruct((DB, LANES), jnp.int32)],
        compiler_params=_cparams(("arbitrary",)),
        name="nsa_step_compressed",
    )(page_table, nq.astype(F32), ck_new, cv_new, tbl_k, tbl_v, cw_bd[:, 0], cb2)


def _nsa1_attn_kernel(idx_ref, pt_ref, nq_ref, kvn_ref, gt_ref, ocmp_ref, wk_ref, wv_ref, sk_hbm, sv_hbm,
                      o_ref, kbuf, vbuf, sem, *, n_pages, q_pos, past_len):
    b = pl.program_id(0)
    bpp = PAGE_SIZE // CMP_BLK
    nb_past = n_pages * bpp
    wb = wk_ref.shape[3]

    def block_copies(bb, slot, g, r):
        n = idx_ref[bb, g * N_SEL + r]
        past = (n >= 0) & (n < nb_past)
        page = pt_ref[bb, jnp.clip(n, 0, nb_past - 1) // bpp]
        ck = pltpu.make_async_copy(sk_hbm.at[page, g], kbuf.at[slot, g, r], sem.at[slot, 0])
        cv = pltpu.make_async_copy(sv_hbm.at[page, g], vbuf.at[slot, g, r], sem.at[slot, 1])
        return past, ck, cv

    def issue(bb, slot):
        for g in range(G_NSA):
            for r in range(N_SEL):
                past, ck, cv = block_copies(bb, slot, g, r)

                @pl.when(past)
                def _():
                    ck.start()
                    cv.start()

                @pl.when(jnp.logical_not(past))
                def _():
                    kbuf[slot, g, r] = jnp.zeros((HD_NSA, PAGE_SIZE), F32)
                    vbuf[slot, g, r] = jnp.zeros((HD_NSA, PAGE_SIZE), F32)

    @pl.when(b == 0)
    def _():
        issue(0, 0)

    @pl.when(b + 1 < pl.num_programs(0))
    def _():
        issue(b + 1, (b + 1) % 2)

    slot = b % 2
    for g in range(G_NSA):
        for r in range(N_SEL):
            past, ck, cv = block_copies(b, slot, g, r)

            @pl.when(past)
            def _():
                ck.wait()
                cv.wait()

    qrow = nq_ref[0].astype(F32)
    kvn = kvn_ref[0]
    gt = gt_ref[0]
    ocmp = ocmp_ref[0]
    scale = HD_NSA ** -0.5
    blk_in_page = _iota((1, PAGE_SIZE), 1) // CMP_BLK
    wpos = past_len - wb + _iota((1, wb), 1)
    wdist = q_pos - wpos
    w_ok = (wdist >= 0) & (wdist < WINDOW) & (wpos >= 0)
    bfr = lambda t: t.astype(BF16).astype(F32)
    pieces = []
    for g in range(G_NSA):
        q4 = jnp.concatenate([qrow[:, (g * R_NSA + r) * HD_NSA:(g * R_NSA + r + 1) * HD_NSA]
                              for r in range(R_NSA)], axis=0)
        q4b = q4.astype(BF16)
        new = lambda i: kvn[:, i * D_KV + g * HD_NSA:i * D_KV + (g + 1) * HD_NSA]
        scores, keeps = [], []
        has_new = jnp.int32(0)
        for r in range(N_SEL):
            n = idx_ref[b, g * N_SEL + r]
            past = ((n >= 0) & (n < nb_past)).astype(jnp.int32)
            keep = (blk_in_page == n % bpp) & (past > 0)
            has_new = has_new | (n == nb_past).astype(jnp.int32)
            keeps.append(keep)
            scores.append(jnp.where(keep, _dot(q4b, kbuf[slot, g, r].astype(BF16)) * scale, NEG))
        s_new = jnp.sum(bfr(q4) * bfr(new(2)), axis=-1, keepdims=True) * scale
        s_new = jnp.where(has_new > 0, s_new, NEG)
        m = s_new
        for s in scores:
            m = jnp.maximum(m, jnp.max(s, axis=-1, keepdims=True))
        p_new = jnp.where(s_new > 0.5 * NEG, jnp.exp(s_new - m), 0.0)
        l = p_new
        acc = bfr(p_new) * bfr(new(3))
        for r in range(N_SEL):
            p = jnp.where(scores[r] > 0.5 * NEG, jnp.exp(scores[r] - m), 0.0)
            l = l + jnp.sum(p, axis=-1, keepdims=True)
            v_t = jnp.where(keeps[r], vbuf[slot, g, r], 0.0).astype(BF16)
            acc = acc + _dot_nt(p.astype(BF16), v_t)
        o_slc = acc / jnp.maximum(l, 1e-30)
        wk = wk_ref[0, g].astype(BF16)
        wv = wv_ref[0, g].astype(BF16)
        s = jnp.where(w_ok, _dot(q4b, wk) * scale, NEG)
        s_new = jnp.sum(bfr(q4) * bfr(new(4)), axis=-1, keepdims=True) * scale
        m = jnp.maximum(jnp.max(s, axis=-1, keepdims=True), s_new)
        p = jnp.where(s > 0.5 * NEG, jnp.exp(s - m), 0.0)
        p_new = jnp.exp(s_new - m)
        l = jnp.sum(p, axis=-1, keepdims=True) + p_new
        o_win = (_dot_nt(p.astype(BF16), wv) + bfr(p_new) * bfr(new(5))) / jnp.maximum(l, 1e-30)
        for r in range(R_NSA):
            h = g * R_NSA + r
            pieces.append(gt[:, 3 * h:3 * h + 1] * ocmp[:, h * HD_NSA:(h + 1) * HD_NSA]
                          + gt[:, 3 * h + 1:3 * h + 2] * o_slc[r:r + 1]
                          + gt[:, 3 * h + 2:3 * h + 3] * o_win[r:r + 1])
    o_ref[0] = jnp.concatenate(pieces, axis=-1)


def _nsa1_attn_call(idx, page_table, nq, kv_new, gates, ocmp, win_k, win_v, slc_k, slc_v, q_pos, past_len):
    DB, n_pages = page_table.shape
    wb = win_k.shape[3]
    r3 = lambda a: a.reshape(DB, 1, a.shape[-1])
    row = lambda w: pl.BlockSpec((1, 1, w), lambda b, *_: (b, 0, 0))
    win = pl.BlockSpec((1, G_NSA, HD_NSA, wb), lambda b, *_: (b, 0, 0, 0))
    o = pl.pallas_call(
        functools.partial(_nsa1_attn_kernel, n_pages=n_pages, q_pos=q_pos, past_len=past_len),
        grid_spec=pltpu.PrefetchScalarGridSpec(
            num_scalar_prefetch=2,
            grid=(DB,),
            in_specs=[row(D_NSA), row(6 * D_KV), row(LANES), row(D_NSA), win, win,
                      pl.BlockSpec(memory_space=pl.ANY), pl.BlockSpec(memory_space=pl.ANY)],
            out_specs=row(D_NSA),
            scratch_shapes=[pltpu.VMEM((2, G_NSA, N_SEL, HD_NSA, PAGE_SIZE), F32),
                            pltpu.VMEM((2, G_NSA, N_SEL, HD_NSA, PAGE_SIZE), F32),
                            pltpu.SemaphoreType.DMA((2, 2))]),
        out_shape=jax.ShapeDtypeStruct((DB, 1, D_NSA), F32),
        compiler_params=_cparams(("arbitrary",)),
        name="nsa_step_attend",
    )(idx, page_table, r3(nq), r3(kv_new), r3(gates), r3(ocmp), win_k, win_v, slc_k, slc_v)
    return o.reshape(DB, D_NSA).astype(BF16)


def _token_tail(yret, onsa, h, mem_attend, lw, tm):
    w_out, g_mem, w_mq, w_mo, g_ffn, wq_h, sk_pad, u, v, g_final = lw
    h1, mq = _mixout_call(yret, onsa, h, w_out, g_mem, w_mq, tm)
    omem = mem_attend(mq)
    h2, xn, idx, gw = _route_call(omem, h1, w_mo, g_ffn, wq_h, sk_pad, tm)
    return _peer_call(idx, xn, gw, h2, g_final, u, v, min(PEER_TT, h.shape[0]))


def _all_tables(pos):
    return (_rot_tables(pos, H_RET, HD_RET, HD_RET, RET_THETA)
            + _rot_tables(pos, H_NSA, HD_NSA, ROPE_DIMS, ROPE_THETA)
            + _rot_tables(pos, G_NSA, HD_NSA, ROPE_DIMS, ROPE_THETA))


def kernel(x_prompt, x_sample, mem_prompt, state_ret, cache_cmp_k, cache_cmp_v, cache_slc_k, cache_slc_v,
           cache_win_k, cache_win_v, cache_mem_k, cache_mem_v, page_table, norm_mix_g, w_in, ret_gn_g,
           cmp_w, cmp_b, w_out, norm_mem_g, mem_norm_g, w_mq, w_mk, w_mv, w_mo, norm_ffn_g,
           peer_wq, peer_subkeys, peer_u, peer_v, norm_final_g):
    B, L, D = x_prompt.shape
    DB, LS, _ = x_sample.shape
    n_mem = mem_prompt.shape[1]
    n_pages = page_table.shape[1]
    past_len = n_pages * PAGE_SIZE
    assert w_in.shape[0] == 1 and LS == 1 and D == D_MODEL
    l = 0
    tm = 256

    w_all = _prep_w_in(w_in[l])
    cw_bd = _prep_cmp_w(cmp_w[l])
    cb2 = jnp.tile(cmp_b[l], (1, G_NSA)).reshape(2, 1, LANES)
    w_kv = jnp.concatenate([w_mk[l], w_mv[l]], axis=1).astype(BF16)
    wq_h = peer_wq[l].reshape(D, PEER_HEADS, PEER_DKEY).transpose(1, 0, 2).astype(BF16)
    lw = (w_out[l].astype(BF16), norm_mem_g[l], w_mq[l].astype(BF16), w_mo[l].astype(BF16), norm_ffn_g[l],
          wq_h, _prep_peer_keys(peer_subkeys[l]), _pack_table(peer_u[l]), _pack_table(peer_v[l]), norm_final_g)

    xs = x_sample.reshape(DB, D)
    rq, rk, rv, rg, nq, kvs, _, gt = _proj_call(xs, norm_mix_g[l], w_all,
                                                _all_tables(jnp.full((DB,), past_len, jnp.int32)), DB)
    yret, s_state = _ret1_call(rq, rk, rv, rg, ret_gn_g[l], state_ret[l])
    fm = lambda cache: jnp.swapaxes(cache[l], -1, -2)
    cw_pg = _prep_cmp_w(cmp_w[l].transpose(0, 2, 1, 3))
    tbl_k = _cmp_pages_call(fm(cache_cmp_k), cw_pg[0], cmp_b[l, 0])
    tbl_v = _cmp_pages_call(fm(cache_cmp_v), cw_pg[1], cmp_b[l, 1])
    ocmp, sel_idx = _nsa1_cmp_call(page_table, nq, kvs[:, 0:D_KV], kvs[:, D_KV:2 * D_KV], tbl_k, tbl_v, cw_bd, cb2,
                                   past_len)
    onsa = _nsa1_attn_call(sel_idx, page_table, nq, kvs, gt, ocmp, fm(cache_win_k), fm(cache_win_v),
                           fm(cache_slc_k), fm(cache_slc_v), past_len, past_len)
    y_s = _token_tail(yret, onsa, xs, lambda mq: _memattn1_call(mq, cache_mem_k[l], cache_mem_v[l]), lw, DB)

    new = lambda i: kvs[:, i * D_KV:(i + 1) * D_KV].reshape(DB, G_NSA, 1, HD_NSA)
    wb = cache_win_k.shape[3]
    keep_s = min(WINDOW, wb + 1)
    win = lambda cache, i: jnp.concatenate([cache[l], new(i)], axis=2)[:, :, wb + 1 - keep_s:]

    xp = x_prompt.reshape(B * L, D)
    rq, rk, rv, rg, nq, kv, kvb, gt = _proj_call(xp, norm_mix_g[l], w_all,
                                                 _all_tables(jnp.arange(L, dtype=jnp.int32)), tm)
    yret, p_state = _ret_call(rq, rk, rv, rg, ret_gn_g[l], B, L)
    onsa = _nsa_call(nq, kv, kvb, gt, cw_bd, cb2, B, L)
    mkv, mkvb = _normmm_call(mem_prompt.reshape(B * n_mem, D), mem_norm_g[l], w_kv, tm)
    y_p = _token_tail(yret, onsa, xp, lambda mq: _memattn_call(mq, mkvb, B, L, n_mem, tm), lw, tm)

    kv6 = kv.reshape(B, L, 6, G_NSA, HD_NSA)
    pages = lambda i: kv6[:, :, i].reshape(B, L // PAGE_SIZE, PAGE_SIZE, G_NSA, HD_NSA).transpose(0, 1, 3, 2, 4)
    keep = min(WINDOW, L)
    tail = lambda i: kv6[:, L - keep:, i].transpose(0, 2, 1, 3)
    mem4 = lambda t: t.reshape(B, n_mem, H_MEM, HD_MEM)
    st = lambda t: t[None]
    return (y_p.reshape(B, L, D), y_s.reshape(DB, 1, D), st(p_state),
            st(pages(0)), st(pages(1)), st(pages(2)), st(pages(3)), st(tail(4)), st(tail(5)),
            st(mem4(mkv[:, :D_MEM])), st(mem4(mkv[:, D_MEM:])), st(s_state),
            st(new(0)), st(new(1)), st(new(2)), st(new(3)), st(win(cache_win_k, 4)), st(win(cache_win_v, 5)))
```

```python
import functools

import numpy as np
import jax
import jax.numpy as jnp
from jax import lax
from jax.experimental import pallas as pl
from jax.experimental.pallas import tpu as pltpu

F32 = jnp.float32
BF16 = jnp.bfloat16

D_MODEL = 1024
PAGE_SIZE = 128
H_RET = 8
HD_RET = 64
RET_CHUNK = 128
RET_THETA = 10000.0
H_NSA = 8
G_NSA = 2
R_NSA = H_NSA // G_NSA
HD_NSA = 64
CMP_BLK = 64
N_SEL = 16
WINDOW = 512
SEL_FORCE = 1.0e4
ROPE_THETA = 500000.0
ROPE_DIMS = HD_NSA // 4
H_MEM = 4
HD_MEM = 128
PEER_KEYS = 128
PEER_HEADS = 8
PEER_DKEY = 128
PEER_TOPK = 16
PEER_ROW = 4
EPS = 1e-6

D_RET = H_RET * HD_RET
D_NSA = H_NSA * HD_NSA
D_KV = G_NSA * HD_NSA
D_MEM = H_MEM * HD_MEM
NEG = -1.0e30

LANES = 128
VMEM_LIMIT = 56 * 1024 * 1024

C_RQ, C_RQR, C_RK, C_RKR, C_RV, C_RG, C_NQ, C_NQR = (i * 512 for i in range(8))
C_KV = 4096
C_KVR = C_KV + 6 * D_KV
C_NG = C_KVR + 3 * D_KV
N_PROJ = C_NG + LANES


def _cparams(sem):
    return pltpu.CompilerParams(dimension_semantics=sem, vmem_limit_bytes=VMEM_LIMIT)


def _rms(x, g):
    return x * lax.rsqrt(jnp.mean(x * x, axis=-1, keepdims=True) + EPS) * g


def _dot(a, b):
    return jnp.dot(a, b, preferred_element_type=F32)


def _dot_nt(a, b):
    return lax.dot_general(a, b, (((1,), (1,)), ((), ())), preferred_element_type=F32)


def _full(shape):
    n = len(shape)
    return pl.BlockSpec(shape, lambda *_: (0,) * n)


def _partner_cols(n_heads, hd, n_rot):
    half = n_rot // 2
    j = np.arange(hd)
    p = np.where(j < half, j + half, np.where(j < n_rot, j - half, j))
    return (np.arange(n_heads)[:, None] * hd + p[None, :]).reshape(-1)


def _prep_w_in(w_in):
    o = 0
    seg = {}
    for name, size in (("rq", D_RET), ("rk", D_RET), ("rv", D_RET), ("rg", D_RET), ("nq", D_NSA),
                       ("ck", D_KV), ("cv", D_KV), ("sk", D_KV), ("sv", D_KV), ("wk", D_KV), ("wv", D_KV),
                       ("ng", 3 * H_NSA)):
        seg[name] = (o, size)
        o += size
    cols = lambda n: np.arange(seg[n][0], seg[n][0] + seg[n][1])
    pr = _partner_cols(H_RET, HD_RET, HD_RET)
    pn = _partner_cols(H_NSA, HD_NSA, ROPE_DIMS)
    pk = _partner_cols(G_NSA, HD_NSA, ROPE_DIMS)
    order = np.concatenate([
        cols("rq"), cols("rq")[pr], cols("rk"), cols("rk")[pr], cols("rv"), cols("rg"),
        cols("nq"), cols("nq")[pn],
        cols("ck"), cols("cv"), cols("sk"), cols("sv"), cols("wk"), cols("wv"),
        cols("ck")[pk], cols("sk")[pk], cols("wk")[pk], cols("ng")])
    w = jnp.take(w_in, jnp.asarray(order, jnp.int32), axis=1)
    w = jnp.pad(w, ((0, 0), (0, N_PROJ - w.shape[1])))
    return w.astype(BF16)


def _rot_tables(pos, n_heads, hd, n_rot, theta):
    half = n_rot // 2
    inv = theta ** (-jnp.arange(half, dtype=F32) / half)
    ang = pos.astype(F32)[:, None] * inv[None, :]
    cos, sin = jnp.cos(ang), jnp.sin(ang)
    P = pos.shape[0]
    c = jnp.concatenate([cos, cos, jnp.ones((P, hd - n_rot), F32)], axis=1)
    s = jnp.concatenate([-sin, sin, jnp.zeros((P, hd - n_rot), F32)], axis=1)
    return jnp.tile(c, (1, n_heads)), jnp.tile(s, (1, n_heads))


def _proj_kernel(x_ref, g_ref, w_ref, cr_ref, sr_ref, cn_ref, sn_ref, ck_ref, sk_ref,
                 rq_ref, rk_ref, rv_ref, rg_ref, nq_ref, kv_ref, kvb_ref, gt_ref):
    hn = _rms(x_ref[...], g_ref[...]).astype(BF16)
    seg = lambda c0, n: _dot(hn, w_ref[:, c0:c0 + n])
    cr, sr = cr_ref[...], sr_ref[...]
    rq_ref[...] = (seg(C_RQ, 512) * cr + seg(C_RQR, 512) * sr).astype(BF16)
    rk_ref[...] = ((seg(C_RK, 512) * cr + seg(C_RKR, 512) * sr) * (HD_RET ** -0.5)).astype(BF16)
    rv_ref[...] = seg(C_RV, 512).astype(BF16)
    rg_ref[...] = seg(C_RG, 512)
    nq_ref[...] = (seg(C_NQ, 512) * cn_ref[...] + seg(C_NQR, 512) * sn_ref[...]).astype(BF16)
    ck, sk = ck_ref[...], sk_ref[...]
    for i in range(6):
        a = seg(C_KV + i * D_KV, D_KV)
        if i % 2 == 0:
            a = a * ck + seg(C_KVR + (i // 2) * D_KV, D_KV) * sk
        kv_ref[:, i * D_KV:(i + 1) * D_KV] = a
        kvb_ref[:, i * D_KV:(i + 1) * D_KV] = a.astype(BF16)
    z = seg(C_NG, LANES)
    gt_ref[...] = 1.0 / (1.0 + jnp.exp(-z))


def _proj_call(x2d, g, w_all, tabs, tm):
    n = x2d.shape[0]
    period = tabs[0].shape[0] // tm
    row = lambda w: pl.BlockSpec((tm, w), lambda i: (i, 0))
    tab = lambda w: pl.BlockSpec((tm, w), lambda i: (i % period, 0))
    outs = [(512, BF16), (512, BF16), (512, BF16), (512, F32), (512, BF16), (6 * D_KV, F32), (6 * D_KV, BF16),
            (LANES, F32)]
    return pl.pallas_call(
        _proj_kernel,
        grid=(n // tm,),
        in_specs=[row(D_MODEL), _full((1, D_MODEL)), _full((D_MODEL, N_PROJ)),
                  tab(512), tab(512), tab(512), tab(512), tab(D_KV), tab(D_KV)],
        out_specs=[row(w) for w, _ in outs],
        out_shape=[jax.ShapeDtypeStruct((n, w), dt) for w, dt in outs],
        compiler_params=_cparams(("parallel",)),
        name="proj",
    )(x2d, g.reshape(1, D_MODEL), w_all, *tabs)


def _ret_consts(C):
    lg = jnp.log(1.0 - 2.0 ** (-5.0 - jnp.arange(H_RET, dtype=F32)))
    idx = jnp.arange(C, dtype=F32)
    diff = idx[:, None] - idx[None, :]
    dmat = jnp.where(diff >= 0, jnp.exp(lg[:, None, None] * jnp.maximum(diff, 0.0)), 0.0)
    xi = jnp.exp(lg[None, :] * (idx[:, None] + 1.0))
    zeta = jnp.exp(lg[:, None] * (C - 1.0 - idx[None, :]))
    g_c = jnp.exp(lg * C)
    return dmat, xi, zeta, g_c


def _ret_kernel(gc_ref, q_ref, k_ref, v_ref, rg_ref, gn_ref, dmat_ref, xi_ref, zeta_ref,
                y_ref, st_ref, s_scr):
    c = pl.program_id(1)

    @pl.when(c == 0)
    def _():
        s_scr[...] = jnp.zeros_like(s_scr)

    q, k, v = q_ref[...], k_ref[...], v_ref[...]
    k_t = k.astype(F32).T
    outs = []
    for h in range(H_RET):
        sl = slice(h * HD_RET, (h + 1) * HD_RET)
        qh, kh, vh = q[:, sl], k[:, sl], v[:, sl]
        att = _dot_nt(qh, kh) * dmat_ref[h]
        inner = _dot(att.astype(BF16), vh)
        s_old = s_scr[h]
        cross = _dot(qh, s_old.astype(BF16)) * xi_ref[:, h:h + 1]
        o = inner + cross
        kz = (k_t[sl, :] * zeta_ref[h:h + 1, :]).astype(BF16)
        s_scr[h] = s_old * gc_ref[h] + _dot(kz, vh)
        mu = jnp.mean(o, axis=-1, keepdims=True)
        d = o - mu
        var = jnp.mean(d * d, axis=-1, keepdims=True)
        outs.append(d * lax.rsqrt(var + EPS))
    on = jnp.concatenate(outs, axis=-1)
    rg = rg_ref[...]
    silu = rg * (1.0 / (1.0 + jnp.exp(-rg)))
    y_ref[...] = (silu * (on * gn_ref[...])).astype(BF16)

    @pl.when(c == pl.num_programs(1) - 1)
    def _():
        st_ref[0] = s_scr[...]


def _ret_call(rq, rk, rv, rg, gn, B, L):
    C = RET_CHUNK
    nC = L // C
    dmat, xi, zeta, g_c = _ret_consts(C)
    blk = lambda: pl.BlockSpec((C, D_RET), lambda b, c: (b * nC + c, 0))
    return pl.pallas_call(
        _ret_kernel,
        grid=(B, nC),
        in_specs=[pl.BlockSpec(memory_space=pltpu.SMEM), blk(), blk(), blk(), blk(), _full((1, D_RET)),
                  _full((H_RET, C, C)), _full((C, H_RET)), _full((H_RET, C))],
        out_specs=[blk(), pl.BlockSpec((1, H_RET, HD_RET, HD_RET), lambda b, c: (b, 0, 0, 0))],
        out_shape=[jax.ShapeDtypeStruct((B * L, D_RET), BF16),
                   jax.ShapeDtypeStruct((B, H_RET, HD_RET, HD_RET), F32)],
        scratch_shapes=[pltpu.VMEM((H_RET, HD_RET, HD_RET), F32)],
        compiler_params=_cparams(("parallel", "arbitrary")),
        name="retention",
    )(g_c, rq, rk, rv, rg, gn.reshape(1, D_RET), dmat, xi, zeta)


def _ret1_kernel(q_ref, k_ref, v_ref, s_ref, gam_ref, rg_ref, gn_ref, y_ref, so_ref):
    q, k, v, s, gam = q_ref[...], k_ref[...], v_ref[...], s_ref[...], gam_ref[...]
    qk = jnp.sum(q * k, axis=1, keepdims=True)
    cross = jnp.sum(q * s, axis=1, keepdims=True) * gam
    o = qk * v + cross
    so_ref[...] = s * gam + k * v
    mu = jnp.mean(o, axis=-1, keepdims=True)
    d = o - mu
    var = jnp.mean(d * d, axis=-1, keepdims=True)
    rg = rg_ref[...]
    silu = rg * (1.0 / (1.0 + jnp.exp(-rg)))
    y_ref[...] = silu * (d * lax.rsqrt(var + EPS) * gn_ref[...])


def _ret1_call(rq, rk, rv, rg, gn, state):
    DB = rq.shape[0]
    n = DB * H_RET
    _, _, _, g_c = _ret_consts(1)
    col = lambda t: t.astype(F32).reshape(n, HD_RET, 1)
    rowv = lambda t: t.astype(F32).reshape(n, 1, HD_RET)
    gam = jnp.tile(g_c, DB).reshape(n, 1, 1)
    gn3 = jnp.tile(gn.reshape(H_RET, 1, HD_RET), (DB, 1, 1))
    tb = 128
    b3 = lambda a, b: pl.BlockSpec((tb, a, b), lambda i: (i, 0, 0))
    y, s_new = pl.pallas_call(
        _ret1_kernel,
        grid=(n // tb,),
        in_specs=[b3(HD_RET, 1), b3(HD_RET, 1), b3(1, HD_RET), b3(HD_RET, HD_RET), b3(1, 1), b3(1, HD_RET),
                  b3(1, HD_RET)],
        out_specs=[b3(1, HD_RET), b3(HD_RET, HD_RET)],
        out_shape=[jax.ShapeDtypeStruct((n, 1, HD_RET), F32),
                   jax.ShapeDtypeStruct((n, HD_RET, HD_RET), F32)],
        compiler_params=_cparams(("parallel",)),
        name="retention_step",
    )(col(rq), col(rk), rowv(rv), state.astype(F32).reshape(n, HD_RET, HD_RET), gam, rowv(rg), gn3)
    return y.reshape(DB, D_RET).astype(BF16), s_new.reshape(DB, H_RET, HD_RET, HD_RET)


NSA_TQ = 128
NSA_TK = 512


def _prep_cmp_w(cmp_w):
    z = jnp.zeros_like(cmp_w)
    top = jnp.concatenate([cmp_w, z], axis=-1)
    bot = jnp.concatenate([z, cmp_w], axis=-1)
    return jnp.concatenate([top, bot], axis=-2).astype(BF16)


def _iota(shape, dim):
    return lax.broadcasted_iota(jnp.int32, shape, dim)


def _group_queries(nqf, g, tq):
    lane_g = _iota((tq, LANES), 1) // HD_NSA
    parts = []
    for r in range(R_NSA):
        h = g * R_NSA + r
        x = nqf[:, (h // 2) * LANES:(h // 2 + 1) * LANES]
        if h % 2 != g:
            x = pltpu.roll(x, HD_NSA, axis=1)
        parts.append((jnp.where(lane_g == g, x, 0.0) * (HD_NSA ** -0.5)).astype(BF16))
    return parts


def _select_blocks(score, nb, n_sel):
    s_t = score.T[:nb, :]
    n_i = _iota(s_t.shape, 0)
    rank = jnp.zeros(s_t.shape, F32)
    for m in range(nb):
        row = s_t[m:m + 1, :]
        ahead = (row > s_t) | ((row == s_t) & (n_i > m))
        rank = rank + jnp.where(ahead, 1.0, 0.0)
    sel_t = jnp.where((rank < n_sel) & (s_t >= 0.0), 1.0, 0.0)
    sel_t = jnp.concatenate([sel_t, jnp.zeros((LANES - nb, s_t.shape[1]), F32)], axis=0)
    return sel_t.T


def _flash_run(q_all, k_ref, v_ref, bias_ref, n_steps, tq, tk):
    def body(j, carry):
        m, l, acc = carry
        rows = pl.ds(pl.multiple_of(j * tk, tk), tk)
        s = _dot_nt(q_all, k_ref[rows, :]).reshape(G_NSA, R_NSA, tq, tk) + bias_ref[j][:, None]
        s = s.reshape(H_NSA, tq, tk)
        m_new = jnp.maximum(m, jnp.max(s, axis=-1, keepdims=True))
        alpha = jnp.exp(m - m_new)
        p = jnp.exp(s - m_new)
        l = alpha * l + jnp.sum(p, axis=-1, keepdims=True)
        pv = _dot(p.reshape(H_NSA * tq, tk).astype(BF16), v_ref[rows, :]).reshape(H_NSA, tq, LANES)
        return m_new, l, alpha * acc + pv

    init = (jnp.full((H_NSA, tq, 1), NEG, F32), jnp.zeros((H_NSA, tq, 1), F32),
            jnp.zeros((H_NSA, tq, LANES), F32))
    _, l, acc = lax.fori_loop(0, n_steps, body, init)
    return acc / jnp.maximum(l, 1e-30)


def _window_bias(tq):
    n = WINDOW // tq + 1
    dist = (jnp.arange(tq)[None, :, None] - jnp.arange(WINDOW + tq)[None, None, :]
            + jnp.arange(n)[:, None, None] * tq)
    return jnp.where((dist >= 0) & (dist < WINDOW), 0.0, NEG).astype(F32)


def _nsa_kernel(nq_ref, ckf_ref, cvf_ref, sk_ref, sv_ref, wk_ref, wv_ref, gt_ref, cw_ref, cb_ref, wb_ref,
                o_ref, ck_scr, cv_scr, bias_scr, *, nb):
    qi = pl.program_id(1)
    tq, tk = NSA_TQ, NSA_TK
    n_sel = min(N_SEL, nb)

    @pl.when(qi == 0)
    def _compress():
        for which, (src, dst) in enumerate(((ckf_ref, ck_scr), (cvf_ref, cv_scr))):
            def body(j, acc):
                x = src[pl.ds(j, nb, stride=CMP_BLK), :].astype(BF16)
                return acc + _dot(x, cw_ref[which, j])
            acc = lax.fori_loop(0, CMP_BLK, body, jnp.zeros((nb, LANES), F32))
            dst[...] = jnp.zeros_like(dst)
            dst[0:nb, :] = (acc + cb_ref[which]).astype(BF16)

    t0 = qi * tq
    nqf = nq_ref[...].astype(F32)
    gt = gt_ref[...]
    pos = t0 + _iota((tq, 1), 0)
    blk = _iota((1, LANES), 1)
    vis = ((blk * CMP_BLK + CMP_BLK - 1) <= pos) & (blk < nb)
    forced = ((blk == 0) | (blk == pos // CMP_BLK)) & (blk < nb)
    vis_bias = jnp.where(vis, 0.0, NEG)
    lane = _iota((tq, LANES), 1)
    kcol = _iota((1, tk), 1)
    blk_row = _iota((LANES, tk), 0)
    blk_of_key = _iota((LANES, tk), 1) // CMP_BLK

    n_kv = (t0 + tq + tk - 1) // tk

    q_all = jnp.concatenate(_group_queries(nqf, 0, tq) + _group_queries(nqf, 1, tq), axis=0)

    sc = _dot_nt(q_all, ck_scr[...]).reshape(H_NSA, tq, LANES) + vis_bias[None]
    pc = jnp.where(sc > 0.5 * NEG, jnp.exp(sc - jnp.max(sc, axis=-1, keepdims=True)), 0.0)
    pc = pc / jnp.maximum(jnp.sum(pc, axis=-1, keepdims=True), 1e-30)
    o_cmp = _dot(pc.reshape(H_NSA * tq, LANES).astype(BF16), cv_scr[...]).reshape(H_NSA, tq, LANES)

    sels = []
    for g in range(G_NSA):
        imp = pc[g * R_NSA] + pc[g * R_NSA + 1] + pc[g * R_NSA + 2] + pc[g * R_NSA + 3]
        score = jnp.where(forced, SEL_FORCE, jnp.where(vis, imp, -1.0))
        score = jnp.where(blk < nb, score, -2.0)
        sels.append(_select_blocks(score, nb, n_sel).astype(BF16))

    def mask_tile(j, _):
        expand = jnp.where(blk_row == blk_of_key + j * (tk // CMP_BLK), 1.0, 0.0).astype(BF16)
        causal = j * tk + kcol <= pos
        for g in range(G_NSA):
            chosen = _dot(sels[g], expand)
            bias_scr[j, g] = jnp.where((chosen > 0.5) & causal, 0.0, NEG)
        return 0

    lax.fori_loop(0, n_kv, mask_tile, 0)
    o_slc = _flash_run(q_all, sk_ref, sv_ref, bias_scr, n_kv, tq, tk)

    slab = pl.ds(pl.multiple_of(jnp.maximum(t0 - WINDOW, 0), tq), WINDOW + tq)
    wbias = wb_ref[jnp.minimum(qi, WINDOW // tq)]
    sw = _dot_nt(q_all, wk_ref[slab, :]).reshape(H_NSA, tq, WINDOW + tq) + wbias[None]
    pw = jnp.exp(sw - jnp.max(sw, axis=-1, keepdims=True))
    o_win = (_dot(pw.reshape(H_NSA * tq, WINDOW + tq).astype(BF16), wv_ref[slab, :]).reshape(H_NSA, tq, LANES)
             / jnp.maximum(jnp.sum(pw, axis=-1, keepdims=True), 1e-30))

    for c in range(H_NSA // 2):
        pair = []
        for h in (2 * c, 2 * c + 1):
            o_h = (gt[:, 3 * h:3 * h + 1] * o_cmp[h] + gt[:, 3 * h + 1:3 * h + 2] * o_slc[h]
                   + gt[:, 3 * h + 2:3 * h + 3] * o_win[h])
            if h % 2 != h // R_NSA:
                o_h = pltpu.roll(o_h, HD_NSA, axis=1)
            pair.append(o_h)
        o_ref[:, c * LANES:(c + 1) * LANES] = jnp.where(lane < HD_NSA, pair[0], pair[1]).astype(BF16)


def _nsa_call(nq, kv, kvb, gates, cw_bd, cb2, B, L):
    tq = NSA_TQ
    nQ = L // tq
    nb = L // CMP_BLK
    assert nb <= LANES and L % NSA_TK == 0 and NSA_TK % tq == 0 and WINDOW % tq == 0 and L >= WINDOW + tq
    win_bias = _window_bias(tq)
    rows = lambda w: pl.BlockSpec((tq, w), lambda b, q: (b * nQ + q, 0))
    seq = lambda c: pl.BlockSpec((L, LANES), lambda b, q: (b, c))
    return pl.pallas_call(
        functools.partial(_nsa_kernel, nb=nb),
        grid=(B, nQ),
        in_specs=[rows(D_NSA), seq(0), seq(1), seq(2), seq(3), seq(4), seq(5), rows(LANES),
                  _full((2, CMP_BLK, LANES, LANES)), _full((2, 1, LANES)), _full(win_bias.shape)],
        out_specs=rows(D_NSA),
        out_shape=jax.ShapeDtypeStruct((B * L, D_NSA), BF16),
        scratch_shapes=[pltpu.VMEM((LANES, LANES), BF16), pltpu.VMEM((LANES, LANES), BF16),
                        pltpu.VMEM((L // NSA_TK, G_NSA, tq, NSA_TK), F32)],
        compiler_params=_cparams(("parallel", "arbitrary")),
        name="nsa_prompt",
    )(nq, kv, kv, kvb, kvb, kvb, kvb, gates, cw_bd, cb2, win_bias)


def _normmm_kernel(x_ref, g_ref, w_ref, o_ref, ob_ref):
    y = _dot(_rms(x_ref[...], g_ref[...]).astype(BF16), w_ref[...])
    o_ref[...] = y
    ob_ref[...] = y.astype(BF16)


def _normmm_call(x2d, g, w, tm):
    n, d = x2d.shape
    m = w.shape[1]
    return pl.pallas_call(
        _normmm_kernel,
        grid=(n // tm,),
        in_specs=[pl.BlockSpec((tm, d), lambda i: (i, 0)), _full((1, d)), _full((d, m))],
        out_specs=[pl.BlockSpec((tm, m), lambda i: (i, 0))] * 2,
        out_shape=[jax.ShapeDtypeStruct((n, m), F32), jax.ShapeDtypeStruct((n, m), BF16)],
        compiler_params=_cparams(("parallel",)),
        name="norm_matmul",
    )(x2d, g.reshape(1, d), w)


def _mixout_kernel(yr_ref, on_ref, h_ref, wo_ref, g_ref, wq_ref, h1_ref, mq_ref):
    h1 = h_ref[...] + _dot(yr_ref[...], wo_ref[0:D_RET, :]) + _dot(on_ref[...], wo_ref[D_RET:, :])
    h1_ref[...] = h1
    mq_ref[...] = _dot(_rms(h1, g_ref[...]).astype(BF16), wq_ref[...]).astype(BF16)


def _mixout_call(yret, onsa, h, w_out, g_mem, w_mq, tm):
    n = h.shape[0]
    row = lambda w: pl.BlockSpec((tm, w), lambda i: (i, 0))
    return pl.pallas_call(
        _mixout_kernel,
        grid=(n // tm,),
        in_specs=[row(D_RET), row(D_NSA), row(D_MODEL), _full((D_RET + D_NSA, D_MODEL)), _full((1, D_MODEL)),
                  _full((D_MODEL, D_MEM))],
        out_specs=[row(D_MODEL), row(D_MEM)],
        out_shape=[jax.ShapeDtypeStruct((n, D_MODEL), F32), jax.ShapeDtypeStruct((n, D_MEM), BF16)],
        compiler_params=_cparams(("parallel",)),
        name="mixer_out",
    )(yret, onsa, h, w_out, g_mem.reshape(1, D_MODEL), w_mq)


def _memattn_kernel(q_ref, k_ref, v_ref, o_ref):
    q, k, v = q_ref[...], k_ref[...], v_ref[...]
    for h in range(H_MEM):
        sl = slice(h * HD_MEM, (h + 1) * HD_MEM)
        s = _dot_nt(q[:, sl], k[:, sl]) * (HD_MEM ** -0.5)
        p = jnp.exp(s - jnp.max(s, axis=-1, keepdims=True))
        p = p / jnp.sum(p, axis=-1, keepdims=True)
        o_ref[:, sl] = _dot(p.astype(BF16), v[:, sl]).astype(BF16)


def _memattn_call(mq, mkvb, B, L, n_mem, tm):
    nT = L // tm
    return pl.pallas_call(
        _memattn_kernel,
        grid=(B, nT),
        in_specs=[pl.BlockSpec((tm, D_MEM), lambda b, i: (b * nT + i, 0)),
                  pl.BlockSpec((n_mem, D_MEM), lambda b, i: (b, 0)),
                  pl.BlockSpec((n_mem, D_MEM), lambda b, i: (b, 1))],
        out_specs=pl.BlockSpec((tm, D_MEM), lambda b, i: (b * nT + i, 0)),
        out_shape=jax.ShapeDtypeStruct((B * L, D_MEM), BF16),
        compiler_params=_cparams(("parallel", "parallel")),
        name="mem_attention",
    )(mq, mkvb, mkvb)


def _memattn1_kernel(q_ref, k_ref, v_ref, o_ref):
    tb = q_ref.shape[0]
    for b in range(tb):
        q = q_ref[b]
        prod = k_ref[b] * q
        outs = []
        for h in range(H_MEM):
            sl = slice(h * HD_MEM, (h + 1) * HD_MEM)
            s = jnp.sum(prod[:, sl], axis=-1, keepdims=True) * (HD_MEM ** -0.5)
            p = jnp.exp(s - jnp.max(s, axis=0, keepdims=True))
            p = p / jnp.sum(p, axis=0, keepdims=True)
            outs.append(jnp.sum(p * v_ref[b][:, sl], axis=0, keepdims=True))
        o_ref[b] = jnp.concatenate(outs, axis=-1)


def _memattn1_call(mq, cache_k, cache_v):
    DB, n_mem = cache_k.shape[0], cache_k.shape[1]
    tb = 8
    blk = pl.BlockSpec((tb, n_mem, D_MEM), lambda i: (i, 0, 0))
    q3 = pl.BlockSpec((tb, 1, D_MEM), lambda i: (i, 0, 0))
    o = pl.pallas_call(
        _memattn1_kernel,
        grid=(DB // tb,),
        in_specs=[q3, blk, blk],
        out_specs=q3,
        out_shape=jax.ShapeDtypeStruct((DB, 1, D_MEM), F32),
        compiler_params=_cparams(("parallel",)),
        name="mem_attention_step",
    )(mq.astype(F32).reshape(DB, 1, D_MEM), cache_k.reshape(DB, n_mem, D_MEM), cache_v.reshape(DB, n_mem, D_MEM))
    return o.reshape(DB, D_MEM).astype(BF16)


def _prep_peer_keys(subkeys):
    half = PEER_DKEY // 2
    z = jnp.zeros_like(subkeys[:, 0])
    k0 = jnp.concatenate([subkeys[:, 0], z], axis=-1)
    k1 = jnp.concatenate([z, subkeys[:, 1]], axis=-1)
    return jnp.concatenate([k0, k1], axis=1).astype(BF16)


def _top_rows(x, k, payload=None):
    n = x.shape[-2]
    ri = _iota(x.shape, x.ndim - 2)
    vals, picks = [], []
    for _ in range(k):
        m = jnp.max(x, axis=-2, keepdims=True)
        i = jnp.min(jnp.where(x == m, ri, n), axis=-2, keepdims=True)
        hit = ri == i
        vals.append(m)
        picks.append(i if payload is None else jnp.max(jnp.where(hit, payload, -1), axis=-2, keepdims=True))
        x = jnp.where(hit, -jnp.inf, x)
    return jnp.concatenate(vals, axis=-2), jnp.concatenate(picks, axis=-2)


def _route_kernel(om_ref, h1_ref, wo_ref, g_ref, wq_ref, sk_ref, h2_ref, xn_ref, idx_ref, gw_ref,
                  idx_scr, gw_scr):
    tm = h1_ref.shape[0]
    h2 = h1_ref[...] + _dot(om_ref[...], wo_ref[...])
    h2_ref[...] = h2
    xn = _rms(h2, g_ref[...])
    xn_ref[...] = xn
    xb = xn.astype(BF16)

    k = PEER_TOPK
    n_b = [k // (a + 1) for a in range(k)]
    pad = -sum(n_b) % 8
    lanes = min(tm, LANES)

    def head(h, _):
        pq = _dot(xb, wq_ref[h]).astype(BF16)
        s_all = _dot_nt(sk_ref[h], pq)
        rows = pl.ds(pl.multiple_of(h * k, k), k)
        for c in range(tm // lanes):
            s = s_all[:, c * lanes:(c + 1) * lanes].reshape(2, PEER_KEYS, lanes)
            v12, i12 = _top_rows(s, k)
            cand = jnp.concatenate([v12[0][a:a + 1] + v12[1][0:n_b[a]] for a in range(k)]
                                   + [jnp.full((pad, lanes), -jnp.inf, F32)], axis=0)
            cidx = jnp.concatenate([i12[0][a:a + 1] * PEER_KEYS + i12[1][0:n_b[a]] for a in range(k)]
                                   + [jnp.full((pad, lanes), -1, jnp.int32)], axis=0)
            top, expert = _top_rows(cand, k, payload=cidx)
            e = jnp.exp(top - top[0:1, :])
            idx_scr[rows, c * lanes:(c + 1) * lanes] = expert * PEER_ROW
            gw_scr[rows, c * lanes:(c + 1) * lanes] = e / jnp.sum(e, axis=0, keepdims=True)
        return 0

    lax.fori_loop(0, PEER_HEADS, head, 0)
    idx_ref[...] = idx_scr[...].T
    gw_ref[...] = gw_scr[...].T


def _route_call(omem, h1, w_mo, g_ffn, wq_h, sk_pad, tm):
    n = h1.shape[0]
    nk = PEER_HEADS * PEER_TOPK
    row = lambda w: pl.BlockSpec((tm, w), lambda i: (i, 0))
    return pl.pallas_call(
        _route_kernel,
        grid=(n // tm,),
        in_specs=[row(D_MEM), row(D_MODEL), _full((D_MEM, D_MODEL)), _full((1, D_MODEL)),
                  _full((PEER_HEADS, D_MODEL, PEER_DKEY)), _full((PEER_HEADS, 2 * PEER_KEYS, PEER_DKEY))],
        out_specs=[row(D_MODEL), row(D_MODEL), row(nk), row(nk)],
        out_shape=[jax.ShapeDtypeStruct((n, D_MODEL), F32), jax.ShapeDtypeStruct((n, D_MODEL), F32),
                   jax.ShapeDtypeStruct((n, nk), jnp.int32), jax.ShapeDtypeStruct((n, nk), F32)],
        scratch_shapes=[pltpu.VMEM((nk, tm), jnp.int32), pltpu.VMEM((nk, tm), F32)],
        compiler_params=_cparams(("parallel",)),
        name="peer_route",
    )(omem, h1, w_mo, g_ffn.reshape(1, D_MODEL), wq_h, sk_pad)


def _gelu_tanh(x):
    return 0.5 * x * (1.0 + jnp.tanh(0.7978845608028654 * (x + 0.044715 * x * x * x)))


PEER_TT = 128


def _pack_table(t):
    e, d = t.shape
    b = lax.bitcast_convert_type(t.astype(BF16), jnp.uint16).astype(jnp.uint32)
    w = b[:, :d // 2] | (b[:, d // 2:] << 16)
    return w.reshape(e * PEER_ROW, LANES)


def _expert_row(tab_ref, off):
    w = tab_ref[pl.ds(pl.multiple_of(off, PEER_ROW), PEER_ROW), :]
    lo = pltpu.bitcast(w << 16, F32)
    hi = pltpu.bitcast(w & jnp.uint32(0xFFFF0000), F32)
    return lo, hi


def _peer_act_kernel(idx_ref, x_ref, gw_ref, tab_ref, c_ref, part_scr, act_scr):
    tt, nk = gw_ref.shape
    sub = _iota((8, LANES), 0)
    keep_pairs = (sub % 4) < 2
    keep_even = (sub % 2) == 0
    feed = (0, 4, 2, 6, 1, 5, 3, 7)

    def row_sums8(p):
        p = [p[i] for i in feed]
        v = [jnp.concatenate([p[2 * i], p[2 * i + 1]], axis=0) for i in range(4)]
        w = [x + pltpu.roll(x, 6, axis=0) for x in v]
        u = [jnp.where(keep_pairs, w[2 * i], pltpu.roll(w[2 * i + 1], 2, axis=0)) for i in range(2)]
        z = [x + pltpu.roll(x, 7, axis=0) for x in u]
        return jnp.where(keep_even, z[0], pltpu.roll(z[1], 1, axis=0))

    def lane_sums(t):
        act_scr[pl.ds(t, 1), :] = jnp.sum(part_scr[t].T, axis=0, keepdims=True)

    part_scr[0] = jnp.zeros((nk, LANES), F32)

    def token(t, _):
        lane_sums(jnp.maximum(t - 1, 0))
        xb = pltpu.bitcast(x_ref[t].astype(BF16).astype(F32), jnp.uint32)
        x_pk = pltpu.bitcast((xb[0:PEER_ROW] >> 16) | xb[PEER_ROW:], BF16)
        hi_mask = jnp.uint32(0xFFFF0000)
        for j0 in range(0, nk, 8):
            prods = []
            for j in range(j0, j0 + 8):
                w = tab_ref[pl.ds(pl.multiple_of(idx_ref[t * nk + j], PEER_ROW), PEER_ROW), :]
                pp = pltpu.bitcast(pltpu.bitcast(w, BF16) * x_pk, jnp.uint32)
                prods.append(pltpu.bitcast(pp << 16, F32) + pltpu.bitcast(pp & hi_mask, F32))
            part_scr[t, j0:j0 + 8, :] = row_sums8(prods)
        return 0

    lax.fori_loop(0, tt, token, 0)
    lane_sums(tt - 1)
    c_ref[...] = gw_ref[...] * _gelu_tanh(act_scr[...])


def _peer_out_kernel(idx_ref, c_ref, h2_ref, gf_ref, tab_ref, y_ref, splat_scr):
    tt, nk = c_ref.shape
    n_acc = 4

    def splat(buf, t):
        splat_scr[buf] = jnp.broadcast_to(c_ref[pl.ds(t, 1), :], (nk, nk)).T

    def accumulate(buf, t):
        acc_lo = [jnp.zeros((PEER_ROW, LANES), F32)] * n_acc
        acc_hi = [jnp.zeros((PEER_ROW, LANES), F32)] * n_acc
        for j in range(nk):
            lo, hi = _expert_row(tab_ref, idx_ref[t * nk + j])
            c = splat_scr[buf, j:j + 1, :]
            acc_lo[j % n_acc] = acc_lo[j % n_acc] + c * lo
            acc_hi[j % n_acc] = acc_hi[j % n_acc] + c * hi
        tree = lambda v: v[0] if len(v) == 1 else tree([a + b for a, b in zip(v[0::2], v[1::2])])
        y_ref[t] = h2_ref[t] + jnp.concatenate([tree(acc_lo), tree(acc_hi)], axis=0)

    splat(0, 0)

    def token_pair(i, _):
        t = 2 * i
        splat(1, t + 1)
        accumulate(0, t)
        splat(0, jnp.minimum(t + 2, tt - 1))
        accumulate(1, t + 1)
        return 0

    lax.fori_loop(0, tt // 2, token_pair, 0)
    h3 = y_ref[...]
    ms = jnp.sum(jnp.sum(h3 * h3, axis=2, keepdims=True), axis=1, keepdims=True) * (1.0 / D_MODEL)
    y_ref[...] = h3 * lax.rsqrt(ms + EPS) * gf_ref[...]


def _peer_call(idx, xn, gw, h2, g_final, u_tab, v_tab, tt):
    n, nk = idx.shape
    assert n % tt == 0 and nk % 8 == 0 and tt % 2 == 0
    smem = lambda: pl.BlockSpec((tt * nk,), lambda i: (i,), memory_space=pltpu.SMEM)
    idx = idx.reshape(n * nk)
    tile = lambda: pl.BlockSpec((tt, 8, LANES), lambda i: (i, 0, 0))
    table = lambda t: pl.BlockSpec(t.shape, lambda i: (0, 0), pipeline_mode=pl.Buffered(1))
    as_tiles = lambda a: a.reshape(n, 8, LANES)
    c = pl.pallas_call(
        _peer_act_kernel,
        grid=(n // tt,),
        in_specs=[smem(), tile(), pl.BlockSpec((tt, nk), lambda i: (i, 0)), table(u_tab)],
        out_specs=pl.BlockSpec((tt, nk), lambda i: (i, 0)),
        out_shape=jax.ShapeDtypeStruct((n, nk), F32),
        scratch_shapes=[pltpu.VMEM((tt, nk, LANES), F32), pltpu.VMEM((tt, nk), F32)],
        compiler_params=_cparams(("arbitrary",)),
        name="peer_act",
    )(idx, as_tiles(xn), gw, u_tab)
    y = pl.pallas_call(
        _peer_out_kernel,
        grid=(n // tt,),
        in_specs=[smem(), pl.BlockSpec((tt, nk), lambda i: (i, 0)), tile(), _full((8, LANES)), table(v_tab)],
        out_specs=tile(),
        out_shape=jax.ShapeDtypeStruct((n, 8, LANES), F32),
        scratch_shapes=[pltpu.VMEM((2, nk, nk), F32)],
        compiler_params=_cparams(("arbitrary",)),
        name="peer_out",
    )(idx, c, as_tiles(h2), g_final.reshape(8, LANES), v_tab)
    return y.reshape(n, D_MODEL)


CMP_PAGES = 128


def _cmp_pages_kernel(x_hbm, w_ref, b_ref, o_ref, buf, sem):
    i = pl.program_id(0)
    n_pages = o_ref.shape[0]

    def start(step, slot):
        def body(d, _):
            for g in range(G_NSA):
                pltpu.make_async_copy(x_hbm.at[pl.ds(step * n_pages, n_pages), g, d, :], buf.at[slot, d, g],
                                      sem.at[slot]).start()
            return 0
        lax.fori_loop(0, HD_NSA, body, 0)

    @pl.when(i == 0)
    def _():
        start(0, 0)

    @pl.when(i + 1 < pl.num_programs(0))
    def _():
        start(i + 1, (i + 1) % 2)

    slot = i % 2
    pltpu.make_async_copy(buf.at[slot], buf.at[slot], sem.at[slot]).wait()
    n_chain = 4

    def body(k, accs):
        out = []
        for g in range(G_NSA):
            for c in range(n_chain):
                d = k * n_chain + c
                out.append(accs[g * n_chain + c] + _dot(buf[slot, d, g].astype(BF16), w_ref[d]))
        return tuple(out)

    zero = jnp.zeros((n_pages, LANES), F32)
    accs = lax.fori_loop(0, HD_NSA // n_chain, body, (zero,) * (G_NSA * n_chain))
    for g in range(G_NSA):
        acc = accs[g * n_chain]
        for c in range(1, n_chain):
            acc = acc + accs[g * n_chain + c]
        o_ref[:, g * LANES:(g + 1) * LANES] = acc + b_ref[...]


def _cmp_pages_call(pool_t, w_bd, b):
    n_phys = pool_t.shape[0]
    assert n_phys % CMP_PAGES == 0
    out = pl.pallas_call(
        _cmp_pages_kernel,
        grid=(n_phys // CMP_PAGES,),
        in_specs=[pl.BlockSpec(memory_space=pl.ANY), _full((HD_NSA, PAGE_SIZE, LANES)), _full((1, LANES))],
        out_specs=pl.BlockSpec((CMP_PAGES, G_NSA * LANES), lambda i: (i, 0)),
        out_shape=jax.ShapeDtypeStruct((n_phys, G_NSA * LANES), F32),
        scratch_shapes=[pltpu.VMEM((2, HD_NSA, G_NSA, CMP_PAGES, PAGE_SIZE), F32), pltpu.SemaphoreType.DMA((2,))],
        compiler_params=_cparams(("arbitrary",)),
        name="compress_pages",
    )(pool_t, w_bd, jnp.tile(b, 2).reshape(1, LANES))
    return out.reshape(n_phys * G_NSA, LANES)


def _nsa1_cmp_kernel(pt_ref, nq_ref, ckn_ref, cvn_ref, tk_ref, tv_ref, cw_ref, cb_ref, ocmp_ref, idx_ref,
                     kg_scr, vg_scr, *, n_pages, q_pos):
    tb = nq_ref.shape[0]
    nb_past = n_pages * (PAGE_SIZE // CMP_BLK)
    base = pl.program_id(0) * tb
    lane = _iota((1, LANES), 1)
    blk_n = 2 * (lane % HD_NSA) + lane // HD_NSA
    forced = (blk_n == 0) | (blk_n == q_pos // CMP_BLK)
    vis = (blk_n * CMP_BLK + CMP_BLK - 1) <= q_pos
    new_vis = (nb_past * CMP_BLK + CMP_BLK - 1) <= q_pos
    new_forced = nb_past == q_pos // CMP_BLK
    n_row = jnp.broadcast_to(blk_n, (LANES, LANES))
    n_col = 2 * (_iota((LANES, LANES), 0) % HD_NSA) + _iota((LANES, LANES), 0) // HD_NSA
    lane8 = _iota((1, LANES), 1) // HD_NSA

    def sample(b, _):
        qrow = nq_ref[pl.ds(b, 1), :].astype(F32)
        new_k = _dot(ckn_ref[pl.ds(b, 1), :].astype(BF16), cw_ref[0]) + cb_ref[0]
        new_v = _dot(cvn_ref[pl.ds(b, 1), :].astype(BF16), cw_ref[1]) + cb_ref[1]
        o_row, idx_row = [], jnp.full((1, LANES), -1, jnp.int32)
        for g in range(G_NSA):
            def gather(i, _):
                r = pt_ref[base + b, i] * G_NSA + g
                kg_scr[pl.ds(i, 1), :] = tk_ref[pl.ds(r, 1), :]
                vg_scr[pl.ds(i, 1), :] = tv_ref[pl.ds(r, 1), :]
                return 0
            lax.fori_loop(0, n_pages, gather, 0)
            rows = []
            for c in range(2):
                for r in range(R_NSA):
                    h = g * R_NSA + r
                    x = qrow[:, (h // 2) * LANES:(h // 2 + 1) * LANES]
                    if h % 2 != c:
                        x = pltpu.roll(x, HD_NSA, axis=1)
                    rows.append(jnp.where(lane8 == c, x, 0.0))
            qpad = jnp.concatenate(rows, axis=0)
            s = _dot_nt(qpad.astype(BF16), kg_scr[...].astype(BF16)) * (HD_NSA ** -0.5)
            s3 = s.reshape(2, R_NSA, n_pages)
            qg = qpad[R_NSA * g:R_NSA * (g + 1), :]
            nk_g = jnp.where(lane8 == g, new_k.astype(BF16).astype(F32), 0.0)
            s_new = jnp.sum(qg.astype(BF16).astype(F32) * nk_g, axis=-1, keepdims=True) * (HD_NSA ** -0.5)
            s_new = jnp.where(new_vis, s_new, NEG)[None]
            m = jnp.maximum(jnp.max(jnp.max(s3, axis=2, keepdims=True), axis=0, keepdims=True), s_new)
            p = jnp.exp(s3 - m)
            p_new = jnp.where(s_new > 0.5 * NEG, jnp.exp(s_new - m), 0.0)
            l = jnp.sum(jnp.sum(p, axis=2, keepdims=True), axis=0, keepdims=True) + p_new
            inv = 1.0 / jnp.maximum(l, 1e-30)
            p = p * inv
            p_new = p_new * inv
            res = _dot(p.reshape(2 * R_NSA, n_pages).astype(BF16), vg_scr[...].astype(BF16))
            o4 = res[0:R_NSA] + pltpu.roll(res[R_NSA:], HD_NSA, axis=1)
            nv_g = new_v.astype(BF16).astype(F32)
            if g == 1:
                nv_g = pltpu.roll(nv_g, HD_NSA, axis=1)
            o4 = o4 + p_new[0].astype(BF16).astype(F32) * nv_g
            o_row += [o4[r:r + 1, 0:HD_NSA] for r in range(R_NSA)]
            imp2 = jnp.sum(p, axis=1)
            imp = jnp.concatenate([imp2[0:1], imp2[1:2]], axis=-1)
            score = jnp.where(forced, SEL_FORCE, jnp.where(vis, imp, -1.0))
            imp_new = jnp.sum(p_new)
            sc_new = SEL_FORCE if new_forced else jnp.where(new_vis, imp_new, -1.0)
            a = jnp.broadcast_to(score, (LANES, LANES))
            bt = a.T
            ahead = (bt > a) | ((bt == a) & (n_col < n_row))
            rank = jnp.sum(jnp.where(ahead, 1.0, 0.0), axis=0, keepdims=True) + jnp.where(sc_new > score, 1.0, 0.0)
            rank_new = jnp.sum(jnp.where(score >= sc_new, 1.0, 0.0))
            for r in range(N_SEL):
                hit = (rank == r) & (score >= 0.0)
                val = jnp.sum(jnp.where(hit, blk_n + 1, 0)) - 1
                val = jnp.where((rank_new == r) & (sc_new >= 0.0), nb_past, val)
                idx_row = jnp.where(lane == g * N_SEL + r, val, idx_row)
        ocmp_ref[pl.ds(b, 1), :] = jnp.concatenate(o_row, axis=-1)
        idx_ref[pl.ds(b, 1), :] = idx_row
        return 0

    lax.fori_loop(0, tb, sample, 0, unroll=2)


def _nsa1_cmp_call(page_table, nq, ck_new, cv_new, tbl_k, tbl_v, cw_bd, cb2, q_pos):
    DB, n_pages = page_table.shape
    assert n_pages * (PAGE_SIZE // CMP_BLK) == LANES
    tb = 8
    row = lambda w: pl.BlockSpec((tb, w), lambda i, pt: (i, 0))
    whole = lambda a: pl.BlockSpec(a.shape, lambda i, pt: (0,) * a.ndim, pipeline_mode=pl.Buffered(1))
    return pl.pallas_call(
        functools.partial(_nsa1_cmp_kernel, n_pages=n_pages, q_pos=q_pos),
        grid_spec=pltpu.PrefetchScalarGridSpec(
            num_scalar_prefetch=1,
            grid=(DB // tb,),
            in_specs=[row(D_NSA), row(LANES), row(LANES), whole(tbl_k), whole(tbl_v),
                      pl.BlockSpec((2, LANES, LANES), lambda i, pt: (0, 0, 0)),
                      pl.BlockSpec((2, 1, LANES), lambda i, pt: (0, 0, 0))],
            out_specs=[row(D_NSA), row(LANES)],
            scratch_shapes=[pltpu.VMEM((n_pages, LANES), F32), pltpu.VMEM((n_pages, LANES), F32)]),
        out_shape=[jax.ShapeDtypeStruct((DB, D_NSA), F32), jax.ShapeDtypeStruct((DB, LANES), jnp.int32)],
        compiler_params=_cparams(("arbitrary",)),
        name="nsa_step_compressed",
    )(page_table, nq.astype(F32), ck_new, cv_new, tbl_k, tbl_v, cw_bd[:, 0], cb2)


def _nsa1_attn_kernel(idx_ref, pt_ref, nq_ref, kvn_ref, gt_ref, ocmp_ref, wk_ref, wv_ref, sk_hbm, sv_hbm,
                      o_ref, kbuf, vbuf, sem, *, n_pages, q_pos, past_len):
    b = pl.program_id(0)
    bpp = PAGE_SIZE // CMP_BLK
    nb_past = n_pages * bpp
    wb = wk_ref.shape[3]

    def block_copies(bb, slot, g, r):
        n = idx_ref[bb, g * N_SEL + r]
        past = (n >= 0) & (n < nb_past)
        page = pt_ref[bb, jnp.clip(n, 0, nb_past - 1) // bpp]
        ck = pltpu.make_async_copy(sk_hbm.at[page, g], kbuf.at[slot, g, r], sem.at[slot, 0])
        cv = pltpu.make_async_copy(sv_hbm.at[page, g], vbuf.at[slot, g, r], sem.at[slot, 1])
        return past, ck, cv

    def issue(bb, slot):
        for g in range(G_NSA):
            for r in range(N_SEL):
                past, ck, cv = block_copies(bb, slot, g, r)

                @pl.when(past)
                def _():
                    ck.start()
                    cv.start()

                @pl.when(jnp.logical_not(past))
                def _():
                    kbuf[slot, g, r] = jnp.zeros((HD_NSA, PAGE_SIZE), F32)
                    vbuf[slot, g, r] = jnp.zeros((HD_NSA, PAGE_SIZE), F32)

    @pl.when(b == 0)
    def _():
        issue(0, 0)

    @pl.when(b + 1 < pl.num_programs(0))
    def _():
        issue(b + 1, (b + 1) % 2)

    slot = b % 2
    for g in range(G_NSA):
        for r in range(N_SEL):
            past, ck, cv = block_copies(b, slot, g, r)

            @pl.when(past)
            def _():
                ck.wait()
                cv.wait()

    qrow = nq_ref[0].astype(F32)
    kvn = kvn_ref[0]
    gt = gt_ref[0]
    ocmp = ocmp_ref[0]
    scale = HD_NSA ** -0.5
    blk_in_page = _iota((1, PAGE_SIZE), 1) // CMP_BLK
    wpos = past_len - wb + _iota((1, wb), 1)
    wdist = q_pos - wpos
    w_ok = (wdist >= 0) & (wdist < WINDOW) & (wpos >= 0)
    bfr = lambda t: t.astype(BF16).astype(F32)
    pieces = []
    for g in range(G_NSA):
        q4 = jnp.concatenate([qrow[:, (g * R_NSA + r) * HD_NSA:(g * R_NSA + r + 1) * HD_NSA]
                              for r in range(R_NSA)], axis=0)
        q4b = q4.astype(BF16)
        new = lambda i: kvn[:, i * D_KV + g * HD_NSA:i * D_KV + (g + 1) * HD_NSA]
        scores, keeps = [], []
        has_new = jnp.int32(0)
        for r in range(N_SEL):
            n = idx_ref[b, g * N_SEL + r]
            past = ((n >= 0) & (n < nb_past)).astype(jnp.int32)
            keep = (blk_in_page == n % bpp) & (past > 0)
            has_new = has_new | (n == nb_past).astype(jnp.int32)
            keeps.append(keep)
            scores.append(jnp.where(keep, _dot(q4b, kbuf[slot, g, r].astype(BF16)) * scale, NEG))
        s_new = jnp.sum(bfr(q4) * bfr(new(2)), axis=-1, keepdims=True) * scale
        s_new = jnp.where(has_new > 0, s_new, NEG)
        m = s_new
        for s in scores:
            m = jnp.maximum(m, jnp.max(s, axis=-1, keepdims=True))
        p_new = jnp.where(s_new > 0.5 * NEG, jnp.exp(s_new - m), 0.0)
        l = p_new
        acc = bfr(p_new) * bfr(new(3))
        for r in range(N_SEL):
            p = jnp.where(scores[r] > 0.5 * NEG, jnp.exp(scores[r] - m), 0.0)
            l = l + jnp.sum(p, axis=-1, keepdims=True)
            v_t = jnp.where(keeps[r], vbuf[slot, g, r], 0.0).astype(BF16)
            acc = acc + _dot_nt(p.astype(BF16), v_t)
        o_slc = acc / jnp.maximum(l, 1e-30)
        wk = wk_ref[0, g].astype(BF16)
        wv = wv_ref[0, g].astype(BF16)
        s = jnp.where(w_ok, _dot(q4b, wk) * scale, NEG)
        s_new = jnp.sum(bfr(q4) * bfr(new(4)), axis=-1, keepdims=True) * scale
        m = jnp.maximum(jnp.max(s, axis=-1, keepdims=True), s_new)
        p = jnp.where(s > 0.5 * NEG, jnp.exp(s - m), 0.0)
        p_new = jnp.exp(s_new - m)
        l = jnp.sum(p, axis=-1, keepdims=True) + p_new
        o_win = (_dot_nt(p.astype(BF16), wv) + bfr(p_new) * bfr(new(5))) / jnp.maximum(l, 1e-30)
        for r in range(R_NSA):
            h = g * R_NSA + r
            pieces.append(gt[:, 3 * h:3 * h + 1] * ocmp[:, h * HD_NSA:(h + 1) * HD_NSA]
                          + gt[:, 3 * h + 1:3 * h + 2] * o_slc[r:r + 1]
                          + gt[:, 3 * h + 2:3 * h + 3] * o_win[r:r + 1])
    o_ref[0] = jnp.concatenate(pieces, axis=-1)


def _nsa1_attn_call(idx, page_table, nq, kv_new, gates, ocmp, win_k, win_v, slc_k, slc_v, q_pos, past_len):
    DB, n_pages = page_table.shape
    wb = win_k.shape[3]
    r3 = lambda a: a.reshape(DB, 1, a.shape[-1])
    row = lambda w: pl.BlockSpec((1, 1, w), lambda b, *_: (b, 0, 0))
    win = pl.BlockSpec((1, G_NSA, HD_NSA, wb), lambda b, *_: (b, 0, 0, 0))
    o = pl.pallas_call(
        functools.partial(_nsa1_attn_kernel, n_pages=n_pages, q_pos=q_pos, past_len=past_len),
        grid_spec=pltpu.PrefetchScalarGridSpec(
            num_scalar_prefetch=2,
            grid=(DB,),
            in_specs=[row(D_NSA), row(6 * D_KV), row(LANES), row(D_NSA), win, win,
                      pl.BlockSpec(memory_space=pl.ANY), pl.BlockSpec(memory_space=pl.ANY)],
            out_specs=row(D_NSA),
            scratch_shapes=[pltpu.VMEM((2, G_NSA, N_SEL, HD_NSA, PAGE_SIZE), F32),
                            pltpu.VMEM((2, G_NSA, N_SEL, HD_NSA, PAGE_SIZE), F32),
                            pltpu.SemaphoreType.DMA((2, 2))]),
        out_shape=jax.ShapeDtypeStruct((DB, 1, D_NSA), F32),
        compiler_params=_cparams(("arbitrary",)),
        name="nsa_step_attend",
    )(idx, page_table, r3(nq), r3(kv_new), r3(gates), r3(ocmp), win_k, win_v, slc_k, slc_v)
    return o.reshape(DB, D_NSA).astype(BF16)


def _token_tail(yret, onsa, h, mem_attend, lw, tm):
    w_out, g_mem, w_mq, w_mo, g_ffn, wq_h, sk_pad, u, v, g_final = lw
    h1, mq = _mixout_call(yret, onsa, h, w_out, g_mem, w_mq, tm)
    omem = mem_attend(mq)
    h2, xn, idx, gw = _route_call(omem, h1, w_mo, g_ffn, wq_h, sk_pad, tm)
    return _peer_call(idx, xn, gw, h2, g_final, u, v, min(PEER_TT, h.shape[0]))


def _all_tables(pos):
    return (_rot_tables(pos, H_RET, HD_RET, HD_RET, RET_THETA)
            + _rot_tables(pos, H_NSA, HD_NSA, ROPE_DIMS, ROPE_THETA)
            + _rot_tables(pos, G_NSA, HD_NSA, ROPE_DIMS, ROPE_THETA))


def kernel(x_prompt, x_sample, mem_prompt, state_ret, cache_cmp_k, cache_cmp_v, cache_slc_k, cache_slc_v,
           cache_win_k, cache_win_v, cache_mem_k, cache_mem_v, page_table, norm_mix_g, w_in, ret_gn_g,
           cmp_w, cmp_b, w_out, norm_mem_g, mem_norm_g, w_mq, w_mk, w_mv, w_mo, norm_ffn_g,
           peer_wq, peer_subkeys, peer_u, peer_v, norm_final_g):
    B, L, D = x_prompt.shape
    DB, LS, _ = x_sample.shape
    n_mem = mem_prompt.shape[1]
    n_pages = page_table.shape[1]
    past_len = n_pages * PAGE_SIZE
    assert w_in.shape[0] == 1 and LS == 1 and D == D_MODEL
    l = 0
    tm = 256

    w_all = _prep_w_in(w_in[l])
    cw_bd = _prep_cmp_w(cmp_w[l])
    cb2 = jnp.tile(cmp_b[l], (1, G_NSA)).reshape(2, 1, LANES)
    w_kv = jnp.concatenate([w_mk[l], w_mv[l]], axis=1).astype(BF16)
    wq_h = peer_wq[l].reshape(D, PEER_HEADS, PEER_DKEY).transpose(1, 0, 2).astype(BF16)
    lw = (w_out[l].astype(BF16), norm_mem_g[l], w_mq[l].astype(BF16), w_mo[l].astype(BF16), norm_ffn_g[l],
          wq_h, _prep_peer_keys(peer_subkeys[l]), _pack_table(peer_u[l]), _pack_table(peer_v[l]), norm_final_g)

    xs = x_sample.reshape(DB, D)
    rq, rk, rv, rg, nq, kvs, _, gt = _proj_call(xs, norm_mix_g[l], w_all,
                                                _all_tables(jnp.full((DB,), past_len, jnp.int32)), DB)
    yret, s_state = _ret1_call(rq, rk, rv, rg, ret_gn_g[l], state_ret[l])
    fm = lambda cache: jnp.swapaxes(cache[l], -1, -2)
    cw_pg = _prep_cmp_w(cmp_w[l].transpose(0, 2, 1, 3))
    tbl_k = _cmp_pages_call(fm(cache_cmp_k), cw_pg[0], cmp_b[l, 0])
    tbl_v = _cmp_pages_call(fm(cache_cmp_v), cw_pg[1], cmp_b[l, 1])
    ocmp, sel_idx = _nsa1_cmp_call(page_table, nq, kvs[:, 0:D_KV], kvs[:, D_KV:2 * D_KV], tbl_k, tbl_v, cw_bd, cb2,
                                   past_len)
    onsa = _nsa1_attn_call(sel_idx, page_table, nq, kvs, gt, ocmp, fm(cache_win_k), fm(cache_win_v),
                           fm(cache_slc_k), fm(cache_slc_v), past_len, past_len)
    y_s = _token_tail(yret, onsa, xs, lambda mq: _memattn1_call(mq, cache_mem_k[l], cache_mem_v[l]), lw, DB)

    new = lambda i: kvs[:, i * D_KV:(i + 1) * D_KV].reshape(DB, G_NSA, 1, HD_NSA)
    wb = cache_win_k.shape[3]
    keep_s = min(WINDOW, wb + 1)
    win = lambda cache, i: jnp.concatenate([cache[l], new(i)], axis=2)[:, :, wb + 1 - keep_s:]

    xp = x_prompt.reshape(B * L, D)
    rq, rk, rv, rg, nq, kv, kvb, gt = _proj_call(xp, norm_mix_g[l], w_all,
                                                 _all_tables(jnp.arange(L, dtype=jnp.int32)), tm)
    yret, p_state = _ret_call(rq, rk, rv, rg, ret_gn_g[l], B, L)
    onsa = _nsa_call(nq, kv, kvb, gt, cw_bd, cb2, B, L)
    mkv, mkvb = _normmm_call(mem_prompt.reshape(B * n_mem, D), mem_norm_g[l], w_kv, tm)
    y_p = _token_tail(yret, onsa, xp, lambda mq: _memattn_call(mq, mkvb, B, L, n_mem, tm), lw, tm)

    kv6 = kv.reshape(B, L, 6, G_NSA, HD_NSA)
    pages = lambda i: kv6[:, :, i].reshape(B, L // PAGE_SIZE, PAGE_SIZE, G_NSA, HD_NSA).transpose(0, 1, 3, 2, 4)
    keep = min(WINDOW, L)
    tail = lambda i: kv6[:, L - keep:, i].transpose(0, 2, 1, 3)
    mem4 = lambda t: t.reshape(B, n_mem, H_MEM, HD_MEM)
    st = lambda t: t[None]
    return (y_p.reshape(B, L, D), y_s.reshape(DB, 1, D), st(p_state),
            st(pages(0)), st(pages(1)), st(pages(2)), st(pages(3)), st(tail(4)), st(tail(5)),
            st(mem4(mkv[:, :D_MEM])), st(mem4(mkv[:, D_MEM:])), st(s_state),
            st(new(0)), st(new(1)), st(new(2)), st(new(3)), st(win(cache_win_k, 4)), st(win(cache_win_v, 5)))
```
